```python
import jax, jax.numpy as jnp
from jax import lax
import numpy as np

D_MODEL = 1024
BATCH = 2
SEQ = 8192
DEPTH = 2

CHUNK = 64
D_POOL = D_MODEL // 4
POOL_WINDOWS = (2, 4, 8, 16)
N_POOL_GROUPS = len(POOL_WINDOWS)
POOL_GROUP = D_POOL // N_POOL_GROUPS
D_GMLP = 3 * D_MODEL // 8
GMLP_HEADS = 4
GMLP_HEAD_DIM = D_GMLP // GMLP_HEADS
GMLP_BLOCK = 128
D_CONV = D_MODEL - D_POOL - D_GMLP
CONV_WIDTH = 31
D_MIX = D_POOL + D_GMLP + D_CONV
D_IN = D_POOL + 2 * D_GMLP + 2 * D_CONV
D_FF = 2816
N_EXPERTS = 8
TOP_K = 2
D_FF_EXPERT = 3584
N_DENSE = (DEPTH + 1) // 2
N_MOE = DEPTH // 2
EPS = 1e-6

kernel_name = "hybrid_pool_gmlp_conv_moe_trunk"


def rms_norm(x, g):
    xf = x.astype(jnp.float32)
    y = xf * lax.rsqrt(jnp.mean(xf * xf, axis=-1, keepdims=True) + EPS)
    return (y * g.astype(jnp.float32)).astype(x.dtype)


def layer_norm(x, g, b):
    xf = x.astype(jnp.float32)
    mu = jnp.mean(xf, axis=-1, keepdims=True)
    var = jnp.mean(jnp.square(xf - mu), axis=-1, keepdims=True)
    y = (xf - mu) * lax.rsqrt(var + EPS)
    return (y * g.astype(jnp.float32) + b.astype(jnp.float32)).astype(x.dtype)


def pool_mixer(a, w, scale):
    B, S, _ = a.shape
    af = a.astype(jnp.float32)
    csum = jnp.cumsum(af, axis=1)
    t = jnp.arange(1, S + 1, dtype=jnp.float32)
    outs = []
    for gi, win in enumerate(POOL_WINDOWS):
        sl = slice(gi * POOL_GROUP, (gi + 1) * POOL_GROUP)
        c = csum[..., sl]
        c_prev = jnp.pad(c[:, :-win], ((0, 0), (win, 0), (0, 0)))
        mean = (c - c_prev) / jnp.minimum(t, float(win))[None, :, None]
        outs.append(mean - af[..., sl])
    y = jnp.stack(outs, axis=2).astype(a.dtype)
    y = jnp.einsum('bsgc,gcd->bsgd', y, w).reshape(B, S, D_POOL)
    return y * scale


def gmlp_mixer(u, v, g, ws, b):
    B, S, _ = u.shape
    v = rms_norm(v, g)
    n_blk = S // GMLP_BLOCK
    vb = v.reshape(B, n_blk, GMLP_BLOCK, GMLP_HEADS, GMLP_HEAD_DIM)
    ci = jnp.arange(GMLP_BLOCK) // CHUNK
    mask = ci[:, None] >= ci[None, :]
    ws_m = jnp.where(mask[None], ws, jnp.zeros_like(ws))
    z = jnp.einsum('hij,bnjhc->bnihc', ws_m, vb) + b.T[None, None, :, :, None]
    return u * z.reshape(B, S, D_GMLP)


def conv_module(val, gate, w_dw, b_dw, ln_g, ln_b, w_pw, b_pw):
    h = val * jax.nn.sigmoid(gate)
    h = lax.conv_general_dilated(
        h, w_dw[:, None, :], window_strides=(1,),
        padding=((CONV_WIDTH - 1, 0),),
        dimension_numbers=('NWC', 'WIO', 'NWC'),
        feature_group_count=D_CONV) + b_dw
    h = jax.nn.silu(layer_norm(h, ln_g, ln_b))
    return h @ w_pw + b_pw


def swiglu(h, wg, wu, wd):
    return (jax.nn.silu(h @ wg) * (h @ wu)) @ wd


def moe_swiglu(h, router, wg, wu, wd):
    B, S, D = h.shape
    t = h.reshape(B * S, D)
    logits = t.astype(jnp.float32) @ router.astype(jnp.float32)
    top_v, top_i = lax.top_k(logits, TOP_K)
    top_w = jax.nn.softmax(top_v, axis=-1)
    gates = jnp.sum(jax.nn.one_hot(top_i, N_EXPERTS, dtype=jnp.float32)
                    * top_w[..., None], axis=1)
    out = jnp.zeros_like(t)
    for e in range(N_EXPERTS):
        out = out + gates[:, e:e + 1].astype(t.dtype) * swiglu(t, wg[e], wu[e], wd[e])
    return out.reshape(B, S, D)


def setup_inputs(seed: int = 0) -> dict:
    key = jax.random.key(seed)
    k = jax.random.split(key, 26)
    f32 = jnp.float32
    nrm = lambda kk, shape, s: jax.random.normal(kk, shape, f32) * s
    return {
        "x": nrm(k[0], (BATCH, SEQ, D_MODEL), 1.0),
        "norm1_g": 1.0 + nrm(k[1], (DEPTH, D_MODEL), 0.05),
        "w_in": nrm(k[2], (DEPTH, D_MODEL, D_IN), D_MODEL ** -0.5),
        "pool_w": nrm(k[3], (DEPTH, N_POOL_GROUPS, POOL_GROUP, POOL_GROUP), POOL_GROUP ** -0.5),
        "pool_scale": 1.0 + nrm(k[4], (DEPTH, D_POOL), 0.1),
        "gm_norm_g": 1.0 + nrm(k[5], (DEPTH, D_GMLP), 0.05),
        "gm_ws": nrm(k[6], (DEPTH, GMLP_HEADS, GMLP_BLOCK, GMLP_BLOCK), 0.5 * GMLP_BLOCK ** -0.5),
        "gm_b": 1.0 + nrm(k[7], (DEPTH, GMLP_HEADS, GMLP_BLOCK), 0.1),
        "conv_dw_w": nrm(k[8], (DEPTH, CONV_WIDTH, D_CONV), CONV_WIDTH ** -0.5),
        "conv_dw_b": nrm(k[9], (DEPTH, D_CONV), 0.02),
        "conv_ln_g": 1.0 + nrm(k[10], (DEPTH, D_CONV), 0.05),
        "conv_ln_b": nrm(k[11], (DEPTH, D_CONV), 0.02),
        "conv_pw_w": nrm(k[12], (DEPTH, D_CONV, D_CONV), D_CONV ** -0.5),
        "conv_pw_b": nrm(k[13], (DEPTH, D_CONV), 0.02),
        "w_out": nrm(k[14], (DEPTH, D_MIX, D_MODEL), D_MIX ** -0.5),
        "norm2_g": 1.0 + nrm(k[15], (DEPTH, D_MODEL), 0.05),
        "ffn_wg": nrm(k[16], (N_DENSE, D_MODEL, D_FF), D_MODEL ** -0.5),
        "ffn_wu": nrm(k[17], (N_DENSE, D_MODEL, D_FF), D_MODEL ** -0.5),
        "ffn_wd": nrm(k[18], (N_DENSE, D_FF, D_MODEL), D_FF ** -0.5),
        "moe_router": nrm(k[19], (N_MOE, D_MODEL, N_EXPERTS), D_MODEL ** -0.5),
        "moe_wg": nrm(k[20], (N_MOE, N_EXPERTS, D_MODEL, D_FF_EXPERT), D_MODEL ** -0.5),
        "moe_wu": nrm(k[21], (N_MOE, N_EXPERTS, D_MODEL, D_FF_EXPERT), D_MODEL ** -0.5),
        "moe_wd": nrm(k[22], (N_MOE, N_EXPERTS, D_FF_EXPERT, D_MODEL), D_FF_EXPERT ** -0.5),
        "final_g": 1.0 + nrm(k[23], (D_MODEL,), 0.05),
    }


def reference(x, norm1_g, w_in, pool_w, pool_scale, gm_norm_g, gm_ws, gm_b,
              conv_dw_w, conv_dw_b, conv_ln_g, conv_ln_b, conv_pw_w, conv_pw_b,
              w_out, norm2_g, ffn_wg, ffn_wu, ffn_wd,
              moe_router, moe_wg, moe_wu, moe_wd, final_g):
    splits = [D_POOL, D_POOL + D_GMLP, D_POOL + 2 * D_GMLP,
              D_POOL + 2 * D_GMLP + D_CONV]
    for l in range(DEPTH):
        h = rms_norm(x, norm1_g[l])
        p = h @ w_in[l]
        a, u, v, cv, cg = jnp.split(p, splits, axis=-1)
        ya = pool_mixer(a, pool_w[l], pool_scale[l])
        yb = gmlp_mixer(u, v, gm_norm_g[l], gm_ws[l], gm_b[l])
        yc = conv_module(cv, cg, conv_dw_w[l], conv_dw_b[l], conv_ln_g[l],
                         conv_ln_b[l], conv_pw_w[l], conv_pw_b[l])
        x = x + jnp.concatenate([ya, yb, yc], axis=-1) @ w_out[l]
        h = rms_norm(x, norm2_g[l])
        if l % 2 == 0:
            j = l // 2
            x = x + swiglu(h, ffn_wg[j], ffn_wu[j], ffn_wd[j])
        else:
            j = l // 2
            x = x + moe_swiglu(h, moe_router[j], moe_wg[j], moe_wu[j], moe_wd[j])
    return rms_norm(x, final_g)
```

```python
import functools

import jax
import jax.numpy as jnp
from jax import lax
from jax.experimental import pallas as pl
from jax.experimental.pallas import tpu as pltpu

D_MODEL = 1024
CHUNK = 64
D_POOL = 256
POOL_WINDOWS = (2, 4, 8, 16)
POOL_GROUP = 64
D_GMLP = 384
GMLP_HEADS = 4
GMLP_HEAD_DIM = 96
GMLP_BLOCK = 128
D_CONV = 384
CONV_WIDTH = 31
D_IN = D_POOL + 2 * D_GMLP + 2 * D_CONV
D_FF = 2816
N_EXPERTS = 8
D_FF_EXPERT = 3584
EPS = 1e-6

V7X_VMEM_LIMIT_BYTES = 56 * 1024 * 1024

POOL_HALO = 16
CONV_HALO = 32
MIX_TM = 512
CONV_ROWS = 64
FFN_TM = 512
MOE_TM = 1024
MOE_FK = 512

BF16 = jnp.bfloat16
F32 = jnp.float32


def _rms(x, g):
    return x * lax.rsqrt(jnp.mean(x * x, axis=-1, keepdims=True) + EPS) * g


def _silu(x):
    return x * jax.nn.sigmoid(x)


def _dot(a, b):
    return jnp.dot(a, b, preferred_element_type=F32)


def _mixer_kernel(x_ref, g1_ref, win_ref, poolw_ref, pools_ref, gmg_ref,
                  gmws_ref, gmb_ref, dww_ref, dwb_ref, lng_ref, lnb_ref,
                  pww_ref, pwb_ref, wout_ref, o_ref, abuf, hbuf, ycat):
    tm = x_ref.shape[1]
    s = pl.program_id(1)

    @pl.when(s == 0)
    def _():
        abuf[0:POOL_HALO, :] = jnp.zeros((POOL_HALO, D_POOL), F32)
        hbuf[0:CONV_HALO, :] = jnp.zeros((CONV_HALO, D_CONV), F32)

    x = x_ref[0]
    h = _rms(x, g1_ref[...])
    p = _dot(h.astype(BF16), win_ref[...])
    o1 = D_POOL
    o2 = o1 + D_GMLP
    o3 = o2 + D_GMLP
    o4 = o3 + D_CONV
    a = p[:, :o1]
    u = p[:, o1:o2]
    v = p[:, o2:o3]
    cv = p[:, o3:o4]
    cg = p[:, o4:]

    abuf[POOL_HALO:POOL_HALO + tm, :] = a

    def shifted(j, lo):
        return abuf[POOL_HALO - j:POOL_HALO - j + tm, lo:lo + 128]

    lane = lax.broadcasted_iota(jnp.int32, (1, 128), 1)
    row = lax.broadcasted_iota(jnp.int32, (tm, 1), 0)
    tpos = (s * tm + row + 1).astype(F32)
    ys = []
    for half, (w_small, w_big) in enumerate(((2, 4), (8, 16))):
        lo = half * 128
        acc = shifted(0, lo)
        for j in range(1, w_small):
            acc = acc + shifted(j, lo)
        s_small = acc
        for j in range(w_small, w_big):
            acc = acc + shifted(j, lo)
        first = lane < POOL_GROUP
        ssel = jnp.where(first, s_small, acc)
        win = jnp.where(first, float(w_small), float(w_big))
        mean = ssel / jnp.minimum(tpos, win)
        ys.append(mean - a[:, lo:lo + 128])
    y = jnp.concatenate(ys, axis=1).astype(BF16)
    ya = _dot(y, poolw_ref[...]) * pools_ref[...]
    ycat[:, 0:o1] = ya.astype(BF16)
    abuf[0:POOL_HALO, :] = abuf[tm:tm + POOL_HALO, :]

    vn = _rms(v, gmg_ref[...])
    ri = lax.broadcasted_iota(jnp.int32, (GMLP_BLOCK, GMLP_HEADS * GMLP_BLOCK), 0)
    cj = lax.broadcasted_iota(jnp.int32, (GMLP_BLOCK, GMLP_HEADS * GMLP_BLOCK), 1)
    causal = (ri // CHUNK) >= ((cj % GMLP_BLOCK) // CHUNK)
    wcat = jnp.where(causal, gmws_ref[...], 0.0).astype(BF16)
    hid = lax.broadcasted_iota(jnp.int32, (1, D_GMLP), 1) // GMLP_HEAD_DIM
    gmb = gmb_ref[...]
    for blk in range(tm // GMLP_BLOCK):
        r0 = blk * GMLP_BLOCK
        vb = vn[r0:r0 + GMLP_BLOCK, :]
        vstack = jnp.concatenate(
            [jnp.where(hid == hh, vb, 0.0) for hh in range(GMLP_HEADS)],
            axis=0).astype(BF16)
        z = _dot(wcat, vstack) + gmb
        ycat[r0:r0 + GMLP_BLOCK, o1:o2] = (u[r0:r0 + GMLP_BLOCK, :] * z).astype(BF16)

    hbuf[CONV_HALO:CONV_HALO + tm, :] = cv * jax.nn.sigmoid(cg)
    base = CONV_HALO - (CONV_WIDTH - 1)
    dwb = dwb_ref[...]
    lng = lng_ref[...]
    lnb = lnb_ref[...]
    pwb = pwb_ref[...]
    for c0 in range(0, tm, CONV_ROWS):
        acc = dww_ref[0:1, :] * hbuf[base + c0:base + c0 + CONV_ROWS, :]
        for j in range(1, CONV_WIDTH):
            acc = acc + dww_ref[j:j + 1, :] * hbuf[base + c0 + j:base + c0 + j + CONV_ROWS, :]
        acc = acc + dwb
        mu = jnp.mean(acc, axis=-1, keepdims=True)
        cen = acc - mu
        var = jnp.mean(cen * cen, axis=-1, keepdims=True)
        ln = cen * lax.rsqrt(var + EPS) * lng + lnb
        yc = _dot(_silu(ln).astype(BF16), pww_ref[...]) + pwb
        ycat[c0:c0 + CONV_ROWS, o2:D_MODEL] = yc.astype(BF16)
    hbuf[0:CONV_HALO, :] = hbuf[tm:tm + CONV_HALO, :]

    o_ref[0] = x + _dot(ycat[...], wout_ref[...])


def _const_spec(shape):
    nd = len(shape)
    return pl.BlockSpec(shape, lambda b, s: (0,) * nd)


def _mixer(x, g1, w_in, pool_bd, pool_scale, gm_g, gm_wcat, gm_bias, dw_w, dw_b,
           ln_g, ln_b, pw_w, pw_b, w_out):
    B, S, D = x.shape
    tm = MIX_TM
    weights = (g1, w_in, pool_bd, pool_scale, gm_g, gm_wcat, gm_bias, dw_w, dw_b,
               ln_g, ln_b, pw_w, pw_b, w_out)
    return pl.pallas_call(
        _mixer_kernel,
        out_shape=jax.ShapeDtypeStruct((B, S, D), F32),
        grid=(B, S // tm),
        in_specs=[pl.BlockSpec((1, tm, D), lambda b, s: (b, s, 0))]
        + [_const_spec(w.shape) for w in weights],
        out_specs=pl.BlockSpec((1, tm, D), lambda b, s: (b, s, 0)),
        scratch_shapes=[
            pltpu.VMEM((POOL_HALO + tm, D_POOL), F32),
            pltpu.VMEM((CONV_HALO + tm, D_CONV), F32),
            pltpu.VMEM((tm, D_MODEL), BF16),
        ],
        compiler_params=pltpu.CompilerParams(
            dimension_semantics=("arbitrary", "arbitrary"),
            vmem_limit_bytes=V7X_VMEM_LIMIT_BYTES),
        name="token_mixer",
    )(x, *weights)


def _ffn_kernel(x_ref, g_ref, wg_ref, wu_ref, wd_ref, o_ref):
    x = x_ref[...]
    h = _rms(x, g_ref[...]).astype(BF16)
    gate = _dot(h, wg_ref[...])
    up = _dot(h, wu_ref[...])
    act = (_silu(gate) * up).astype(BF16)
    o_ref[...] = x + _dot(act, wd_ref[...])


def _ffn(x, g, wg, wu, wd):
    N, D = x.shape
    tm = FFN_TM
    const = lambda shape: pl.BlockSpec(shape, lambda i: (0, 0))
    return pl.pallas_call(
        _ffn_kernel,
        out_shape=jax.ShapeDtypeStruct((N, D), F32),
        grid=(N // tm,),
        in_specs=[pl.BlockSpec((tm, D), lambda i: (i, 0)), const(g.shape),
                  const(wg.shape), const(wu.shape), const(wd.shape)],
        out_specs=pl.BlockSpec((tm, D), lambda i: (i, 0)),
        compiler_params=pltpu.CompilerParams(
            dimension_semantics=("arbitrary",),
            vmem_limit_bytes=V7X_VMEM_LIMIT_BYTES),
        name="dense_swiglu",
    )(x, g, wg, wu, wd)


def _moe_kernel(x_ref, g_ref, router_ref, wg_ref, wu_ref, wd_ref, gf_ref, o_ref,
                h_scr, gates_scr, acc_scr):
    e = pl.program_id(1)
    k = pl.program_id(2)
    last_e = pl.num_programs(1) - 1
    last_k = pl.num_programs(2) - 1

    @pl.when((e == 0) & (k == 0))
    def _():
        h = _rms(x_ref[...], g_ref[...])
        h_scr[...] = h.astype(BF16)
        logits = jnp.dot(h, router_ref[...], preferred_element_type=F32,
                         precision=lax.Precision.HIGHEST)
        lane = lax.broadcasted_iota(jnp.int32, logits.shape, 1)
        m1 = jnp.max(logits, axis=-1, keepdims=True)
        i1 = jnp.min(jnp.where(logits == m1, lane, N_EXPERTS), axis=-1, keepdims=True)
        rest = jnp.where(lane == i1, -jnp.inf, logits)
        m2 = jnp.max(rest, axis=-1, keepdims=True)
        i2 = jnp.min(jnp.where(rest == m2, lane, N_EXPERTS), axis=-1, keepdims=True)
        e2 = jnp.exp(m2 - m1)
        w1 = 1.0 / (1.0 + e2)
        w2 = e2 / (1.0 + e2)
        gates_scr[...] = jnp.where(lane == i1, w1, 0.0) + jnp.where(lane == i2, w2, 0.0)
        acc_scr[...] = jnp.zeros_like(acc_scr)

    lane = lax.broadcasted_iota(jnp.int32, gates_scr.shape, 1)
    gate_col = jnp.sum(jnp.where(lane == e, gates_scr[...], 0.0), axis=-1, keepdims=True)
    h = h_scr[...]
    gate = _dot(h, wg_ref[0].astype(BF16))
    up = _dot(h, wu_ref[0].astype(BF16))
    act = (_silu(gate) * up).astype(BF16)
    acc_scr[...] += gate_col * _dot(act, wd_ref[0].astype(BF16))

    @pl.when((e == last_e) & (k == last_k))
    def _():
        o_ref[...] = _rms(x_ref[...] + acc_scr[...], gf_ref[...])


def _moe(x, g, router, wg, wu, wd, gf):
    N, D = x.shape
    tm, fk = MOE_TM, MOE_FK
    E, _, F = wg.shape
    return pl.pallas_call(
        _moe_kernel,
        out_shape=jax.ShapeDtypeStruct((N, D), F32),
        grid=(N // tm, E, F // fk),
        in_specs=[
            pl.BlockSpec((tm, D), lambda i, e, k: (i, 0)),
            pl.BlockSpec(g.shape, lambda i, e, k: (0, 0)),
            pl.BlockSpec(router.shape, lambda i, e, k: (0, 0)),
            pl.BlockSpec((1, D, fk), lambda i, e, k: (e, 0, k)),
            pl.BlockSpec((1, D, fk), lambda i, e, k: (e, 0, k)),
            pl.BlockSpec((1, fk, D), lambda i, e, k: (e, k, 0)),
            pl.BlockSpec(gf.shape, lambda i, e, k: (0, 0)),
        ],
        out_specs=pl.BlockSpec((tm, D), lambda i, e, k: (i, 0)),
        scratch_shapes=[
            pltpu.VMEM((tm, D), BF16),
            pltpu.VMEM((tm, N_EXPERTS), F32),
            pltpu.VMEM((tm, D), F32),
        ],
        compiler_params=pltpu.CompilerParams(
            dimension_semantics=("arbitrary", "arbitrary", "arbitrary"),
            vmem_limit_bytes=V7X_VMEM_LIMIT_BYTES),
        name="moe_swiglu",
    )(x, g, router, wg, wu, wd, gf)


def kernel(x, norm1_g, w_in, pool_w, pool_scale, gm_norm_g, gm_ws, gm_b,
           conv_dw_w, conv_dw_b, conv_ln_g, conv_ln_b, conv_pw_w, conv_pw_b,
           w_out, norm2_g, ffn_wg, ffn_wu, ffn_wd,
           moe_router, moe_wg, moe_wu, moe_wd, final_g):
    B, S, D = x.shape
    depth = w_in.shape[0]
    row = lambda t: t.reshape(1, -1)
    for l in range(depth):
        pool_bd = jax.scipy.linalg.block_diag(*[pool_w[l, gi] for gi in range(len(POOL_WINDOWS))])
        gm_wcat = jnp.transpose(gm_ws[l], (1, 0, 2)).reshape(GMLP_BLOCK, GMLP_HEADS * GMLP_BLOCK)
        gm_bias = jnp.repeat(gm_b[l].T, GMLP_HEAD_DIM, axis=1)
        x = _mixer(x, row(norm1_g[l]), w_in[l].astype(BF16), pool_bd.astype(BF16),
                   row(pool_scale[l]), row(gm_norm_g[l]), gm_wcat, gm_bias,
                   conv_dw_w[l], row(conv_dw_b[l]), row(conv_ln_g[l]), row(conv_ln_b[l]),
                   conv_pw_w[l].astype(BF16), row(conv_pw_b[l]), w_out[l].astype(BF16))
        xf = x.reshape(B * S, D)
        j = l // 2
        if l % 2 == 0:
            xf = _ffn(xf, row(norm2_g[l]), ffn_wg[j].astype(BF16), ffn_wu[j].astype(BF16),
                      ffn_wd[j].astype(BF16))
        else:
            xf = _moe(xf, row(norm2_g[l]), moe_router[j], moe_wg[j], moe_wu[j], moe_wd[j],
                      row(final_g))
        x = xf.reshape(B, S, D)
    return x
```

```python
import functools

import jax
import jax.numpy as jnp
from jax import lax
from jax.experimental import pallas as pl
from jax.experimental.pallas import tpu as pltpu

D_MODEL = 1024
CHUNK = 64
D_POOL = 256
POOL_WINDOWS = (2, 4, 8, 16)
POOL_GROUP = 64
D_GMLP = 384
GMLP_HEADS = 4
GMLP_HEAD_DIM = 96
GMLP_BLOCK = 128
D_CONV = 384
CONV_WIDTH = 31
D_IN = D_POOL + 2 * D_GMLP + 2 * D_CONV
D_FF = 2816
N_EXPERTS = 8
D_FF_EXPERT = 3584
EPS = 1e-6

V7X_VMEM_LIMIT_BYTES = 56 * 1024 * 1024

POOL_HALO = 16
CONV_HALO = 32
MIX_TM = 512
CONV_ROWS = 64
FFN_TM = 512
ROUTE_T = 512
EXPERT_TM = 1024
EXPERT_FK = 512

BF16 = jnp.bfloat16
F32 = jnp.float32


def _rms(x, g):
    return x * lax.rsqrt(jnp.mean(x * x, axis=-1, keepdims=True) + EPS) * g


def _silu(x):
    return x * jax.nn.sigmoid(x)


def _dot(a, b):
    return jnp.dot(a, b, preferred_element_type=F32)


def _mixer_kernel(x_ref, g1_ref, win_ref, poolw_ref, pools_ref, gmg_ref,
                  gmws_ref, gmb_ref, dww_ref, dwb_ref, lng_ref, lnb_ref,
                  pww_ref, pwb_ref, wout_ref, o_ref, abuf, hbuf, ycat):
    tm = x_ref.shape[1]
    s = pl.program_id(1)

    @pl.when(s == 0)
    def _():
        abuf[0:POOL_HALO, :] = jnp.zeros((POOL_HALO, D_POOL), F32)
        hbuf[0:CONV_HALO, :] = jnp.zeros((CONV_HALO, D_CONV), F32)

    x = x_ref[0]
    h = _rms(x, g1_ref[...])
    p = _dot(h.astype(BF16), win_ref[...])
    o1 = D_POOL
    o2 = o1 + D_GMLP
    o3 = o2 + D_GMLP
    o4 = o3 + D_CONV
    a = p[:, :o1]
    u = p[:, o1:o2]
    v = p[:, o2:o3]
    cv = p[:, o3:o4]
    cg = p[:, o4:]

    abuf[POOL_HALO:POOL_HALO + tm, :] = a

    def shifted(j, lo):
        return abuf[POOL_HALO - j:POOL_HALO - j + tm, lo:lo + 128]

    lane = lax.broadcasted_iota(jnp.int32, (1, 128), 1)
    row = lax.broadcasted_iota(jnp.int32, (tm, 1), 0)
    tpos = (s * tm + row + 1).astype(F32)
    ys = []
    for half, (w_small, w_big) in enumerate(((2, 4), (8, 16))):
        lo = half * 128
        acc = shifted(0, lo)
        for j in range(1, w_small):
            acc = acc + shifted(j, lo)
        s_small = acc
        for j in range(w_small, w_big):
            acc = acc + shifted(j, lo)
        first = lane < POOL_GROUP
        ssel = jnp.where(first, s_small, acc)
        win = jnp.where(first, float(w_small), float(w_big))
        mean = ssel / jnp.minimum(tpos, win)
        ys.append(mean - a[:, lo:lo + 128])
    y = jnp.concatenate(ys, axis=1).astype(BF16)
    ya = _dot(y, poolw_ref[...]) * pools_ref[...]
    ycat[:, 0:o1] = ya.astype(BF16)
    abuf[0:POOL_HALO, :] = abuf[tm:tm + POOL_HALO, :]

    vn = _rms(v, gmg_ref[...])
    ri = lax.broadcasted_iota(jnp.int32, (GMLP_BLOCK, GMLP_HEADS * GMLP_BLOCK), 0)
    cj = lax.broadcasted_iota(jnp.int32, (GMLP_BLOCK, GMLP_HEADS * GMLP_BLOCK), 1)
    causal = (ri // CHUNK) >= ((cj % GMLP_BLOCK) // CHUNK)
    wcat = jnp.where(causal, gmws_ref[...], 0.0).astype(BF16)
    hid = lax.broadcasted_iota(jnp.int32, (1, D_GMLP), 1) // GMLP_HEAD_DIM
    gmb = gmb_ref[...]
    for blk in range(tm // GMLP_BLOCK):
        r0 = blk * GMLP_BLOCK
        vb = vn[r0:r0 + GMLP_BLOCK, :]
        vstack = jnp.concatenate(
            [jnp.where(hid == hh, vb, 0.0) for hh in range(GMLP_HEADS)],
            axis=0).astype(BF16)
        z = _dot(wcat, vstack) + gmb
        ycat[r0:r0 + GMLP_BLOCK, o1:o2] = (u[r0:r0 + GMLP_BLOCK, :] * z).astype(BF16)

    hbuf[CONV_HALO:CONV_HALO + tm, :] = cv * jax.nn.sigmoid(cg)
    base = CONV_HALO - (CONV_WIDTH - 1)
    dwb = dwb_ref[...]
    lng = lng_ref[...]
    lnb = lnb_ref[...]
    pwb = pwb_ref[...]
    for c0 in range(0, tm, CONV_ROWS):
        acc = dww_ref[0:1, :] * hbuf[base + c0:base + c0 + CONV_ROWS, :]
        for j in range(1, CONV_WIDTH):
            acc = acc + dww_ref[j:j + 1, :] * hbuf[base + c0 + j:base + c0 + j + CONV_ROWS, :]
        acc = acc + dwb
        mu = jnp.mean(acc, axis=-1, keepdims=True)
        cen = acc - mu
        var = jnp.mean(cen * cen, axis=-1, keepdims=True)
        ln = cen * lax.rsqrt(var + EPS) * lng + lnb
        yc = _dot(_silu(ln).astype(BF16), pww_ref[...]) + pwb
        ycat[c0:c0 + CONV_ROWS, o2:D_MODEL] = yc.astype(BF16)
    hbuf[0:CONV_HALO, :] = hbuf[tm:tm + CONV_HALO, :]

    o_ref[0] = x + _dot(ycat[...], wout_ref[...])


def _const_spec(shape):
    nd = len(shape)
    return pl.BlockSpec(shape, lambda b, s: (0,) * nd)


def _mixer(x, g1, w_in, pool_bd, pool_scale, gm_g, gm_wcat, gm_bias, dw_w, dw_b,
           ln_g, ln_b, pw_w, pw_b, w_out):
    B, S, D = x.shape
    tm = MIX_TM
    weights = (g1, w_in, pool_bd, pool_scale, gm_g, gm_wcat, gm_bias, dw_w, dw_b,
               ln_g, ln_b, pw_w, pw_b, w_out)
    return pl.pallas_call(
        _mixer_kernel,
        out_shape=jax.ShapeDtypeStruct((B, S, D), F32),
        grid=(B, S // tm),
        in_specs=[pl.BlockSpec((1, tm, D), lambda b, s: (b, s, 0))]
        + [_const_spec(w.shape) for w in weights],
        out_specs=pl.BlockSpec((1, tm, D), lambda b, s: (b, s, 0)),
        scratch_shapes=[
            pltpu.VMEM((POOL_HALO + tm, D_POOL), F32),
            pltpu.VMEM((CONV_HALO + tm, D_CONV), F32),
            pltpu.VMEM((tm, D_MODEL), BF16),
        ],
        compiler_params=pltpu.CompilerParams(
            dimension_semantics=("arbitrary", "arbitrary"),
            vmem_limit_bytes=V7X_VMEM_LIMIT_BYTES),
        name="token_mixer",
    )(x, *weights)


def _ffn_kernel(x_ref, g_ref, wg_ref, wu_ref, wd_ref, o_ref):
    x = x_ref[...]
    h = _rms(x, g_ref[...]).astype(BF16)
    gate = _dot(h, wg_ref[...])
    up = _dot(h, wu_ref[...])
    act = (_silu(gate) * up).astype(BF16)
    o_ref[...] = x + _dot(act, wd_ref[...])


def _ffn(x, g, wg, wu, wd):
    N, D = x.shape
    tm = FFN_TM
    const = lambda shape: pl.BlockSpec(shape, lambda i: (0, 0))
    return pl.pallas_call(
        _ffn_kernel,
        out_shape=jax.ShapeDtypeStruct((N, D), F32),
        grid=(N // tm,),
        in_specs=[pl.BlockSpec((tm, D), lambda i: (i, 0)), const(g.shape),
                  const(wg.shape), const(wu.shape), const(wd.shape)],
        out_specs=pl.BlockSpec((tm, D), lambda i: (i, 0)),
        compiler_params=pltpu.CompilerParams(
            dimension_semantics=("arbitrary",),
            vmem_limit_bytes=V7X_VMEM_LIMIT_BYTES),
        name="dense_swiglu",
    )(x, g, wg, wu, wd)


E_LANES = 128
RUN_ALIGN = 8
SLOT_ROWS = 2 * ROUTE_T + RUN_ALIGN * N_EXPERTS
RUN_BITS = tuple(range((ROUTE_T // RUN_ALIGN).bit_length() - 1, -1, -1))
TAIL_BITS = tuple(range((EXPERT_TM // RUN_ALIGN).bit_length() - 2, -1, -1))


def _route_kernel(x_ref, g_ref, router_ref, h_ref, info_ref, cnt_ref):
    T = x_ref.shape[0]
    h = _rms(x_ref[...], g_ref[...])
    h_ref[...] = h.astype(BF16)
    logits = jnp.dot(h, router_ref[...], preferred_element_type=F32,
                     precision=lax.Precision.HIGHEST)
    lane = lax.broadcasted_iota(jnp.int32, (T, E_LANES), 1)
    logits = jnp.where(lane < N_EXPERTS, logits, -jnp.inf)
    m1 = jnp.max(logits, axis=-1, keepdims=True)
    i1 = jnp.min(jnp.where(logits == m1, lane, E_LANES), axis=-1, keepdims=True)
    rest = jnp.where(lane == i1, -jnp.inf, logits)
    m2 = jnp.max(rest, axis=-1, keepdims=True)
    i2 = jnp.min(jnp.where(rest == m2, lane, E_LANES), axis=-1, keepdims=True)
    e2 = jnp.exp(m2 - m1)
    w1 = 1.0 / (1.0 + e2)
    w2 = e2 / (1.0 + e2)
    oh1 = (lane == i1).astype(F32)
    oh2 = (lane == i2).astype(F32)
    oh = oh1 + oh2
    ri = lax.broadcasted_iota(jnp.int32, (T, T), 0)
    ci = lax.broadcasted_iota(jnp.int32, (T, T), 1)
    rank = _dot((ri > ci).astype(BF16), oh.astype(BF16))
    cnt = jnp.sum(oh, axis=0, keepdims=True)
    padc = jnp.floor((cnt + (RUN_ALIGN - 1.0)) * (1.0 / RUN_ALIGN)) * RUN_ALIGN
    lane1 = lax.broadcasted_iota(jnp.int32, (1, E_LANES), 1)
    start = jnp.zeros((1, E_LANES), F32)
    for e in range(N_EXPERTS - 1):
        start = start + jnp.where(lane1 > e, padc[:, e:e + 1], 0.0)
    slot = start + rank
    q1 = jnp.sum(oh1 * slot, axis=-1, keepdims=True)
    q2 = jnp.sum(oh2 * slot, axis=-1, keepdims=True)
    info_ref[...] = jnp.where(lane == 0, q1, jnp.where(lane == 1, q2, jnp.where(
        lane == 2, w1, jnp.where(lane == 3, w2, 0.0))))
    cnt_ref[0] = padc


def _route(x, g, router_p):
    N, D = x.shape
    T = ROUTE_T
    n_t = N // T
    return pl.pallas_call(
        _route_kernel,
        out_shape=(jax.ShapeDtypeStruct((N, D), BF16),
                   jax.ShapeDtypeStruct((N, E_LANES), F32),
                   jax.ShapeDtypeStruct((n_t, 1, E_LANES), F32)),
        grid=(n_t,),
        in_specs=[pl.BlockSpec((T, D), lambda t: (t, 0)),
                  pl.BlockSpec(g.shape, lambda t: (0, 0)),
                  pl.BlockSpec(router_p.shape, lambda t: (0, 0))],
        out_specs=(pl.BlockSpec((T, D), lambda t: (t, 0)),
                   pl.BlockSpec((T, E_LANES), lambda t: (t, 0)),
                   pl.BlockSpec((1, 1, E_LANES), lambda t: (t, 0, 0))),
        compiler_params=pltpu.CompilerParams(
            dimension_semantics=("arbitrary",),
            vmem_limit_bytes=V7X_VMEM_LIMIT_BYTES),
        name="moe_route",
    )(x, g, router_p)


def _run_copies(src, dst, src0, dst0, rows, bits, sem):
    m = rows // RUN_ALIGN
    off = 0
    out = []
    for b in bits:
        size = RUN_ALIGN << b
        take = (m >> b) & 1
        s0 = 0 if src0 is None else pl.multiple_of(src0 + off, RUN_ALIGN)
        d0 = pl.multiple_of(dst0 + off, RUN_ALIGN)
        out.append((take == 1, pltpu.make_async_copy(
            src.at[pl.ds(s0, size)], dst.at[pl.ds(d0, size)], sem)))
        off = off + take * size
    return out


def _start_then_wait(copies):
    for pred, cp in copies:
        @pl.when(pred)
        def _(cp=cp):
            cp.start()
    for pred, cp in copies:
        @pl.when(pred)
        def _(cp=cp):
            cp.wait()


def _slots(info, T):
    q1 = info[:, 0:1].astype(jnp.int32)
    q2 = info[:, 1:2].astype(jnp.int32)
    r = lax.broadcasted_iota(jnp.int32, (T, SLOT_ROWS), 1)
    return q1 == r, q2 == r


def _dispatch_kernel(gdst_ref, glen_ref, gsrc_ref, tail0_ref, tailn_ref,
                     h_ref, info_ref, xs_hbm, xbuf, zbuf, sem, *, slack_pieces):
    t = pl.program_id(0)
    T = h_ref.shape[0]
    hit1, hit2 = _slots(info_ref[...], T)
    onehot = (hit1 | hit2).astype(BF16)
    xbuf[...] = lax.dot_general(onehot, h_ref[...], (((0,), (0,)), ((), ())),
                                preferred_element_type=F32)
    copies = []
    for e in range(N_EXPERTS):
        j = t * N_EXPERTS + e
        copies += _run_copies(xbuf, xs_hbm, gsrc_ref[j], gdst_ref[j], glen_ref[j], RUN_BITS, sem)
    _start_then_wait(copies)

    @pl.when(t == 0)
    def _():
        zbuf[...] = jnp.zeros_like(zbuf)
        tails = []
        for e in range(N_EXPERTS):
            tails += _run_copies(zbuf, xs_hbm, None, tail0_ref[e], tailn_ref[e], TAIL_BITS, sem)
        piece = zbuf.shape[0]
        for c in range(slack_pieces):
            d0 = pl.multiple_of(tail0_ref[N_EXPERTS] + c * piece, RUN_ALIGN)
            tails.append((c < tailn_ref[N_EXPERTS], pltpu.make_async_copy(
                zbuf, xs_hbm.at[pl.ds(d0, piece)], sem)))
        _start_then_wait(tails)


def _dispatch(h, info, gdst, glen, gsrc, tail0, tailn, rows_max):
    N, D = h.shape
    T = ROUTE_T
    slack_pieces = (rows_max - 2 * N) // (EXPERT_TM // 2)
    return pl.pallas_call(
        functools.partial(_dispatch_kernel, slack_pieces=slack_pieces),
        out_shape=jax.ShapeDtypeStruct((rows_max, D), F32),
        grid_spec=pltpu.PrefetchScalarGridSpec(
            num_scalar_prefetch=5,
            grid=(N // T,),
            in_specs=[pl.BlockSpec((T, D), lambda t, *_: (t, 0)),
                      pl.BlockSpec((T, E_LANES), lambda t, *_: (t, 0))],
            out_specs=pl.BlockSpec(memory_space=pl.ANY),
            scratch_shapes=[pltpu.VMEM((SLOT_ROWS, D), F32),
                            pltpu.VMEM((EXPERT_TM // 2, D), F32),
                            pltpu.SemaphoreType.DMA]),
        compiler_params=pltpu.CompilerParams(
            dimension_semantics=("arbitrary",),
            vmem_limit_bytes=V7X_VMEM_LIMIT_BYTES),
        name="moe_dispatch",
    )(gdst, glen, gsrc, tail0, tailn, h, info)


def _experts_kernel(te_ref, nv_ref, x_ref, wg_ref, wu_ref, wd_ref, o_ref, xb, acc):
    i = pl.program_id(0)
    k = pl.program_id(1)

    @pl.when(i < nv_ref[0])
    def _():
        @pl.when(k == 0)
        def _():
            xb[...] = x_ref[...].astype(BF16)
            acc[...] = jnp.zeros_like(acc)

        h = xb[...]
        gate = _dot(h, wg_ref[0].astype(BF16))
        up = _dot(h, wu_ref[0].astype(BF16))
        act = (_silu(gate) * up).astype(BF16)
        acc[...] += _dot(act, wd_ref[0].astype(BF16))

        @pl.when(k == pl.num_programs(1) - 1)
        def _():
            o_ref[...] = acc[...]

    @pl.when((i >= nv_ref[0]) & (k == 0))
    def _():
        o_ref[...] = jnp.zeros_like(o_ref)


def _experts(xs, tile_expert, n_valid, wg, wu, wd):
    M, D = xs.shape
    tm, fk = EXPERT_TM, EXPERT_FK
    E, _, F = wg.shape
    nk = F // fk

    def row_map(i, k, te, nv):
        return (jnp.minimum(i, nv[0] - 1), 0)

    def kk(i, k, nv):
        return jnp.where(i < nv[0], k, nk - 1)

    return pl.pallas_call(
        _experts_kernel,
        out_shape=jax.ShapeDtypeStruct((M, D), F32),
        grid_spec=pltpu.PrefetchScalarGridSpec(
            num_scalar_prefetch=2,
            grid=(M // tm, nk),
            in_specs=[
                pl.BlockSpec((tm, D), row_map),
                pl.BlockSpec((1, D, fk), lambda i, k, te, nv: (te[i], 0, kk(i, k, nv))),
                pl.BlockSpec((1, D, fk), lambda i, k, te, nv: (te[i], 0, kk(i, k, nv))),
                pl.BlockSpec((1, fk, D), lambda i, k, te, nv: (te[i], kk(i, k, nv), 0)),
            ],
            out_specs=pl.BlockSpec((tm, D), lambda i, k, te, nv: (i, 0)),
            scratch_shapes=[pltpu.VMEM((tm, D), BF16), pltpu.VMEM((tm, D), F32)]),
        compiler_params=pltpu.CompilerParams(
            dimension_semantics=("arbitrary", "arbitrary"),
            vmem_limit_bytes=V7X_VMEM_LIMIT_BYTES),
        name="moe_experts",
    )(tile_expert, n_valid, xs, wg, wu, wd)


def _combine_kernel(gdst_ref, glen_ref, gsrc_ref, x_ref, info_ref, gf_ref, ys_hbm, o_ref,
                    ybuf, sem):
    t = pl.program_id(0)
    T = x_ref.shape[0]
    copies = []
    for e in range(N_EXPERTS):
        j = t * N_EXPERTS + e
        copies += _run_copies(ys_hbm, ybuf, gdst_ref[j], gsrc_ref[j], glen_ref[j], RUN_BITS, sem)
    _start_then_wait(copies)
    last = t * N_EXPERTS + N_EXPERTS - 1
    used = gsrc_ref[last] + glen_ref[last]
    rows = lax.broadcasted_iota(jnp.int32, (SLOT_ROWS, 1), 0)
    y = jnp.where(rows < used, ybuf[...], 0.0).astype(BF16)
    info = info_ref[...]
    hit1, hit2 = _slots(info, T)
    weights = (jnp.where(hit1, info[:, 2:3], 0.0) + jnp.where(hit2, info[:, 3:4], 0.0)).astype(BF16)
    o_ref[...] = _rms(x_ref[...] + _dot(weights, y), gf_ref[...])


def _combine(x, info, gf, ys, gdst, glen, gsrc):
    N, D = x.shape
    T = ROUTE_T
    return pl.pallas_call(
        _combine_kernel,
        out_shape=jax.ShapeDtypeStruct((N, D), F32),
        grid_spec=pltpu.PrefetchScalarGridSpec(
            num_scalar_prefetch=3,
            grid=(N // T,),
            in_specs=[pl.BlockSpec((T, D), lambda t, *_: (t, 0)),
                      pl.BlockSpec((T, E_LANES), lambda t, *_: (t, 0)),
                      pl.BlockSpec(gf.shape, lambda t, *_: (0, 0)),
                      pl.BlockSpec(memory_space=pl.ANY)],
            out_specs=pl.BlockSpec((T, D), lambda t, *_: (t, 0)),
            scratch_shapes=[pltpu.VMEM((SLOT_ROWS, D), F32),
                            pltpu.SemaphoreType.DMA]),
        compiler_params=pltpu.CompilerParams(
            dimension_semantics=("arbitrary",),
            vmem_limit_bytes=V7X_VMEM_LIMIT_BYTES),
        name="moe_combine",
    )(gdst, glen, gsrc, x, info, gf, ys)


def _moe(x, g, router, wg, wu, wd, gf):
    N, D = x.shape
    n_t = N // ROUTE_T
    tm = EXPERT_TM
    router_p = jnp.pad(router, ((0, 0), (0, E_LANES - N_EXPERTS)))
    h, info, cnt = _route(x, g, router_p)
    glen = cnt[:, 0, :N_EXPERTS].astype(jnp.int32)
    total = jnp.sum(glen, axis=0)
    gpad = (total + tm - 1) // tm * tm
    gend = jnp.cumsum(gpad)
    goff = gend - gpad
    gdst = goff[None, :] + jnp.cumsum(glen, axis=0) - glen
    gsrc = jnp.cumsum(glen, axis=1) - glen
    rows_max = -(-(2 * N + n_t * N_EXPERTS * (RUN_ALIGN - 1) + N_EXPERTS * (tm - RUN_ALIGN)) // tm) * tm
    n_tiles = rows_max // tm
    n_valid = (gend[-1] // tm).reshape(1)
    tile_row = jnp.minimum(jnp.arange(n_tiles, dtype=jnp.int32), n_valid[0] - 1) * tm
    tile_expert = jnp.sum((tile_row[:, None] >= gend[None, :]).astype(jnp.int32), axis=1)
    flat = lambda a: a.reshape(-1).astype(jnp.int32)
    tail0 = jnp.concatenate([goff + total, gend[-1:]])
    tailn = jnp.concatenate([gpad - total, (rows_max - gend[-1:]) // (tm // 2)])
    xs = _dispatch(h, info, flat(gdst), flat(glen), flat(gsrc), flat(tail0), flat(tailn), rows_max)
    ys = _experts(xs, flat(tile_expert), flat(n_valid), wg, wu, wd)
    return _combine(x, info, gf, ys, flat(gdst), flat(glen), flat(gsrc))


def kernel(x, norm1_g, w_in, pool_w, pool_scale, gm_norm_g, gm_ws, gm_b,
           conv_dw_w, conv_dw_b, conv_ln_g, conv_ln_b, conv_pw_w, conv_pw_b,
           w_out, norm2_g, ffn_wg, ffn_wu, ffn_wd,
           moe_router, moe_wg, moe_wu, moe_wd, final_g):
    B, S, D = x.shape
    depth = w_in.shape[0]
    assert depth == 2, "layer 0 is the dense SwiGLU layer, layer 1 the expert layer + final norm"
    row = lambda t: t.reshape(1, -1)
    for l in range(depth):
        pool_bd = jax.scipy.linalg.block_diag(*[pool_w[l, gi] for gi in range(len(POOL_WINDOWS))])
        gm_wcat = jnp.transpose(gm_ws[l], (1, 0, 2)).reshape(GMLP_BLOCK, GMLP_HEADS * GMLP_BLOCK)
        gm_bias = jnp.repeat(gm_b[l].T, GMLP_HEAD_DIM, axis=1)
        x = _mixer(x, row(norm1_g[l]), w_in[l].astype(BF16), pool_bd.astype(BF16),
                   row(pool_scale[l]), row(gm_norm_g[l]), gm_wcat, gm_bias,
                   conv_dw_w[l], row(conv_dw_b[l]), row(conv_ln_g[l]), row(conv_ln_b[l]),
                   conv_pw_w[l].astype(BF16), row(conv_pw_b[l]), w_out[l].astype(BF16))
        xf = x.reshape(B * S, D)
        j = l // 2
        if l % 2 == 0:
            xf = _ffn(xf, row(norm2_g[l]), ffn_wg[j].astype(BF16), ffn_wu[j].astype(BF16),
                      ffn_wd[j].astype(BF16))
        else:
            xf = _moe(xf, row(norm2_g[l]), moe_router[j], moe_wg[j], moe_wu[j], moe_wd[j],
                      row(final_g))
        x = xf.reshape(B, S, D)
    return x
```

```python
import functools

import jax
import jax.numpy as jnp
from jax import lax
from jax.experimental import pallas as pl
from jax.experimental.pallas import tpu as pltpu

D_MODEL = 1024
CHUNK = 64
D_POOL = 256
POOL_WINDOWS = (2, 4, 8, 16)
POOL_GROUP = 64
D_GMLP = 384
GMLP_HEADS = 4
GMLP_HEAD_DIM = 96
GMLP_BLOCK = 128
D_CONV = 384
CONV_WIDTH = 31
D_IN = D_POOL + 2 * D_GMLP + 2 * D_CONV
D_FF = 2816
N_EXPERTS = 8
D_FF_EXPERT = 3584
EPS = 1e-6

V7X_VMEM_LIMIT_BYTES = 56 * 1024 * 1024

POOL_HALO = 16
CONV_HALO = 32
MIX_TM = 512
CONV_ROWS = 64
FFN_TM = 512
ROUTE_T = 512
EXPERT_TM = 1024
EXPERT_FK = 512
EXPERT_SUB = 256

BF16 = jnp.bfloat16
F32 = jnp.float32


def _rms(x, g):
    return x * lax.rsqrt(jnp.mean(x * x, axis=-1, keepdims=True) + EPS) * g


def _silu(x):
    return x * jax.nn.sigmoid(x)


def _dot(a, b):
    return jnp.dot(a, b, preferred_element_type=F32)


def _mixer_kernel(x_ref, g1_ref, win_ref, poolw_ref, pools_ref, gmg_ref,
                  gmws_ref, gmb_ref, dww_ref, dwb_ref, lng_ref, lnb_ref,
                  pww_ref, pwb_ref, wout_ref, o_ref, abuf, hbuf, hshift, ycat):
    tm = x_ref.shape[1]
    s = pl.program_id(1)

    @pl.when(s == 0)
    def _():
        abuf[0:POOL_HALO, :] = jnp.zeros((POOL_HALO, D_POOL), F32)
        hbuf[0:CONV_HALO, :] = jnp.zeros((CONV_HALO, D_CONV), F32)

    x = x_ref[0]
    h = _rms(x, g1_ref[...])
    p = _dot(h.astype(BF16), win_ref[...])
    o1 = D_POOL
    o2 = o1 + D_GMLP
    o3 = o2 + D_GMLP
    o4 = o3 + D_CONV
    a = p[:, :o1]
    u = p[:, o1:o2]
    v = p[:, o2:o3]
    cv = p[:, o3:o4]
    cg = p[:, o4:]

    abuf[POOL_HALO:POOL_HALO + tm, :] = a

    def shifted(j, lo):
        return abuf[POOL_HALO - j:POOL_HALO - j + tm, lo:lo + 128]

    lane = lax.broadcasted_iota(jnp.int32, (1, 128), 1)
    row = lax.broadcasted_iota(jnp.int32, (tm, 1), 0)
    tpos = (s * tm + row + 1).astype(F32)
    ys = []
    for half, (w_small, w_big) in enumerate(((2, 4), (8, 16))):
        lo = half * 128
        acc = shifted(0, lo)
        for j in range(1, w_small):
            acc = acc + shifted(j, lo)
        s_small = acc
        for j in range(w_small, w_big):
            acc = acc + shifted(j, lo)
        first = lane < POOL_GROUP
        ssel = jnp.where(first, s_small, acc)
        win = jnp.where(first, float(w_small), float(w_big))
        mean = ssel / jnp.minimum(tpos, win)
        ys.append(mean - a[:, lo:lo + 128])
    y = jnp.concatenate(ys, axis=1).astype(BF16)
    ya = _dot(y, poolw_ref[...]) * pools_ref[...]
    ycat[:, 0:o1] = ya.astype(BF16)
    abuf[0:POOL_HALO, :] = abuf[tm:tm + POOL_HALO, :]

    vn = _rms(v, gmg_ref[...])
    ri = lax.broadcasted_iota(jnp.int32, (GMLP_BLOCK, GMLP_HEADS * GMLP_BLOCK), 0)
    cj = lax.broadcasted_iota(jnp.int32, (GMLP_BLOCK, GMLP_HEADS * GMLP_BLOCK), 1)
    causal = (ri // CHUNK) >= ((cj % GMLP_BLOCK) // CHUNK)
    wcat = jnp.where(causal, gmws_ref[...], 0.0).astype(BF16)
    hid = lax.broadcasted_iota(jnp.int32, (1, D_GMLP), 1) // GMLP_HEAD_DIM
    gmb = gmb_ref[...]
    for blk in range(tm // GMLP_BLOCK):
        r0 = blk * GMLP_BLOCK
        vb = vn[r0:r0 + GMLP_BLOCK, :]
        vstack = jnp.concatenate(
            [jnp.where(hid == hh, vb, 0.0) for hh in range(GMLP_HEADS)],
            axis=0).astype(BF16)
        z = _dot(wcat, vstack) + gmb
        ycat[r0:r0 + GMLP_BLOCK, o1:o2] = (u[r0:r0 + GMLP_BLOCK, :] * z).astype(BF16)

    hbuf[CONV_HALO:CONV_HALO + tm, :] = cv * jax.nn.sigmoid(cg)
    base = CONV_HALO - (CONV_WIDTH - 1)
    dwb = dwb_ref[...]
    lng = lng_ref[...]
    lnb = lnb_ref[...]
    pwb = pwb_ref[...]
    for b in range(8):
        n_rows = tm + 8 * (len(range(b, CONV_WIDTH, 8)) - 1)
        hshift[b, 0:n_rows, :] = hbuf[base + b:base + b + n_rows, :]
    for c0 in range(0, tm, CONV_ROWS):
        acc = None
        for j in range(CONV_WIDTH):
            a, b = divmod(j, 8)
            term = dww_ref[j:j + 1, :] * hshift[b, c0 + 8 * a:c0 + 8 * a + CONV_ROWS, :]
            acc = term if acc is None else acc + term
        acc = acc + dwb
        mu = jnp.mean(acc, axis=-1, keepdims=True)
        cen = acc - mu
        var = jnp.mean(cen * cen, axis=-1, keepdims=True)
        ln = cen * lax.rsqrt(var + EPS) * lng + lnb
        yc = _dot(_silu(ln).astype(BF16), pww_ref[...]) + pwb
        ycat[c0:c0 + CONV_ROWS, o2:D_MODEL] = yc.astype(BF16)
    hbuf[0:CONV_HALO, :] = hbuf[tm:tm + CONV_HALO, :]

    o_ref[0] = x + _dot(ycat[...], wout_ref[...])


def _const_spec(shape):
    nd = len(shape)
    return pl.BlockSpec(shape, lambda b, s: (0,) * nd)


def _mixer(x, g1, w_in, pool_bd, pool_scale, gm_g, gm_wcat, gm_bias, dw_w, dw_b,
           ln_g, ln_b, pw_w, pw_b, w_out):
    B, S, D = x.shape
    tm = MIX_TM
    weights = (g1, w_in, pool_bd, pool_scale, gm_g, gm_wcat, gm_bias, dw_w, dw_b,
               ln_g, ln_b, pw_w, pw_b, w_out)
    return pl.pallas_call(
        _mixer_kernel,
        out_shape=jax.ShapeDtypeStruct((B, S, D), F32),
        grid=(B, S // tm),
        in_specs=[pl.BlockSpec((1, tm, D), lambda b, s: (b, s, 0))]
        + [_const_spec(w.shape) for w in weights],
        out_specs=pl.BlockSpec((1, tm, D), lambda b, s: (b, s, 0)),
        scratch_shapes=[
            pltpu.VMEM((POOL_HALO + tm, D_POOL), F32),
            pltpu.VMEM((CONV_HALO + tm, D_CONV), F32),
            pltpu.VMEM((8, tm + CONV_HALO - 8, D_CONV), F32),
            pltpu.VMEM((tm, D_MODEL), BF16),
        ],
        compiler_params=pltpu.CompilerParams(
            dimension_semantics=("arbitrary", "arbitrary"),
            vmem_limit_bytes=V7X_VMEM_LIMIT_BYTES),
        name="token_mixer",
    )(x, *weights)


def _ffn_kernel(x_ref, g_ref, wg_ref, wu_ref, wd_ref, o_ref):
    x = x_ref[...]
    h = _rms(x, g_ref[...]).astype(BF16)
    gate = _dot(h, wg_ref[...])
    up = _dot(h, wu_ref[...])
    act = (_silu(gate) * up).astype(BF16)
    o_ref[...] = x + _dot(act, wd_ref[...])


def _ffn(x, g, wg, wu, wd):
    N, D = x.shape
    tm = FFN_TM
    const = lambda shape: pl.BlockSpec(shape, lambda i: (0, 0))
    return pl.pallas_call(
        _ffn_kernel,
        out_shape=jax.ShapeDtypeStruct((N, D), F32),
        grid=(N // tm,),
        in_specs=[pl.BlockSpec((tm, D), lambda i: (i, 0)), const(g.shape),
                  const(wg.shape), const(wu.shape), const(wd.shape)],
        out_specs=pl.BlockSpec((tm, D), lambda i: (i, 0)),
        compiler_params=pltpu.CompilerParams(
            dimension_semantics=("arbitrary",),
            vmem_limit_bytes=V7X_VMEM_LIMIT_BYTES),
        name="dense_swiglu",
    )(x, g, wg, wu, wd)


E_LANES = 128
RUN_ALIGN = 8
SLOT_ROWS = 2 * ROUTE_T + RUN_ALIGN * N_EXPERTS
RUN_BITS = tuple(range((ROUTE_T // RUN_ALIGN).bit_length() - 1, -1, -1))
TAIL_BITS = tuple(range((EXPERT_TM // RUN_ALIGN).bit_length() - 2, -1, -1))


def _route_kernel(x_ref, g_ref, router_ref, h_ref, info_ref, cnt_ref):
    T = x_ref.shape[0]
    h = _rms(x_ref[...], g_ref[...])
    h_ref[...] = h.astype(BF16)
    logits = jnp.dot(h, router_ref[...], preferred_element_type=F32,
                     precision=lax.Precision.HIGHEST)
    lane = lax.broadcasted_iota(jnp.int32, (T, E_LANES), 1)
    logits = jnp.where(lane < N_EXPERTS, logits, -jnp.inf)
    m1 = jnp.max(logits, axis=-1, keepdims=True)
    i1 = jnp.min(jnp.where(logits == m1, lane, E_LANES), axis=-1, keepdims=True)
    rest = jnp.where(lane == i1, -jnp.inf, logits)
    m2 = jnp.max(rest, axis=-1, keepdims=True)
    i2 = jnp.min(jnp.where(rest == m2, lane, E_LANES), axis=-1, keepdims=True)
    e2 = jnp.exp(m2 - m1)
    w1 = 1.0 / (1.0 + e2)
    w2 = e2 / (1.0 + e2)
    oh1 = (lane == i1).astype(F32)
    oh2 = (lane == i2).astype(F32)
    oh = oh1 + oh2
    ri = lax.broadcasted_iota(jnp.int32, (T, T), 0)
    ci = lax.broadcasted_iota(jnp.int32, (T, T), 1)
    rank = _dot((ri > ci).astype(BF16), oh.astype(BF16))
    cnt = jnp.sum(oh, axis=0, keepdims=True)
    padc = jnp.floor((cnt + (RUN_ALIGN - 1.0)) * (1.0 / RUN_ALIGN)) * RUN_ALIGN
    lane1 = lax.broadcasted_iota(jnp.int32, (1, E_LANES), 1)
    start = jnp.zeros((1, E_LANES), F32)
    for e in range(N_EXPERTS - 1):
        start = start + jnp.where(lane1 > e, padc[:, e:e + 1], 0.0)
    slot = start + rank
    q1 = jnp.sum(oh1 * slot, axis=-1, keepdims=True)
    q2 = jnp.sum(oh2 * slot, axis=-1, keepdims=True)
    info_ref[...] = jnp.where(lane == 0, q1, jnp.where(lane == 1, q2, jnp.where(
        lane == 2, w1, jnp.where(lane == 3, w2, 0.0))))
    cnt_ref[0] = padc


def _route(x, g, router_p):
    N, D = x.shape
    T = ROUTE_T
    n_t = N // T
    return pl.pallas_call(
        _route_kernel,
        out_shape=(jax.ShapeDtypeStruct((N, D), BF16),
                   jax.ShapeDtypeStruct((N, E_LANES), F32),
                   jax.ShapeDtypeStruct((n_t, 1, E_LANES), F32)),
        grid=(n_t,),
        in_specs=[pl.BlockSpec((T, D), lambda t: (t, 0)),
                  pl.BlockSpec(g.shape, lambda t: (0, 0)),
                  pl.BlockSpec(router_p.shape, lambda t: (0, 0))],
        out_specs=(pl.BlockSpec((T, D), lambda t: (t, 0)),
                   pl.BlockSpec((T, E_LANES), lambda t: (t, 0)),
                   pl.BlockSpec((1, 1, E_LANES), lambda t: (t, 0, 0))),
        compiler_params=pltpu.CompilerParams(
            dimension_semantics=("arbitrary",),
            vmem_limit_bytes=V7X_VMEM_LIMIT_BYTES),
        name="moe_route",
    )(x, g, router_p)


def _run_copies(src, dst, src0, dst0, rows, bits, sem):
    m = rows // RUN_ALIGN
    off = 0
    out = []
    for b in bits:
        size = RUN_ALIGN << b
        take = (m >> b) & 1
        s0 = 0 if src0 is None else pl.multiple_of(src0 + off, RUN_ALIGN)
        d0 = pl.multiple_of(dst0 + off, RUN_ALIGN)
        out.append((take == 1, pltpu.make_async_copy(
            src.at[pl.ds(s0, size)], dst.at[pl.ds(d0, size)], sem)))
        off = off + take * size
    return out


def _start(copies, enable=True):
    for pred, cp in copies:
        @pl.when(pred & enable)
        def _(cp=cp):
            cp.start()


def _wait(copies, enable=True):
    for pred, cp in copies:
        @pl.when(pred & enable)
        def _(cp=cp):
            cp.wait()


def _tile_runs(tile, table_refs, hbm, buf, sem, *, to_hbm):
    gdst_ref, glen_ref, gsrc_ref = table_refs
    copies = []
    for e in range(N_EXPERTS):
        j = tile * N_EXPERTS + e
        if to_hbm:
            copies += _run_copies(buf, hbm, gsrc_ref[j], gdst_ref[j], glen_ref[j], RUN_BITS, sem)
        else:
            copies += _run_copies(hbm, buf, gdst_ref[j], gsrc_ref[j], glen_ref[j], RUN_BITS, sem)
    return copies


def _slots(info, T):
    q1 = info[:, 0:1].astype(jnp.int32)
    q2 = info[:, 1:2].astype(jnp.int32)
    r = lax.broadcasted_iota(jnp.int32, (T, SLOT_ROWS), 1)
    return q1 == r, q2 == r


def _dispatch_kernel(gdst_ref, glen_ref, gsrc_ref, tail0_ref, tailn_ref,
                     h_ref, info_ref, xs_hbm, xbuf, zbuf, sem, *, slack_pieces):
    t = pl.program_id(0)
    n_t = pl.num_programs(0)
    T = h_ref.shape[0]
    slot = t % 2
    tables = (gdst_ref, glen_ref, gsrc_ref)
    hit1, hit2 = _slots(info_ref[...], T)
    onehot = (hit1 | hit2).astype(BF16)
    xbuf[slot] = lax.dot_general(onehot, h_ref[...], (((0,), (0,)), ((), ())),
                                 preferred_element_type=F32)
    mine = _tile_runs(t, tables, xs_hbm, xbuf.at[slot], sem.at[slot], to_hbm=True)
    _start(mine)
    prev = _tile_runs(jnp.maximum(t - 1, 0), tables, xs_hbm, xbuf.at[1 - slot],
                      sem.at[1 - slot], to_hbm=True)
    _wait(prev, t > 0)
    _wait(mine, t == n_t - 1)

    @pl.when(t == 0)
    def _():
        zbuf[...] = jnp.zeros_like(zbuf)
        tails = []
        for e in range(N_EXPERTS):
            tails += _run_copies(zbuf, xs_hbm, None, tail0_ref[e], tailn_ref[e], TAIL_BITS,
                                 sem.at[2])
        piece = zbuf.shape[0]
        for c in range(slack_pieces):
            d0 = pl.multiple_of(tail0_ref[N_EXPERTS] + c * piece, RUN_ALIGN)
            tails.append((c < tailn_ref[N_EXPERTS], pltpu.make_async_copy(
                zbuf, xs_hbm.at[pl.ds(d0, piece)], sem.at[2])))
        _start(tails)
        _wait(tails)


def _dispatch(h, info, gdst, glen, gsrc, tail0, tailn, rows_max):
    N, D = h.shape
    T = ROUTE_T
    slack_pieces = (rows_max - 2 * N) // (EXPERT_TM // 2)
    return pl.pallas_call(
        functools.partial(_dispatch_kernel, slack_pieces=slack_pieces),
        out_shape=jax.ShapeDtypeStruct((rows_max, D), F32),
        grid_spec=pltpu.PrefetchScalarGridSpec(
            num_scalar_prefetch=5,
            grid=(N // T,),
            in_specs=[pl.BlockSpec((T, D), lambda t, *_: (t, 0)),
                      pl.BlockSpec((T, E_LANES), lambda t, *_: (t, 0))],
            out_specs=pl.BlockSpec(memory_space=pl.ANY),
            scratch_shapes=[pltpu.VMEM((2, SLOT_ROWS, D), F32),
                            pltpu.VMEM((EXPERT_TM // 2, D), F32),
                            pltpu.SemaphoreType.DMA((3,))]),
        compiler_params=pltpu.CompilerParams(
            dimension_semantics=("arbitrary",),
            vmem_limit_bytes=V7X_VMEM_LIMIT_BYTES),
        name="moe_dispatch",
    )(gdst, glen, gsrc, tail0, tailn, h, info)


def _experts_kernel(te_ref, nv_ref, ns_ref, x_ref, wg_ref, wu_ref, wd_ref, o_ref,
                    xb, wgb, wub, wdb, acc):
    i = pl.program_id(0)
    k = pl.program_id(1)
    n_sub = ns_ref[i]

    @pl.when(n_sub > 0)
    def _():
        @pl.when(k == 0)
        def _():
            xb[...] = x_ref[...].astype(BF16)
            acc[...] = jnp.zeros_like(acc)

        wgb[...] = wg_ref[0].astype(BF16)
        wub[...] = wu_ref[0].astype(BF16)
        wdb[...] = wd_ref[0].astype(BF16)

        def piece(s, carry):
            rows = pl.ds(pl.multiple_of(s * EXPERT_SUB, EXPERT_SUB), EXPERT_SUB)
            h = xb[rows, :]
            act = (_silu(_dot(h, wgb[...])) * _dot(h, wub[...])).astype(BF16)
            acc[rows, :] += _dot(act, wdb[...])
            return carry

        lax.fori_loop(0, n_sub, piece, 0)

        @pl.when(k == pl.num_programs(1) - 1)
        def _():
            o_ref[...] = acc[...]

    @pl.when((n_sub == 0) & (k == 0))
    def _():
        o_ref[...] = jnp.zeros_like(o_ref)


def _experts(xs, tile_expert, n_valid, n_sub, wg, wu, wd):
    M, D = xs.shape
    tm, fk = EXPERT_TM, EXPERT_FK
    E, _, F = wg.shape
    nk = F // fk

    def row_map(i, k, te, nv, ns):
        return (jnp.minimum(i, nv[0] - 1), 0)

    def kk(i, k, nv):
        return jnp.where(i < nv[0], k, nk - 1)

    return pl.pallas_call(
        _experts_kernel,
        out_shape=jax.ShapeDtypeStruct((M, D), F32),
        grid_spec=pltpu.PrefetchScalarGridSpec(
            num_scalar_prefetch=3,
            grid=(M // tm, nk),
            in_specs=[
                pl.BlockSpec((tm, D), row_map),
                pl.BlockSpec((1, D, fk), lambda i, k, te, nv, ns: (te[i], 0, kk(i, k, nv))),
                pl.BlockSpec((1, D, fk), lambda i, k, te, nv, ns: (te[i], 0, kk(i, k, nv))),
                pl.BlockSpec((1, fk, D), lambda i, k, te, nv, ns: (te[i], kk(i, k, nv), 0)),
            ],
            out_specs=pl.BlockSpec((tm, D), lambda i, k, te, nv, ns: (i, 0)),
            scratch_shapes=[pltpu.VMEM((tm, D), BF16),
                            pltpu.VMEM((D, fk), BF16), pltpu.VMEM((D, fk), BF16),
                            pltpu.VMEM((fk, D), BF16),
                            pltpu.VMEM((tm, D), F32)]),
        compiler_params=pltpu.CompilerParams(
            dimension_semantics=("arbitrary", "arbitrary"),
            vmem_limit_bytes=V7X_VMEM_LIMIT_BYTES),
        name="moe_experts",
    )(tile_expert, n_valid, n_sub, xs, wg, wu, wd)


def _combine_kernel(gdst_ref, glen_ref, gsrc_ref, x_ref, info_ref, gf_ref, ys_hbm, o_ref,
                    ybuf, sem):
    t = pl.program_id(0)
    n_t = pl.num_programs(0)
    T = x_ref.shape[0]
    slot = t % 2
    tables = (gdst_ref, glen_ref, gsrc_ref)
    mine = _tile_runs(t, tables, ys_hbm, ybuf.at[slot], sem.at[slot], to_hbm=False)
    _start(mine, t == 0)
    ahead = _tile_runs(jnp.minimum(t + 1, n_t - 1), tables, ys_hbm, ybuf.at[1 - slot],
                       sem.at[1 - slot], to_hbm=False)
    _start(ahead, t + 1 < n_t)
    _wait(mine)
    last = t * N_EXPERTS + N_EXPERTS - 1
    used = gsrc_ref[last] + glen_ref[last]
    rows = lax.broadcasted_iota(jnp.int32, (SLOT_ROWS, 1), 0)
    y = jnp.where(rows < used, ybuf[slot], 0.0).astype(BF16)
    info = info_ref[...]
    hit1, hit2 = _slots(info, T)
    weights = (jnp.where(hit1, info[:, 2:3], 0.0) + jnp.where(hit2, info[:, 3:4], 0.0)).astype(BF16)
    o_ref[...] = _rms(x_ref[...] + _dot(weights, y), gf_ref[...])


def _combine(x, info, gf, ys, gdst, glen, gsrc):
    N, D = x.shape
    T = ROUTE_T
    return pl.pallas_call(
        _combine_kernel,
        out_shape=jax.ShapeDtypeStruct((N, D), F32),
        grid_spec=pltpu.PrefetchScalarGridSpec(
            num_scalar_prefetch=3,
            grid=(N // T,),
            in_specs=[pl.BlockSpec((T, D), lambda t, *_: (t, 0)),
                      pl.BlockSpec((T, E_LANES), lambda t, *_: (t, 0)),
                      pl.BlockSpec(gf.shape, lambda t, *_: (0, 0)),
                      pl.BlockSpec(memory_space=pl.ANY)],
            out_specs=pl.BlockSpec((T, D), lambda t, *_: (t, 0)),
            scratch_shapes=[pltpu.VMEM((2, SLOT_ROWS, D), F32),
                            pltpu.SemaphoreType.DMA((2,))]),
        compiler_params=pltpu.CompilerParams(
            dimension_semantics=("arbitrary",),
            vmem_limit_bytes=V7X_VMEM_LIMIT_BYTES),
        name="moe_combine",
    )(gdst, glen, gsrc, x, info, gf, ys)


def _moe(x, g, router, wg, wu, wd, gf):
    N, D = x.shape
    n_t = N // ROUTE_T
    tm = EXPERT_TM
    router_p = jnp.pad(router, ((0, 0), (0, E_LANES - N_EXPERTS)))
    h, info, cnt = _route(x, g, router_p)
    glen = cnt[:, 0, :N_EXPERTS].astype(jnp.int32)
    total = jnp.sum(glen, axis=0)
    gpad = (total + tm - 1) // tm * tm
    gend = jnp.cumsum(gpad)
    goff = gend - gpad
    gdst = goff[None, :] + jnp.cumsum(glen, axis=0) - glen
    gsrc = jnp.cumsum(glen, axis=1) - glen
    rows_max = -(-(2 * N + n_t * N_EXPERTS * (RUN_ALIGN - 1) + N_EXPERTS * (tm - RUN_ALIGN)) // tm) * tm
    n_tiles = rows_max // tm
    n_valid = (gend[-1] // tm).reshape(1)
    tile_row = jnp.minimum(jnp.arange(n_tiles, dtype=jnp.int32), n_valid[0] - 1) * tm
    tile_expert = jnp.sum((tile_row[:, None] >= gend[None, :]).astype(jnp.int32), axis=1)
    tile_rows = jnp.clip((goff + total)[tile_expert] - tile_row, 0, tm)
    tile_rows = jnp.where(jnp.arange(n_tiles) < n_valid[0], tile_rows, 0)
    n_sub = (tile_rows + EXPERT_SUB - 1) // EXPERT_SUB
    flat = lambda a: a.reshape(-1).astype(jnp.int32)
    tail0 = jnp.concatenate([goff + total, gend[-1:]])
    tailn = jnp.concatenate([gpad - total, (rows_max - gend[-1:]) // (tm // 2)])
    xs = _dispatch(h, info, flat(gdst), flat(glen), flat(gsrc), flat(tail0), flat(tailn), rows_max)
    ys = _experts(xs, flat(tile_expert), flat(n_valid), flat(n_sub), wg, wu, wd)
    return _combine(x, info, gf, ys, flat(gdst), flat(glen), flat(gsrc))


def kernel(x, norm1_g, w_in, pool_w, pool_scale, gm_norm_g, gm_ws, gm_b,
           conv_dw_w, conv_dw_b, conv_ln_g, conv_ln_b, conv_pw_w, conv_pw_b,
           w_out, norm2_g, ffn_wg, ffn_wu, ffn_wd,
           moe_router, moe_wg, moe_wu, moe_wd, final_g):
    B, S, D = x.shape
    depth = w_in.shape[0]
    assert depth == 2, "layer 0 is the dense SwiGLU layer, layer 1 the expert layer + final norm"
    row = lambda t: t.reshape(1, -1)
    for l in range(depth):
        pool_bd = jax.scipy.linalg.block_diag(*[pool_w[l, gi] for gi in range(len(POOL_WINDOWS))])
        gm_wcat = jnp.transpose(gm_ws[l], (1, 0, 2)).reshape(GMLP_BLOCK, GMLP_HEADS * GMLP_BLOCK)
        gm_bias = jnp.repeat(gm_b[l].T, GMLP_HEAD_DIM, axis=1)
        x = _mixer(x, row(norm1_g[l]), w_in[l].astype(BF16), pool_bd.astype(BF16),
                   row(pool_scale[l]), row(gm_norm_g[l]), gm_wcat, gm_bias,
                   conv_dw_w[l], row(conv_dw_b[l]), row(conv_ln_g[l]), row(conv_ln_b[l]),
                   conv_pw_w[l].astype(BF16), row(conv_pw_b[l]), w_out[l].astype(BF16))
        xf = x.reshape(B * S, D)
        j = l // 2
        if l % 2 == 0:
            xf = _ffn(xf, row(norm2_g[l]), ffn_wg[j].astype(BF16), ffn_wu[j].astype(BF16),
                      ffn_wd[j].astype(BF16))
        else:
            xf = _moe(xf, row(norm2_g[l]), moe_router[j], moe_wg[j], moe_wu[j], moe_wd[j],
                      row(final_g))
        x = xf.reshape(B, S, D)
    return x
```

```python
import functools

import jax
import jax.numpy as jnp
from jax import lax
from jax.experimental import pallas as pl
from jax.experimental.pallas import tpu as pltpu

D_MODEL = 1024
CHUNK = 64
D_POOL = 256
POOL_WINDOWS = (2, 4, 8, 16)
POOL_GROUP = 64
D_GMLP = 384
GMLP_HEADS = 4
GMLP_HEAD_DIM = 96
GMLP_BLOCK = 128
D_CONV = 384
CONV_WIDTH = 31
D_IN = D_POOL + 2 * D_GMLP + 2 * D_CONV
D_FF = 2816
N_EXPERTS = 8
D_FF_EXPERT = 3584
EPS = 1e-6

V7X_VMEM_LIMIT_BYTES = 56 * 1024 * 1024

POOL_HALO = 16
CONV_HALO = 32
MIX_TM = 512
CONV_ROWS = 64
FFN_TM = 512
ROUTE_T = 512
EXPERT_TM = 1024
EXPERT_FK = 512
EXPERT_SUB = 256

BF16 = jnp.bfloat16
F32 = jnp.float32


def _rms(x, g):
    return x * lax.rsqrt(jnp.mean(x * x, axis=-1, keepdims=True) + EPS) * g


def _silu(x):
    return x * jax.nn.sigmoid(x)


def _dot(a, b):
    return jnp.dot(a, b, preferred_element_type=F32)


def _mixer_kernel(x_ref, g1_ref, win_ref, poolw_ref, pools_ref, gmg_ref,
                  gmws_ref, gmb_ref, dww_ref, dwb_ref, lng_ref, lnb_ref,
                  pww_ref, pwb_ref, wout_ref, o_ref, abuf, hbuf, hshift, wtap, ycat):
    tm = x_ref.shape[1]
    s = pl.program_id(1)

    @pl.when(s == 0)
    def _():
        abuf[0:POOL_HALO, :] = jnp.zeros((POOL_HALO, D_POOL), F32)
        hbuf[0:CONV_HALO, :] = jnp.zeros((CONV_HALO, D_CONV), F32)
        for j in range(CONV_WIDTH):
            wtap[j] = jnp.broadcast_to(dww_ref[j:j + 1, :], (8, D_CONV))

    x = x_ref[0]
    h = _rms(x, g1_ref[...])
    p = _dot(h.astype(BF16), win_ref[...])
    o1 = D_POOL
    o2 = o1 + D_GMLP
    o3 = o2 + D_GMLP
    o4 = o3 + D_CONV
    u = p[:, o1:o2]
    v = p[:, o2:o3]
    cv = p[:, o3:o4]
    cg = p[:, o4:]

    abuf[POOL_HALO:POOL_HALO + tm, :] = p[:, :o1]

    def shifted(j, lo):
        return abuf[POOL_HALO - j:POOL_HALO - j + tm, lo:lo + 128]

    lane = lax.broadcasted_iota(jnp.int32, (1, 128), 1)
    row = lax.broadcasted_iota(jnp.int32, (tm, 1), 0)
    tpos = (s * tm + row + 1).astype(F32)
    ys = []
    for half, (w_small, w_big) in enumerate(((2, 4), (8, 16))):
        lo = half * 128
        acc = shifted(0, lo)
        for j in range(1, w_small):
            acc = acc + shifted(j, lo)
        s_small = acc
        for j in range(w_small, w_big):
            acc = acc + shifted(j, lo)
        first = lane < POOL_GROUP
        ssel = jnp.where(first, s_small, acc)
        win = jnp.where(first, float(w_small), float(w_big))
        mean = ssel / jnp.minimum(tpos, win)
        ys.append(mean - shifted(0, lo))
    y = jnp.concatenate(ys, axis=1).astype(BF16)
    ya = _dot(y, poolw_ref[...]) * pools_ref[...]
    ycat[:, 0:o1] = ya.astype(BF16)
    abuf[0:POOL_HALO, :] = abuf[tm:tm + POOL_HALO, :]

    vn = _rms(v, gmg_ref[...])
    ri = lax.broadcasted_iota(jnp.int32, (GMLP_BLOCK, GMLP_HEADS * GMLP_BLOCK), 0)
    cj = lax.broadcasted_iota(jnp.int32, (GMLP_BLOCK, GMLP_HEADS * GMLP_BLOCK), 1)
    causal = (ri // CHUNK) >= ((cj % GMLP_BLOCK) // CHUNK)
    wcat = jnp.where(causal, gmws_ref[...], 0.0).astype(BF16)
    hid = lax.broadcasted_iota(jnp.int32, (1, D_GMLP), 1) // GMLP_HEAD_DIM
    gmb = gmb_ref[...]
    for blk in range(tm // GMLP_BLOCK):
        r0 = blk * GMLP_BLOCK
        vb = vn[r0:r0 + GMLP_BLOCK, :]
        vstack = jnp.concatenate(
            [jnp.where(hid == hh, vb, 0.0) for hh in range(GMLP_HEADS)],
            axis=0).astype(BF16)
        z = _dot(wcat, vstack) + gmb
        ycat[r0:r0 + GMLP_BLOCK, o1:o2] = (u[r0:r0 + GMLP_BLOCK, :] * z).astype(BF16)

    hbuf[CONV_HALO:CONV_HALO + tm, :] = cv * jax.nn.sigmoid(cg)
    base = CONV_HALO - (CONV_WIDTH - 1)
    dwb = dwb_ref[...]
    lng = lng_ref[...]
    lnb = lnb_ref[...]
    pwb = pwb_ref[...]
    for b in range(8):
        n_rows = tm + 8 * (len(range(b, CONV_WIDTH, 8)) - 1)
        hshift[b, 0:n_rows, :] = hbuf[base + b:base + b + n_rows, :]
    for c0 in range(0, tm, CONV_ROWS):
        acc = None
        for j in range(CONV_WIDTH):
            r0 = c0 + 8 * (j // 8)
            win = hshift[j % 8, r0:r0 + CONV_ROWS, :].reshape(CONV_ROWS // 8, 8, D_CONV)
            term = wtap[j] * win
            acc = term if acc is None else acc + term
        acc = acc.reshape(CONV_ROWS, D_CONV) + dwb
        mu = jnp.mean(acc, axis=-1, keepdims=True)
        cen = acc - mu
        var = jnp.mean(cen * cen, axis=-1, keepdims=True)
        ln = cen * lax.rsqrt(var + EPS) * lng + lnb
        yc = _dot(_silu(ln).astype(BF16), pww_ref[...]) + pwb
        ycat[c0:c0 + CONV_ROWS, o2:D_MODEL] = yc.astype(BF16)
    hbuf[0:CONV_HALO, :] = hbuf[tm:tm + CONV_HALO, :]

    o_ref[0] = x + _dot(ycat[...], wout_ref[...])


def _const_spec(shape):
    nd = len(shape)
    return pl.BlockSpec(shape, lambda b, s: (0,) * nd)


def _mixer(x, g1, w_in, pool_bd, pool_scale, gm_g, gm_wcat, gm_bias, dw_w, dw_b,
           ln_g, ln_b, pw_w, pw_b, w_out):
    B, S, D = x.shape
    tm = MIX_TM
    weights = (g1, w_in, pool_bd, pool_scale, gm_g, gm_wcat, gm_bias, dw_w, dw_b,
               ln_g, ln_b, pw_w, pw_b, w_out)
    return pl.pallas_call(
        _mixer_kernel,
        out_shape=jax.ShapeDtypeStruct((B, S, D), F32),
        grid=(B, S // tm),
        in_specs=[pl.BlockSpec((1, tm, D), lambda b, s: (b, s, 0))]
        + [_const_spec(w.shape) for w in weights],
        out_specs=pl.BlockSpec((1, tm, D), lambda b, s: (b, s, 0)),
        scratch_shapes=[
            pltpu.VMEM((POOL_HALO + tm, D_POOL), F32),
            pltpu.VMEM((CONV_HALO + tm, D_CONV), F32),
            pltpu.VMEM((8, tm + CONV_HALO - 8, D_CONV), F32),
            pltpu.VMEM((CONV_WIDTH, 8, D_CONV), F32),
            pltpu.VMEM((tm, D_MODEL), BF16),
        ],
        compiler_params=pltpu.CompilerParams(
            dimension_semantics=("arbitrary", "arbitrary"),
            vmem_limit_bytes=V7X_VMEM_LIMIT_BYTES),
        name="token_mixer",
    )(x, *weights)


def _ffn_kernel(x_ref, g_ref, wg_ref, wu_ref, wd_ref, o_ref):
    x = x_ref[...]
    h = _rms(x, g_ref[...]).astype(BF16)
    gate = _dot(h, wg_ref[...])
    up = _dot(h, wu_ref[...])
    act = (_silu(gate) * up).astype(BF16)
    o_ref[...] = x + _dot(act, wd_ref[...])


def _ffn(x, g, wg, wu, wd):
    N, D = x.shape
    tm = FFN_TM
    const = lambda shape: pl.BlockSpec(shape, lambda i: (0, 0))
    return pl.pallas_call(
        _ffn_kernel,
        out_shape=jax.ShapeDtypeStruct((N, D), F32),
        grid=(N // tm,),
        in_specs=[pl.BlockSpec((tm, D), lambda i: (i, 0)), const(g.shape),
                  const(wg.shape), const(wu.shape), const(wd.shape)],
        out_specs=pl.BlockSpec((tm, D), lambda i: (i, 0)),
        compiler_params=pltpu.CompilerParams(
            dimension_semantics=("arbitrary",),
            vmem_limit_bytes=V7X_VMEM_LIMIT_BYTES),
        name="dense_swiglu",
    )(x, g, wg, wu, wd)


E_LANES = 128
RUN_ALIGN = 8
SLOT_ROWS = 2 * ROUTE_T + RUN_ALIGN * N_EXPERTS
RUN_BITS = tuple(range((ROUTE_T // RUN_ALIGN).bit_length() - 1, -1, -1))
TAIL_BITS = tuple(range((EXPERT_TM // RUN_ALIGN).bit_length() - 2, -1, -1))


def _route_kernel(x_ref, g_ref, router_ref, h_ref, info_ref, cnt_ref):
    T = x_ref.shape[0]
    h = _rms(x_ref[...], g_ref[...])
    h_hi = h.astype(BF16)
    h_ref[...] = h_hi
    h_lo = (h - h_hi.astype(F32)).astype(BF16)
    r = router_ref[...]
    r_hi = r.astype(BF16)
    r_lo = (r - r_hi.astype(F32)).astype(BF16)
    logits = _dot(h_hi, r_hi) + (_dot(h_lo, r_hi) + _dot(h_hi, r_lo))
    lane = lax.broadcasted_iota(jnp.int32, (T, E_LANES), 1)
    logits = jnp.where(lane < N_EXPERTS, logits, -jnp.inf)
    m1 = jnp.max(logits, axis=-1, keepdims=True)
    i1 = jnp.min(jnp.where(logits == m1, lane, E_LANES), axis=-1, keepdims=True)
    rest = jnp.where(lane == i1, -jnp.inf, logits)
    m2 = jnp.max(rest, axis=-1, keepdims=True)
    i2 = jnp.min(jnp.where(rest == m2, lane, E_LANES), axis=-1, keepdims=True)
    e2 = jnp.exp(m2 - m1)
    w1 = 1.0 / (1.0 + e2)
    w2 = e2 / (1.0 + e2)
    oh1 = (lane == i1).astype(F32)
    oh2 = (lane == i2).astype(F32)
    oh = oh1 + oh2
    ri = lax.broadcasted_iota(jnp.int32, (T, T), 0)
    ci = lax.broadcasted_iota(jnp.int32, (T, T), 1)
    rank = _dot((ri > ci).astype(BF16), oh.astype(BF16))
    cnt = jnp.sum(oh, axis=0, keepdims=True)
    padc = jnp.floor((cnt + (RUN_ALIGN - 1.0)) * (1.0 / RUN_ALIGN)) * RUN_ALIGN
    lane1 = lax.broadcasted_iota(jnp.int32, (1, E_LANES), 1)
    start = jnp.zeros((1, E_LANES), F32)
    for e in range(N_EXPERTS - 1):
        start = start + jnp.where(lane1 > e, padc[:, e:e + 1], 0.0)
    slot = start + rank
    q1 = jnp.sum(oh1 * slot, axis=-1, keepdims=True)
    q2 = jnp.sum(oh2 * slot, axis=-1, keepdims=True)
    info_ref[...] = jnp.where(lane == 0, q1, jnp.where(lane == 1, q2, jnp.where(
        lane == 2, w1, jnp.where(lane == 3, w2, 0.0))))
    cnt_ref[0] = padc


def _route(x, g, router_p):
    N, D = x.shape
    T = ROUTE_T
    n_t = N // T
    return pl.pallas_call(
        _route_kernel,
        out_shape=(jax.ShapeDtypeStruct((N, D), BF16),
                   jax.ShapeDtypeStruct((N, E_LANES), F32),
                   jax.ShapeDtypeStruct((n_t, 1, E_LANES), F32)),
        grid=(n_t,),
        in_specs=[pl.BlockSpec((T, D), lambda t: (t, 0)),
                  pl.BlockSpec(g.shape, lambda t: (0, 0)),
                  pl.BlockSpec(router_p.shape, lambda t: (0, 0))],
        out_specs=(pl.BlockSpec((T, D), lambda t: (t, 0)),
                   pl.BlockSpec((T, E_LANES), lambda t: (t, 0)),
                   pl.BlockSpec((1, 1, E_LANES), lambda t: (t, 0, 0))),
        compiler_params=pltpu.CompilerParams(
            dimension_semantics=("arbitrary",),
            vmem_limit_bytes=V7X_VMEM_LIMIT_BYTES),
        name="moe_route",
    )(x, g, router_p)


def _run_copies(src, dst, src0, dst0, rows, bits, sem):
    m = rows // RUN_ALIGN
    off = 0
    out = []
    for b in bits:
        size = RUN_ALIGN << b
        take = (m >> b) & 1
        s0 = 0 if src0 is None else pl.multiple_of(src0 + off, RUN_ALIGN)
        d0 = pl.multiple_of(dst0 + off, RUN_ALIGN)
        out.append((take == 1, pltpu.make_async_copy(
            src.at[pl.ds(s0, size)], dst.at[pl.ds(d0, size)], sem)))
        off = off + take * size
    return out


def _start(copies, enable=True):
    for pred, cp in copies:
        @pl.when(pred & enable)
        def _(cp=cp):
            cp.start()


def _wait(copies, enable=True):
    for pred, cp in copies:
        @pl.when(pred & enable)
        def _(cp=cp):
            cp.wait()


def _tile_runs(tile, table_refs, hbm, buf, sem, *, to_hbm):
    gdst_ref, glen_ref, gsrc_ref = table_refs
    copies = []
    for e in range(N_EXPERTS):
        j = tile * N_EXPERTS + e
        if to_hbm:
            copies += _run_copies(buf, hbm, gsrc_ref[j], gdst_ref[j], glen_ref[j], RUN_BITS, sem)
        else:
            copies += _run_copies(hbm, buf, gdst_ref[j], gsrc_ref[j], glen_ref[j], RUN_BITS, sem)
    return copies


def _slots(info, T):
    q1 = info[:, 0:1].astype(jnp.int32)
    q2 = info[:, 1:2].astype(jnp.int32)
    r = lax.broadcasted_iota(jnp.int32, (T, SLOT_ROWS), 1)
    return q1 == r, q2 == r


def _dispatch_kernel(gdst_ref, glen_ref, gsrc_ref, tail0_ref, tailn_ref,
                     h_ref, info_ref, xs_hbm, xbuf, zbuf, sem, *, slack_pieces):
    t = pl.program_id(0)
    n_t = pl.num_programs(0)
    T = h_ref.shape[0]
    slot = t % 2
    tables = (gdst_ref, glen_ref, gsrc_ref)
    hit1, hit2 = _slots(info_ref[...], T)
    onehot = (hit1 | hit2).astype(BF16)
    xbuf[slot] = lax.dot_general(onehot, h_ref[...], (((0,), (0,)), ((), ())),
                                 preferred_element_type=F32)
    mine = _tile_runs(t, tables, xs_hbm, xbuf.at[slot], sem.at[slot], to_hbm=True)
    _start(mine)
    prev = _tile_runs(jnp.maximum(t - 1, 0), tables, xs_hbm, xbuf.at[1 - slot],
                      sem.at[1 - slot], to_hbm=True)
    _wait(prev, t > 0)
    _wait(mine, t == n_t - 1)

    @pl.when(t == 0)
    def _():
        zbuf[...] = jnp.zeros_like(zbuf)
        tails = []
        for e in range(N_EXPERTS):
            tails += _run_copies(zbuf, xs_hbm, None, tail0_ref[e], tailn_ref[e], TAIL_BITS,
                                 sem.at[2])
        piece = zbuf.shape[0]
        for c in range(slack_pieces):
            d0 = pl.multiple_of(tail0_ref[N_EXPERTS] + c * piece, RUN_ALIGN)
            tails.append((c < tailn_ref[N_EXPERTS], pltpu.make_async_copy(
                zbuf, xs_hbm.at[pl.ds(d0, piece)], sem.at[2])))
        _start(tails)
        _wait(tails)


def _dispatch(h, info, gdst, glen, gsrc, tail0, tailn, rows_max):
    N, D = h.shape
    T = ROUTE_T
    slack_pieces = (rows_max - 2 * N) // (EXPERT_TM // 2)
    return pl.pallas_call(
        functools.partial(_dispatch_kernel, slack_pieces=slack_pieces),
        out_shape=jax.ShapeDtypeStruct((rows_max, D), F32),
        grid_spec=pltpu.PrefetchScalarGridSpec(
            num_scalar_prefetch=5,
            grid=(N // T,),
            in_specs=[pl.BlockSpec((T, D), lambda t, *_: (t, 0)),
                      pl.BlockSpec((T, E_LANES), lambda t, *_: (t, 0))],
            out_specs=pl.BlockSpec(memory_space=pl.ANY),
            scratch_shapes=[pltpu.VMEM((2, SLOT_ROWS, D), F32),
                            pltpu.VMEM((EXPERT_TM // 2, D), F32),
                            pltpu.SemaphoreType.DMA((3,))]),
        compiler_params=pltpu.CompilerParams(
            dimension_semantics=("arbitrary",),
            vmem_limit_bytes=V7X_VMEM_LIMIT_BYTES),
        name="moe_dispatch",
    )(gdst, glen, gsrc, tail0, tailn, h, info)


def _experts_kernel(te_ref, nv_ref, ns_ref, x_ref, wg_ref, wu_ref, wd_ref, o_ref,
                    xb, wgb, wub, wdb):
    i = pl.program_id(0)
    k = pl.program_id(1)
    n_sub = ns_ref[i]

    full = x_ref.shape[0] // EXPERT_SUB

    @pl.when((n_sub > 0) & (k == 0))
    def _():
        xb[...] = x_ref[...].astype(BF16)
        o_ref[...] = jnp.zeros_like(o_ref)

    @pl.when(n_sub == full)
    def _():
        h = xb[...]
        act = (_silu(_dot(h, wg_ref[0].astype(BF16))) * _dot(h, wu_ref[0].astype(BF16))).astype(BF16)
        o_ref[...] += _dot(act, wd_ref[0].astype(BF16))

    @pl.when((n_sub > 0) & (n_sub < full))
    def _():
        wgb[...] = wg_ref[0].astype(BF16)
        wub[...] = wu_ref[0].astype(BF16)
        wdb[...] = wd_ref[0].astype(BF16)

        def piece(s, carry):
            rows = pl.ds(pl.multiple_of(s * EXPERT_SUB, EXPERT_SUB), EXPERT_SUB)
            h = xb[rows, :]
            act = (_silu(_dot(h, wgb[...])) * _dot(h, wub[...])).astype(BF16)
            o_ref[rows, :] += _dot(act, wdb[...])
            return carry

        lax.fori_loop(0, n_sub, piece, 0)

    @pl.when((n_sub == 0) & (k == 0))
    def _():
        o_ref[...] = jnp.zeros_like(o_ref)


def _experts(xs, tile_expert, n_valid, n_sub, wg, wu, wd):
    M, D = xs.shape
    tm, fk = EXPERT_TM, EXPERT_FK
    E, _, F = wg.shape
    nk = F // fk

    def row_map(i, k, te, nv, ns):
        return (jnp.maximum(jnp.minimum(i, nv[0] - 1), 0), 0)

    def kk(i, k, nv):
        return jnp.where(i < nv[0], k, nk - 1)

    return pl.pallas_call(
        _experts_kernel,
        out_shape=jax.ShapeDtypeStruct((M, D), F32),
        grid_spec=pltpu.PrefetchScalarGridSpec(
            num_scalar_prefetch=3,
            grid=(M // tm, nk),
            in_specs=[
                pl.BlockSpec((tm, D), row_map),
                pl.BlockSpec((1, D, fk), lambda i, k, te, nv, ns: (te[i], 0, kk(i, k, nv))),
                pl.BlockSpec((1, D, fk), lambda i, k, te, nv, ns: (te[i], 0, kk(i, k, nv))),
                pl.BlockSpec((1, fk, D), lambda i, k, te, nv, ns: (te[i], kk(i, k, nv), 0)),
            ],
            out_specs=pl.BlockSpec((tm, D), lambda i, k, te, nv, ns: (i, 0)),
            scratch_shapes=[pltpu.VMEM((tm, D), BF16),
                            pltpu.VMEM((D, fk), BF16), pltpu.VMEM((D, fk), BF16),
                            pltpu.VMEM((fk, D), BF16)]),
        compiler_params=pltpu.CompilerParams(
            dimension_semantics=("arbitrary", "arbitrary"),
            vmem_limit_bytes=V7X_VMEM_LIMIT_BYTES),
        name="moe_experts",
    )(tile_expert, n_valid, n_sub, xs, wg, wu, wd)


def _combine_kernel(gdst_ref, glen_ref, gsrc_ref, x_ref, info_ref, gf_ref, ys_hbm, o_ref,
                    ybuf, sem):
    t = pl.program_id(0)
    n_t = pl.num_programs(0)
    T = x_ref.shape[0]
    slot = t % 2
    tables = (gdst_ref, glen_ref, gsrc_ref)
    mine = _tile_runs(t, tables, ys_hbm, ybuf.at[slot], sem.at[slot], to_hbm=False)
    _start(mine, t == 0)
    ahead = _tile_runs(jnp.minimum(t + 1, n_t - 1), tables, ys_hbm, ybuf.at[1 - slot],
                       sem.at[1 - slot], to_hbm=False)
    _start(ahead, t + 1 < n_t)
    _wait(mine)
    last = t * N_EXPERTS + N_EXPERTS - 1
    used = gsrc_ref[last] + glen_ref[last]
    rows = lax.broadcasted_iota(jnp.int32, (SLOT_ROWS, 1), 0)
    y = jnp.where(rows < used, ybuf[slot], 0.0).astype(BF16)
    info = info_ref[...]
    hit1, hit2 = _slots(info, T)
    weights = (jnp.where(hit1, info[:, 2:3], 0.0) + jnp.where(hit2, info[:, 3:4], 0.0)).astype(BF16)
    o_ref[...] = _rms(x_ref[...] + _dot(weights, y), gf_ref[...])


def _combine(x, info, gf, ys, gdst, glen, gsrc):
    N, D = x.shape
    T = ROUTE_T
    return pl.pallas_call(
        _combine_kernel,
        out_shape=jax.ShapeDtypeStruct((N, D), F32),
        grid_spec=pltpu.PrefetchScalarGridSpec(
            num_scalar_prefetch=3,
            grid=(N // T,),
            in_specs=[pl.BlockSpec((T, D), lambda t, *_: (t, 0)),
                      pl.BlockSpec((T, E_LANES), lambda t, *_: (t, 0)),
                      pl.BlockSpec(gf.shape, lambda t, *_: (0, 0)),
                      pl.BlockSpec(memory_space=pl.ANY)],
            out_specs=pl.BlockSpec((T, D), lambda t, *_: (t, 0)),
            scratch_shapes=[pltpu.VMEM((2, SLOT_ROWS, D), F32),
                            pltpu.SemaphoreType.DMA((2,))]),
        compiler_params=pltpu.CompilerParams(
            dimension_semantics=("arbitrary",),
            vmem_limit_bytes=V7X_VMEM_LIMIT_BYTES),
        name="moe_combine",
    )(gdst, glen, gsrc, x, info, gf, ys)


def _moe(x, g, router, wg, wu, wd, gf):
    N, D = x.shape
    n_t = N // ROUTE_T
    tm = EXPERT_TM
    router_p = jnp.pad(router, ((0, 0), (0, E_LANES - N_EXPERTS)))
    h, info, cnt = _route(x, g, router_p)
    glen = cnt[:, 0, :N_EXPERTS].astype(jnp.int32)
    total = jnp.sum(glen, axis=0)
    gpad = (total + tm - 1) // tm * tm
    gend = jnp.cumsum(gpad)
    goff = gend - gpad
    gdst = goff[None, :] + jnp.cumsum(glen, axis=0) - glen
    gsrc = jnp.cumsum(glen, axis=1) - glen
    rows_max = -(-(2 * N + n_t * N_EXPERTS * (RUN_ALIGN - 1) + N_EXPERTS * (tm - RUN_ALIGN)) // tm) * tm
    n_tiles = rows_max // tm
    n_valid = (gend[-1] // tm).reshape(1)
    tile_row = jnp.minimum(jnp.arange(n_tiles, dtype=jnp.int32), n_valid[0] - 1) * tm
    tile_expert = jnp.sum((tile_row[:, None] >= gend[None, :]).astype(jnp.int32), axis=1)
    tile_rows = jnp.clip((goff + total)[tile_expert] - tile_row, 0, tm)
    tile_rows = jnp.where(jnp.arange(n_tiles) < n_valid[0], tile_rows, 0)
    n_sub = (tile_rows + EXPERT_SUB - 1) // EXPERT_SUB
    flat = lambda a: a.reshape(-1).astype(jnp.int32)
    tail0 = jnp.concatenate([goff + total, gend[-1:]])
    tailn = jnp.concatenate([gpad - total, (rows_max - gend[-1:]) // (tm // 2)])
    xs = _dispatch(h, info, flat(gdst), flat(glen), flat(gsrc), flat(tail0), flat(tailn), rows_max)
    ys = _experts(xs, flat(tile_expert), flat(n_valid), flat(n_sub), wg, wu, wd)
    return _combine(x, info, gf, ys, flat(gdst), flat(glen), flat(gsrc))


def kernel(x, norm1_g, w_in, pool_w, pool_scale, gm_norm_g, gm_ws, gm_b,
           conv_dw_w, conv_dw_b, conv_ln_g, conv_ln_b, conv_pw_w, conv_pw_b,
           w_out, norm2_g, ffn_wg, ffn_wu, ffn_wd,
           moe_router, moe_wg, moe_wu, moe_wd, final_g):
    B, S, D = x.shape
    depth = w_in.shape[0]
    assert depth == 2, "layer 0 is the dense SwiGLU layer, layer 1 the expert layer + final norm"
    row = lambda t: t.reshape(1, -1)
    for l in range(depth):
        pool_bd = jax.scipy.linalg.block_diag(*[pool_w[l, gi] for gi in range(len(POOL_WINDOWS))])
        gm_wcat = jnp.transpose(gm_ws[l], (1, 0, 2)).reshape(GMLP_BLOCK, GMLP_HEADS * GMLP_BLOCK)
        gm_bias = jnp.repeat(gm_b[l].T, GMLP_HEAD_DIM, axis=1)
        x = _mixer(x, row(norm1_g[l]), w_in[l].astype(BF16), pool_bd.astype(BF16),
                   row(pool_scale[l]), row(gm_norm_g[l]), gm_wcat, gm_bias,
                   conv_dw_w[l], row(conv_dw_b[l]), row(conv_ln_g[l]), row(conv_ln_b[l]),
                   conv_pw_w[l].astype(BF16), row(conv_pw_b[l]), w_out[l].astype(BF16))
        xf = x.reshape(B * S, D)
        j = l // 2
        if l % 2 == 0:
            xf = _ffn(xf, row(norm2_g[l]), ffn_wg[j].astype(BF16), ffn_wu[j].astype(BF16),
                      ffn_wd[j].astype(BF16))
        else:
            xf = _moe(xf, row(norm2_g[l]), moe_router[j], moe_wg[j], moe_wu[j], moe_wd[j],
                      row(final_g))
        x = xf.reshape(B, S, D)
    return x
```

```python
import functools

import jax
import jax.numpy as jnp
from jax import lax
from jax.experimental import pallas as pl
from jax.experimental.pallas import tpu as pltpu

D_MODEL = 1024
CHUNK = 64
D_POOL = 256
POOL_WINDOWS = (2, 4, 8, 16)
POOL_GROUP = 64
D_GMLP = 384
GMLP_HEADS = 4
GMLP_HEAD_DIM = 96
GMLP_BLOCK = 128
D_CONV = 384
CONV_WIDTH = 31
D_IN = D_POOL + 2 * D_GMLP + 2 * D_CONV
D_FF = 2816
N_EXPERTS = 8
D_FF_EXPERT = 3584
EPS = 1e-6

V7X_VMEM_LIMIT_BYTES = 56 * 1024 * 1024

POOL_HALO = 16
CONV_HALO = 32
MIX_TM = 512
CONV_ROWS = 64
FFN_TM = 512
ROUTE_T = 512
EXPERT_TM = 1024
EXPERT_FK = 512
EXPERT_SUB = 256

BF16 = jnp.bfloat16
F32 = jnp.float32


def _rms(x, g):
    return x * lax.rsqrt(jnp.mean(x * x, axis=-1, keepdims=True) + EPS) * g


def _silu(x):
    return x * jax.nn.sigmoid(x)


def _dot(a, b):
    return jnp.dot(a, b, preferred_element_type=F32)


def _mixer_kernel(x_ref, g1_ref, win_ref, poolw_ref, pools_ref, gmg_ref,
                  gmws_ref, gmb_ref, dww_ref, dwb_ref, lng_ref, lnb_ref,
                  pww_ref, pwb_ref, wout_ref, o_ref, abuf, hbuf, hshift, wtap, ycat):
    tm = x_ref.shape[1]
    s = pl.program_id(1)

    @pl.when(s == 0)
    def _():
        abuf[0:POOL_HALO, :] = jnp.zeros((POOL_HALO, D_POOL), F32)
        hbuf[0:CONV_HALO, :] = jnp.zeros((CONV_HALO, D_CONV), F32)
        for j in range(CONV_WIDTH):
            wtap[j] = jnp.broadcast_to(dww_ref[j:j + 1, :], (8, D_CONV))

    x = x_ref[0]
    h = _rms(x, g1_ref[...])
    p = _dot(h.astype(BF16), win_ref[...])
    o1 = D_POOL
    o2 = o1 + D_GMLP
    o3 = o2 + D_GMLP
    o4 = o3 + D_CONV
    u = p[:, o1:o2]
    v = p[:, o2:o3]
    cv = p[:, o3:o4]
    cg = p[:, o4:]

    abuf[POOL_HALO:POOL_HALO + tm, :] = p[:, :o1]

    def shifted(j, lo):
        return abuf[POOL_HALO - j:POOL_HALO - j + tm, lo:lo + 128]

    lane = lax.broadcasted_iota(jnp.int32, (1, 128), 1)
    row = lax.broadcasted_iota(jnp.int32, (tm, 1), 0)
    tpos = (s * tm + row + 1).astype(F32)
    ys = []
    for half, (w_small, w_big) in enumerate(((2, 4), (8, 16))):
        lo = half * 128
        acc = shifted(0, lo)
        for j in range(1, w_small):
            acc = acc + shifted(j, lo)
        s_small = acc
        for j in range(w_small, w_big):
            acc = acc + shifted(j, lo)
        first = lane < POOL_GROUP
        ssel = jnp.where(first, s_small, acc)
        win = jnp.where(first, float(w_small), float(w_big))
        mean = ssel / jnp.minimum(tpos, win)
        ys.append(mean - shifted(0, lo))
    y = jnp.concatenate(ys, axis=1).astype(BF16)
    ya = _dot(y, poolw_ref[...]) * pools_ref[...]
    ycat[:, 0:o1] = ya.astype(BF16)
    abuf[0:POOL_HALO, :] = abuf[tm:tm + POOL_HALO, :]

    vn = _rms(v, gmg_ref[...])
    ri = lax.broadcasted_iota(jnp.int32, (GMLP_BLOCK, GMLP_HEADS * GMLP_BLOCK), 0)
    cj = lax.broadcasted_iota(jnp.int32, (GMLP_BLOCK, GMLP_HEADS * GMLP_BLOCK), 1)
    causal = (ri // CHUNK) >= ((cj % GMLP_BLOCK) // CHUNK)
    wcat = jnp.where(causal, gmws_ref[...], 0.0).astype(BF16)
    hid = lax.broadcasted_iota(jnp.int32, (1, D_GMLP), 1) // GMLP_HEAD_DIM
    gmb = gmb_ref[...]
    for blk in range(tm // GMLP_BLOCK):
        r0 = blk * GMLP_BLOCK
        vb = vn[r0:r0 + GMLP_BLOCK, :]
        vstack = jnp.concatenate(
            [jnp.where(hid == hh, vb, 0.0) for hh in range(GMLP_HEADS)],
            axis=0).astype(BF16)
        z = _dot(wcat, vstack) + gmb
        ycat[r0:r0 + GMLP_BLOCK, o1:o2] = (u[r0:r0 + GMLP_BLOCK, :] * z).astype(BF16)

    hbuf[CONV_HALO:CONV_HALO + tm, :] = cv * jax.nn.sigmoid(cg)
    base = CONV_HALO - (CONV_WIDTH - 1)
    dwb = dwb_ref[...]
    lng = lng_ref[...]
    lnb = lnb_ref[...]
    pwb = pwb_ref[...]
    for b in range(8):
        n_rows = tm + 8 * (len(range(b, CONV_WIDTH, 8)) - 1)
        hshift[b, 0:n_rows, :] = hbuf[base + b:base + b + n_rows, :]
    for c0 in range(0, tm, CONV_ROWS):
        acc = None
        for j in range(CONV_WIDTH):
            r0 = c0 + 8 * (j // 8)
            win = hshift[j % 8, r0:r0 + CONV_ROWS, :].reshape(CONV_ROWS // 8, 8, D_CONV)
            term = wtap[j] * win
            acc = term if acc is None else acc + term
        acc = acc.reshape(CONV_ROWS, D_CONV) + dwb
        mu = jnp.mean(acc, axis=-1, keepdims=True)
        cen = acc - mu
        var = jnp.mean(cen * cen, axis=-1, keepdims=True)
        ln = cen * lax.rsqrt(var + EPS) * lng + lnb
        yc = _dot(_silu(ln).astype(BF16), pww_ref[...]) + pwb
        ycat[c0:c0 + CONV_ROWS, o2:D_MODEL] = yc.astype(BF16)
    hbuf[0:CONV_HALO, :] = hbuf[tm:tm + CONV_HALO, :]

    o_ref[0] = x + _dot(ycat[...], wout_ref[...])


def _const_spec(shape):
    nd = len(shape)
    return pl.BlockSpec(shape, lambda b, s: (0,) * nd)


def _mixer(x, g1, w_in, pool_bd, pool_scale, gm_g, gm_wcat, gm_bias, dw_w, dw_b,
           ln_g, ln_b, pw_w, pw_b, w_out):
    B, S, D = x.shape
    tm = MIX_TM
    weights = (g1, w_in, pool_bd, pool_scale, gm_g, gm_wcat, gm_bias, dw_w, dw_b,
               ln_g, ln_b, pw_w, pw_b, w_out)
    return pl.pallas_call(
        _mixer_kernel,
        out_shape=jax.ShapeDtypeStruct((B, S, D), F32),
        grid=(B, S // tm),
        in_specs=[pl.BlockSpec((1, tm, D), lambda b, s: (b, s, 0))]
        + [_const_spec(w.shape) for w in weights],
        out_specs=pl.BlockSpec((1, tm, D), lambda b, s: (b, s, 0)),
        scratch_shapes=[
            pltpu.VMEM((POOL_HALO + tm, D_POOL), F32),
            pltpu.VMEM((CONV_HALO + tm, D_CONV), F32),
            pltpu.VMEM((8, tm + CONV_HALO - 8, D_CONV), F32),
            pltpu.VMEM((CONV_WIDTH, 8, D_CONV), F32),
            pltpu.VMEM((tm, D_MODEL), BF16),
        ],
        compiler_params=pltpu.CompilerParams(
            dimension_semantics=("arbitrary", "arbitrary"),
            vmem_limit_bytes=V7X_VMEM_LIMIT_BYTES),
        name="token_mixer",
    )(x, *weights)


def _ffn_kernel(x_ref, g_ref, wg_ref, wu_ref, wd_ref, o_ref):
    x = x_ref[...]
    h = _rms(x, g_ref[...]).astype(BF16)
    gate = _dot(h, wg_ref[...])
    up = _dot(h, wu_ref[...])
    act = (_silu(gate) * up).astype(BF16)
    o_ref[...] = x + _dot(act, wd_ref[...])


def _ffn(x, g, wg, wu, wd):
    N, D = x.shape
    tm = FFN_TM
    const = lambda shape: pl.BlockSpec(shape, lambda i: (0, 0))
    return pl.pallas_call(
        _ffn_kernel,
        out_shape=jax.ShapeDtypeStruct((N, D), F32),
        grid=(N // tm,),
        in_specs=[pl.BlockSpec((tm, D), lambda i: (i, 0)), const(g.shape),
                  const(wg.shape), const(wu.shape), const(wd.shape)],
        out_specs=pl.BlockSpec((tm, D), lambda i: (i, 0)),
        compiler_params=pltpu.CompilerParams(
            dimension_semantics=("arbitrary",),
            vmem_limit_bytes=V7X_VMEM_LIMIT_BYTES),
        name="dense_swiglu",
    )(x, g, wg, wu, wd)


E_LANES = 128
RUN_ALIGN = 8
SLOT_ROWS = 2 * ROUTE_T + RUN_ALIGN * N_EXPERTS
RUN_BITS = tuple(range((ROUTE_T // RUN_ALIGN).bit_length() - 1, -1, -1))
TAIL_BITS = tuple(range((EXPERT_TM // RUN_ALIGN).bit_length() - 2, -1, -1))
SLOT_BITS = tuple(range((SLOT_ROWS // RUN_ALIGN).bit_length() - 1, -1, -1))


def _route_kernel(x_ref, g_ref, router_ref, h_ref, info_ref, cnt_ref):
    T = x_ref.shape[0]
    h = _rms(x_ref[...], g_ref[...])
    h_hi = h.astype(BF16)
    h_ref[...] = h_hi
    h_lo = (h - h_hi.astype(F32)).astype(BF16)
    r = router_ref[...]
    r_hi = r.astype(BF16)
    r_lo = (r - r_hi.astype(F32)).astype(BF16)
    logits = _dot(h_hi, r_hi) + (_dot(h_lo, r_hi) + _dot(h_hi, r_lo))
    lane = lax.broadcasted_iota(jnp.int32, (T, E_LANES), 1)
    logits = jnp.where(lane < N_EXPERTS, logits, -jnp.inf)
    m1 = jnp.max(logits, axis=-1, keepdims=True)
    i1 = jnp.min(jnp.where(logits == m1, lane, E_LANES), axis=-1, keepdims=True)
    rest = jnp.where(lane == i1, -jnp.inf, logits)
    m2 = jnp.max(rest, axis=-1, keepdims=True)
    i2 = jnp.min(jnp.where(rest == m2, lane, E_LANES), axis=-1, keepdims=True)
    e2 = jnp.exp(m2 - m1)
    w1 = 1.0 / (1.0 + e2)
    w2 = e2 / (1.0 + e2)
    oh1 = (lane == i1).astype(F32)
    oh2 = (lane == i2).astype(F32)
    oh = oh1 + oh2
    ri = lax.broadcasted_iota(jnp.int32, (T, T), 0)
    ci = lax.broadcasted_iota(jnp.int32, (T, T), 1)
    rank = _dot((ri > ci).astype(BF16), oh.astype(BF16))
    cnt = jnp.sum(oh, axis=0, keepdims=True)
    padc = jnp.floor((cnt + (RUN_ALIGN - 1.0)) * (1.0 / RUN_ALIGN)) * RUN_ALIGN
    lane1 = lax.broadcasted_iota(jnp.int32, (1, E_LANES), 1)
    start = jnp.zeros((1, E_LANES), F32)
    for e in range(N_EXPERTS - 1):
        start = start + jnp.where(lane1 > e, padc[:, e:e + 1], 0.0)
    slot = start + rank
    q1 = jnp.sum(oh1 * slot, axis=-1, keepdims=True)
    q2 = jnp.sum(oh2 * slot, axis=-1, keepdims=True)
    info_ref[...] = jnp.where(lane == 0, q1, jnp.where(lane == 1, q2, jnp.where(
        lane == 2, w1, jnp.where(lane == 3, w2, 0.0))))
    cnt_ref[0] = padc


def _route(x, g, router_p):
    N, D = x.shape
    T = ROUTE_T
    n_t = N // T
    return pl.pallas_call(
        _route_kernel,
        out_shape=(jax.ShapeDtypeStruct((N, D), BF16),
                   jax.ShapeDtypeStruct((N, E_LANES), F32),
                   jax.ShapeDtypeStruct((n_t, 1, E_LANES), F32)),
        grid=(n_t,),
        in_specs=[pl.BlockSpec((T, D), lambda t: (t, 0)),
                  pl.BlockSpec(g.shape, lambda t: (0, 0)),
                  pl.BlockSpec(router_p.shape, lambda t: (0, 0))],
        out_specs=(pl.BlockSpec((T, D), lambda t: (t, 0)),
                   pl.BlockSpec((T, E_LANES), lambda t: (t, 0)),
                   pl.BlockSpec((1, 1, E_LANES), lambda t: (t, 0, 0))),
        compiler_params=pltpu.CompilerParams(
            dimension_semantics=("arbitrary",),
            vmem_limit_bytes=V7X_VMEM_LIMIT_BYTES),
        name="moe_route",
    )(x, g, router_p)


def _run_copies(src, dst, src0, dst0, rows, bits, sem):
    m = rows // RUN_ALIGN
    off = 0
    out = []
    for b in bits:
        size = RUN_ALIGN << b
        take = (m >> b) & 1
        s0 = 0 if src0 is None else pl.multiple_of(src0 + off, RUN_ALIGN)
        d0 = pl.multiple_of(dst0 + off, RUN_ALIGN)
        out.append((take == 1, pltpu.make_async_copy(
            src.at[pl.ds(s0, size)], dst.at[pl.ds(d0, size)], sem)))
        off = off + take * size
    return out


def _start(copies, enable=True):
    for pred, cp in copies:
        @pl.when(pred & enable)
        def _(cp=cp):
            cp.start()


def _wait(copies, enable=True):
    for pred, cp in copies:
        @pl.when(pred & enable)
        def _(cp=cp):
            cp.wait()


def _tile_runs(tile, table_refs, hbm, buf, sem, *, to_hbm):
    gdst_ref, glen_ref, gsrc_ref = table_refs
    copies = []
    for e in range(N_EXPERTS):
        j = tile * N_EXPERTS + e
        if to_hbm:
            copies += _run_copies(buf, hbm, gsrc_ref[j], gdst_ref[j], glen_ref[j], RUN_BITS, sem)
        else:
            copies += _run_copies(hbm, buf, gdst_ref[j], gsrc_ref[j], glen_ref[j], RUN_BITS, sem)
    return copies


def _tile_rows(tile, table_refs):
    _, glen_ref, gsrc_ref = table_refs
    last = tile * N_EXPERTS + N_EXPERTS - 1
    return gsrc_ref[last] + glen_ref[last]


def _wait_rows(rows, src, dst, sem):
    for pred, cp in _run_copies(src, dst, jnp.int32(0), jnp.int32(0), rows, SLOT_BITS, sem):
        @pl.when(pred)
        def _(cp=cp):
            cp.wait()


def _slots(info, T):
    q1 = info[:, 0:1].astype(jnp.int32)
    q2 = info[:, 1:2].astype(jnp.int32)
    r = lax.broadcasted_iota(jnp.int32, (T, SLOT_ROWS), 1)
    return q1 == r, q2 == r


def _dispatch_kernel(gdst_ref, glen_ref, gsrc_ref, tail0_ref, tailn_ref,
                     h_ref, info_ref, xs_hbm, xbuf, zbuf, sem, *, slack_pieces):
    t = pl.program_id(0)
    n_t = pl.num_programs(0)
    T = h_ref.shape[0]
    slot = t % 2
    tables = (gdst_ref, glen_ref, gsrc_ref)
    hit1, hit2 = _slots(info_ref[...], T)
    onehot = (hit1 | hit2).astype(BF16)
    xbuf[slot] = lax.dot_general(onehot, h_ref[...], (((0,), (0,)), ((), ())),
                                 preferred_element_type=F32)
    _start(_tile_runs(t, tables, xs_hbm, xbuf.at[slot], sem.at[slot], to_hbm=True))

    @pl.when(t > 0)
    def _():
        _wait_rows(_tile_rows(t - 1, tables), xbuf.at[1 - slot], xs_hbm, sem.at[1 - slot])

    @pl.when(t == n_t - 1)
    def _():
        _wait_rows(_tile_rows(t, tables), xbuf.at[slot], xs_hbm, sem.at[slot])

    @pl.when(t == 0)
    def _():
        zbuf[...] = jnp.zeros_like(zbuf)
        tails = []
        for e in range(N_EXPERTS):
            tails += _run_copies(zbuf, xs_hbm, None, tail0_ref[e], tailn_ref[e], TAIL_BITS,
                                 sem.at[2])
        piece = zbuf.shape[0]
        for c in range(slack_pieces):
            d0 = pl.multiple_of(tail0_ref[N_EXPERTS] + c * piece, RUN_ALIGN)
            tails.append((c < tailn_ref[N_EXPERTS], pltpu.make_async_copy(
                zbuf, xs_hbm.at[pl.ds(d0, piece)], sem.at[2])))
        _start(tails)
        _wait(tails)


def _dispatch(h, info, gdst, glen, gsrc, tail0, tailn, rows_max):
    N, D = h.shape
    T = ROUTE_T
    slack_pieces = (rows_max - 2 * N) // (EXPERT_TM // 2)
    return pl.pallas_call(
        functools.partial(_dispatch_kernel, slack_pieces=slack_pieces),
        out_shape=jax.ShapeDtypeStruct((rows_max, D), F32),
        grid_spec=pltpu.PrefetchScalarGridSpec(
            num_scalar_prefetch=5,
            grid=(N // T,),
            in_specs=[pl.BlockSpec((T, D), lambda t, *_: (t, 0)),
                      pl.BlockSpec((T, E_LANES), lambda t, *_: (t, 0))],
            out_specs=pl.BlockSpec(memory_space=pl.ANY),
            scratch_shapes=[pltpu.VMEM((2, SLOT_ROWS, D), F32),
                            pltpu.VMEM((EXPERT_TM // 2, D), F32),
                            pltpu.SemaphoreType.DMA((3,))]),
        compiler_params=pltpu.CompilerParams(
            dimension_semantics=("arbitrary",),
            vmem_limit_bytes=V7X_VMEM_LIMIT_BYTES),
        name="moe_dispatch",
    )(gdst, glen, gsrc, tail0, tailn, h, info)


def _experts_kernel(te_ref, nv_ref, ns_ref, x_ref, wg_ref, wu_ref, wd_ref, o_ref,
                    xb, wgb, wub, wdb):
    i = pl.program_id(0)
    k = pl.program_id(1)
    n_sub = ns_ref[i]

    full = x_ref.shape[0] // EXPERT_SUB

    @pl.when((n_sub > 0) & (k == 0))
    def _():
        xb[...] = x_ref[...].astype(BF16)
        o_ref[...] = jnp.zeros_like(o_ref)

    @pl.when(n_sub == full)
    def _():
        h = xb[...]
        act = (_silu(_dot(h, wg_ref[0].astype(BF16))) * _dot(h, wu_ref[0].astype(BF16))).astype(BF16)
        o_ref[...] += _dot(act, wd_ref[0].astype(BF16))

    @pl.when((n_sub > 0) & (n_sub < full))
    def _():
        wgb[...] = wg_ref[0].astype(BF16)
        wub[...] = wu_ref[0].astype(BF16)
        wdb[...] = wd_ref[0].astype(BF16)

        def piece(s, carry):
            rows = pl.ds(pl.multiple_of(s * EXPERT_SUB, EXPERT_SUB), EXPERT_SUB)
            h = xb[rows, :]
            act = (_silu(_dot(h, wgb[...])) * _dot(h, wub[...])).astype(BF16)
            o_ref[rows, :] += _dot(act, wdb[...])
            return carry

        lax.fori_loop(0, n_sub, piece, 0)

    @pl.when((n_sub == 0) & (k == 0))
    def _():
        o_ref[...] = jnp.zeros_like(o_ref)


def _experts(xs, tile_expert, n_valid, n_sub, wg, wu, wd):
    M, D = xs.shape
    tm, fk = EXPERT_TM, EXPERT_FK
    E, _, F = wg.shape
    nk = F // fk

    def row_map(i, k, te, nv, ns):
        return (jnp.maximum(jnp.minimum(i, nv[0] - 1), 0), 0)

    def kk(i, k, nv):
        return jnp.where(i < nv[0], k, nk - 1)

    return pl.pallas_call(
        _experts_kernel,
        out_shape=jax.ShapeDtypeStruct((M, D), F32),
        grid_spec=pltpu.PrefetchScalarGridSpec(
            num_scalar_prefetch=3,
            grid=(M // tm, nk),
            in_specs=[
                pl.BlockSpec((tm, D), row_map),
                pl.BlockSpec((1, D, fk), lambda i, k, te, nv, ns: (te[i], 0, kk(i, k, nv))),
                pl.BlockSpec((1, D, fk), lambda i, k, te, nv, ns: (te[i], 0, kk(i, k, nv))),
                pl.BlockSpec((1, fk, D), lambda i, k, te, nv, ns: (te[i], kk(i, k, nv), 0)),
            ],
            out_specs=pl.BlockSpec((tm, D), lambda i, k, te, nv, ns: (i, 0)),
            scratch_shapes=[pltpu.VMEM((tm, D), BF16),
                            pltpu.VMEM((D, fk), BF16), pltpu.VMEM((D, fk), BF16),
                            pltpu.VMEM((fk, D), BF16)]),
        compiler_params=pltpu.CompilerParams(
            dimension_semantics=("arbitrary", "arbitrary"),
            vmem_limit_bytes=V7X_VMEM_LIMIT_BYTES),
        name="moe_experts",
    )(tile_expert, n_valid, n_sub, xs, wg, wu, wd)


def _combine_kernel(gdst_ref, glen_ref, gsrc_ref, x_ref, info_ref, gf_ref, ys_hbm, o_ref,
                    ybuf, sem):
    t = pl.program_id(0)
    n_t = pl.num_programs(0)
    T = x_ref.shape[0]
    slot = t % 2
    tables = (gdst_ref, glen_ref, gsrc_ref)
    @pl.when(t == 0)
    def _():
        _start(_tile_runs(t, tables, ys_hbm, ybuf.at[slot], sem.at[slot], to_hbm=False))

    @pl.when(t + 1 < n_t)
    def _():
        _start(_tile_runs(t + 1, tables, ys_hbm, ybuf.at[1 - slot], sem.at[1 - slot],
                          to_hbm=False))

    used = _tile_rows(t, tables)
    _wait_rows(used, ys_hbm, ybuf.at[slot], sem.at[slot])
    rows = lax.broadcasted_iota(jnp.int32, (SLOT_ROWS, 1), 0)
    y = jnp.where(rows < used, ybuf[slot], 0.0).astype(BF16)
    info = info_ref[...]
    hit1, hit2 = _slots(info, T)
    weights = (jnp.where(hit1, info[:, 2:3], 0.0) + jnp.where(hit2, info[:, 3:4], 0.0)).astype(BF16)
    o_ref[...] = _rms(x_ref[...] + _dot(weights, y), gf_ref[...])


def _combine(x, info, gf, ys, gdst, glen, gsrc):
    N, D = x.shape
    T = ROUTE_T
    return pl.pallas_call(
        _combine_kernel,
        out_shape=jax.ShapeDtypeStruct((N, D), F32),
        grid_spec=pltpu.PrefetchScalarGridSpec(
            num_scalar_prefetch=3,
            grid=(N // T,),
            in_specs=[pl.BlockSpec((T, D), lambda t, *_: (t, 0)),
                      pl.BlockSpec((T, E_LANES), lambda t, *_: (t, 0)),
                      pl.BlockSpec(gf.shape, lambda t, *_: (0, 0)),
                      pl.BlockSpec(memory_space=pl.ANY)],
            out_specs=pl.BlockSpec((T, D), lambda t, *_: (t, 0)),
            scratch_shapes=[pltpu.VMEM((2, SLOT_ROWS, D), F32),
                            pltpu.SemaphoreType.DMA((2,))]),
        compiler_params=pltpu.CompilerParams(
            dimension_semantics=("arbitrary",),
            vmem_limit_bytes=V7X_VMEM_LIMIT_BYTES),
        name="moe_combine",
    )(gdst, glen, gsrc, x, info, gf, ys)


def _moe(x, g, router, wg, wu, wd, gf):
    N, D = x.shape
    n_t = N // ROUTE_T
    tm = EXPERT_TM
    router_p = jnp.pad(router, ((0, 0), (0, E_LANES - N_EXPERTS)))
    h, info, cnt = _route(x, g, router_p)
    glen = cnt[:, 0, :N_EXPERTS].astype(jnp.int32)
    total = jnp.sum(glen, axis=0)
    gpad = (total + tm - 1) // tm * tm
    gend = jnp.cumsum(gpad)
    goff = gend - gpad
    gdst = goff[None, :] + jnp.cumsum(glen, axis=0) - glen
    gsrc = jnp.cumsum(glen, axis=1) - glen
    rows_max = -(-(2 * N + n_t * N_EXPERTS * (RUN_ALIGN - 1) + N_EXPERTS * (tm - RUN_ALIGN)) // tm) * tm
    n_tiles = rows_max // tm
    n_valid = (gend[-1] // tm).reshape(1)
    tile_row = jnp.minimum(jnp.arange(n_tiles, dtype=jnp.int32), n_valid[0] - 1) * tm
    tile_expert = jnp.sum((tile_row[:, None] >= gend[None, :]).astype(jnp.int32), axis=1)
    tile_rows = jnp.clip((goff + total)[tile_expert] - tile_row, 0, tm)
    tile_rows = jnp.where(jnp.arange(n_tiles) < n_valid[0], tile_rows, 0)
    n_sub = (tile_rows + EXPERT_SUB - 1) // EXPERT_SUB
    flat = lambda a: a.reshape(-1).astype(jnp.int32)
    tail0 = jnp.concatenate([goff + total, gend[-1:]])
    tailn = jnp.concatenate([gpad - total, (rows_max - gend[-1:]) // (tm // 2)])
    xs = _dispatch(h, info, flat(gdst), flat(glen), flat(gsrc), flat(tail0), flat(tailn), rows_max)
    ys = _experts(xs, flat(tile_expert), flat(n_valid), flat(n_sub), wg, wu, wd)
    return _combine(x, info, gf, ys, flat(gdst), flat(glen), flat(gsrc))


def kernel(x, norm1_g, w_in, pool_w, pool_scale, gm_norm_g, gm_ws, gm_b,
           conv_dw_w, conv_dw_b, conv_ln_g, conv_ln_b, conv_pw_w, conv_pw_b,
           w_out, norm2_g, ffn_wg, ffn_wu, ffn_wd,
           moe_router, moe_wg, moe_wu, moe_wd, final_g):
    B, S, D = x.shape
    depth = w_in.shape[0]
    assert depth == 2, "layer 0 is the dense SwiGLU layer, layer 1 the expert layer + final norm"
    row = lambda t: t.reshape(1, -1)
    for l in range(depth):
        pool_bd = jax.scipy.linalg.block_diag(*[pool_w[l, gi] for gi in range(len(POOL_WINDOWS))])
        gm_wcat = jnp.transpose(gm_ws[l], (1, 0, 2)).reshape(GMLP_BLOCK, GMLP_HEADS * GMLP_BLOCK)
        gm_bias = jnp.repeat(gm_b[l].T, GMLP_HEAD_DIM, axis=1)
        x = _mixer(x, row(norm1_g[l]), w_in[l].astype(BF16), pool_bd.astype(BF16),
                   row(pool_scale[l]), row(gm_norm_g[l]), gm_wcat, gm_bias,
                   conv_dw_w[l], row(conv_dw_b[l]), row(conv_ln_g[l]), row(conv_ln_b[l]),
                   conv_pw_w[l].astype(BF16), row(conv_pw_b[l]), w_out[l].astype(BF16))
        xf = x.reshape(B * S, D)
        j = l // 2
        if l % 2 == 0:
            xf = _ffn(xf, row(norm2_g[l]), ffn_wg[j].astype(BF16), ffn_wu[j].astype(BF16),
                      ffn_wd[j].astype(BF16))
        else:
            xf = _moe(xf, row(norm2_g[l]), moe_router[j], moe_wg[j], moe_wu[j], moe_wd[j],
                      row(final_g))
        x = xf.reshape(B, S, D)
    return x
```

```python
import functools

import jax
import jax.numpy as jnp
from jax import lax
from jax.experimental import pallas as pl
from jax.experimental.pallas import tpu as pltpu

D_MODEL = 1024
CHUNK = 64
D_POOL = 256
POOL_WINDOWS = (2, 4, 8, 16)
POOL_GROUP = 64
D_GMLP = 384
GMLP_HEADS = 4
GMLP_HEAD_DIM = 96
GMLP_BLOCK = 128
D_CONV = 384
CONV_WIDTH = 31
D_IN = D_POOL + 2 * D_GMLP + 2 * D_CONV
D_FF = 2816
N_EXPERTS = 8
D_FF_EXPERT = 3584
EPS = 1e-6

V7X_VMEM_LIMIT_BYTES = 56 * 1024 * 1024

POOL_PAD = 8
POOL_HALO = POOL_PAD + 16
CONV_HALO = 32
MIX_TM = 1024
CONV_ROWS = 256
FFN_TM = 1024
ROUTE_T = 512
EXPERT_TM = 1024
EXPERT_FK = 512
EXPERT_SUB = 256

BF16 = jnp.bfloat16
F32 = jnp.float32


def _rms(x, g):
    return x * lax.rsqrt(jnp.mean(x * x, axis=-1, keepdims=True) + EPS) * g


def _silu(x):
    return x * jax.nn.sigmoid(x)


def _dot(a, b):
    return jnp.dot(a, b, preferred_element_type=F32)


def _mixer_kernel(x_ref, g1_ref, win_ref, poolw_ref, pools_ref, gmg_ref,
                  gmws_ref, gmb_ref, dww_ref, dwb_ref, lng_ref, lnb_ref,
                  pww_ref, pwb_ref, wout_ref, o_ref, abuf, s2buf, s4buf, s8buf, hbuf, hshift,
                  wtap, ycat):
    tm = x_ref.shape[1]
    s = pl.program_id(1)

    @pl.when(s == 0)
    def _():
        abuf[0:POOL_HALO, :] = jnp.zeros((POOL_HALO, D_POOL), F32)
        s2buf[0:POOL_PAD, :] = jnp.zeros((POOL_PAD, D_POOL), F32)
        s4buf[0:POOL_PAD, :] = jnp.zeros((POOL_PAD, D_POOL), F32)
        s8buf[0:POOL_PAD, :] = jnp.zeros((POOL_PAD, 128), F32)
        hbuf[0:CONV_HALO, :] = jnp.zeros((CONV_HALO, D_CONV), F32)
        for j in range(CONV_WIDTH):
            wtap[j] = jnp.broadcast_to(dww_ref[j:j + 1, :], (8, D_CONV))

    x = x_ref[0]
    h = _rms(x, g1_ref[...])
    p = _dot(h.astype(BF16), win_ref[...])
    o1 = D_POOL
    o2 = o1 + D_GMLP
    o3 = o2 + D_GMLP
    o4 = o3 + D_CONV
    u = p[:, o1:o2]
    v = p[:, o2:o3]
    cv = p[:, o3:o4]
    cg = p[:, o4:]

    H = POOL_HALO
    n = tm + H - POOL_PAD
    abuf[H:H + tm, :] = p[:, :o1]
    s2buf[POOL_PAD:POOL_PAD + n, :] = abuf[POOL_PAD:POOL_PAD + n, :] + abuf[POOL_PAD - 1:POOL_PAD - 1 + n, :]
    s4buf[POOL_PAD:POOL_PAD + n, :] = s2buf[POOL_PAD:POOL_PAD + n, :] + s2buf[POOL_PAD - 2:POOL_PAD - 2 + n, :]
    s8buf[POOL_PAD:POOL_PAD + n, :] = (s4buf[POOL_PAD:POOL_PAD + n, 128:]
                                       + s4buf[POOL_PAD - 4:POOL_PAD - 4 + n, 128:])
    s16 = s8buf[H:H + tm, :] + s8buf[H - 8:H - 8 + tm, :]
    lane = lax.broadcasted_iota(jnp.int32, (1, 128), 1)
    row = lax.broadcasted_iota(jnp.int32, (tm, 1), 0)
    tpos = (s * tm + row + 1).astype(F32)
    first = lane < POOL_GROUP
    ys = []
    for lo, (w_small, w_big), s_small, s_big in (
            (0, POOL_WINDOWS[0:2], s2buf[H:H + tm, 0:128], s4buf[H:H + tm, 0:128]),
            (128, POOL_WINDOWS[2:4], s8buf[H:H + tm, :], s16)):
        ssel = jnp.where(first, s_small, s_big)
        win = jnp.where(first, float(w_small), float(w_big))
        mean = ssel / jnp.minimum(tpos, win)
        ys.append(mean - abuf[H:H + tm, lo:lo + 128])
    y = jnp.concatenate(ys, axis=1).astype(BF16)
    ya = _dot(y, poolw_ref[...]) * pools_ref[...]
    ycat[:, 0:o1] = ya.astype(BF16)
    abuf[POOL_PAD:H, :] = abuf[tm + POOL_PAD:tm + H, :]

    vn = _rms(v, gmg_ref[...])
    ri = lax.broadcasted_iota(jnp.int32, (GMLP_BLOCK, GMLP_HEADS * GMLP_BLOCK), 0)
    cj = lax.broadcasted_iota(jnp.int32, (GMLP_BLOCK, GMLP_HEADS * GMLP_BLOCK), 1)
    causal = (ri // CHUNK) >= ((cj % GMLP_BLOCK) // CHUNK)
    wcat = jnp.where(causal, gmws_ref[...], 0.0).astype(BF16)
    hid = lax.broadcasted_iota(jnp.int32, (1, D_GMLP), 1) // GMLP_HEAD_DIM
    gmb = gmb_ref[...]
    for blk in range(tm // GMLP_BLOCK):
        r0 = blk * GMLP_BLOCK
        vb = vn[r0:r0 + GMLP_BLOCK, :]
        vstack = jnp.concatenate(
            [jnp.where(hid == hh, vb, 0.0) for hh in range(GMLP_HEADS)],
            axis=0).astype(BF16)
        z = _dot(wcat, vstack) + gmb
        ycat[r0:r0 + GMLP_BLOCK, o1:o2] = (u[r0:r0 + GMLP_BLOCK, :] * z).astype(BF16)

    hbuf[CONV_HALO:CONV_HALO + tm, :] = cv * jax.nn.sigmoid(cg)
    base = CONV_HALO - (CONV_WIDTH - 1)
    dwb = dwb_ref[...]
    lng = lng_ref[...]
    lnb = lnb_ref[...]
    pwb = pwb_ref[...]
    for b in range(8):
        n_rows = tm + 8 * (len(range(b, CONV_WIDTH, 8)) - 1)
        hshift[b, 0:n_rows, :] = hbuf[base + b:base + b + n_rows, :]
    for c0 in range(0, tm, CONV_ROWS):
        acc = None
        for j in range(CONV_WIDTH):
            r0 = c0 + 8 * (j // 8)
            win = hshift[j % 8, r0:r0 + CONV_ROWS, :].reshape(CONV_ROWS // 8, 8, D_CONV)
            term = wtap[j] * win
            acc = term if acc is None else acc + term
        acc = acc.reshape(CONV_ROWS, D_CONV) + dwb
        mu = jnp.mean(acc, axis=-1, keepdims=True)
        cen = acc - mu
        var = jnp.mean(cen * cen, axis=-1, keepdims=True)
        ln = cen * lax.rsqrt(var + EPS) * lng + lnb
        yc = _dot(_silu(ln).astype(BF16), pww_ref[...]) + pwb
        ycat[c0:c0 + CONV_ROWS, o2:D_MODEL] = yc.astype(BF16)
    hbuf[0:CONV_HALO, :] = hbuf[tm:tm + CONV_HALO, :]

    o_ref[0] = x + _dot(ycat[...], wout_ref[...])


def _const_spec(shape):
    nd = len(shape)
    return pl.BlockSpec(shape, lambda b, s: (0,) * nd, pipeline_mode=pl.Buffered(1))


def _mixer(x, g1, w_in, pool_bd, pool_scale, gm_g, gm_wcat, gm_bias, dw_w, dw_b,
           ln_g, ln_b, pw_w, pw_b, w_out):
    B, S, D = x.shape
    tm = MIX_TM
    weights = (g1, w_in, pool_bd, pool_scale, gm_g, gm_wcat, gm_bias, dw_w, dw_b,
               ln_g, ln_b, pw_w, pw_b, w_out)
    return pl.pallas_call(
        _mixer_kernel,
        out_shape=jax.ShapeDtypeStruct((B, S, D), F32),
        grid=(B, S // tm),
        in_specs=[pl.BlockSpec((1, tm, D), lambda b, s: (b, s, 0))]
        + [_const_spec(w.shape) for w in weights],
        out_specs=pl.BlockSpec((1, tm, D), lambda b, s: (b, s, 0)),
        scratch_shapes=[
            pltpu.VMEM((POOL_HALO + tm, D_POOL), F32),
            pltpu.VMEM((POOL_HALO + tm, D_POOL), F32),
            pltpu.VMEM((POOL_HALO + tm, D_POOL), F32),
            pltpu.VMEM((POOL_HALO + tm, 128), F32),
            pltpu.VMEM((CONV_HALO + tm, D_CONV), F32),
            pltpu.VMEM((8, tm + CONV_HALO - 8, D_CONV), F32),
            pltpu.VMEM((CONV_WIDTH, 8, D_CONV), F32),
            pltpu.VMEM((tm, D_MODEL), BF16),
        ],
        compiler_params=pltpu.CompilerParams(
            dimension_semantics=("arbitrary", "arbitrary"),
            vmem_limit_bytes=V7X_VMEM_LIMIT_BYTES),
        name="token_mixer",
    )(x, *weights)


def _ffn_kernel(x_ref, g_ref, wg_ref, wu_ref, wd_ref, o_ref):
    x = x_ref[...]
    h = _rms(x, g_ref[...]).astype(BF16)
    gate = _dot(h, wg_ref[...])
    up = _dot(h, wu_ref[...])
    act = (_silu(gate) * up).astype(BF16)
    o_ref[...] = x + _dot(act, wd_ref[...])


def _ffn(x, g, wg, wu, wd):
    N, D = x.shape
    tm = FFN_TM
    const = lambda shape: pl.BlockSpec(shape, lambda i: (0, 0), pipeline_mode=pl.Buffered(1))
    return pl.pallas_call(
        _ffn_kernel,
        out_shape=jax.ShapeDtypeStruct((N, D), F32),
        grid=(N // tm,),
        in_specs=[pl.BlockSpec((tm, D), lambda i: (i, 0)), const(g.shape),
                  const(wg.shape), const(wu.shape), const(wd.shape)],
        out_specs=pl.BlockSpec((tm, D), lambda i: (i, 0)),
        compiler_params=pltpu.CompilerParams(
            dimension_semantics=("arbitrary",),
            vmem_limit_bytes=V7X_VMEM_LIMIT_BYTES),
        name="dense_swiglu",
    )(x, g, wg, wu, wd)


E_LANES = 128
RUN_ALIGN = 8
SLOT_ROWS = 2 * ROUTE_T + RUN_ALIGN * N_EXPERTS
RUN_BITS = tuple(range((ROUTE_T // RUN_ALIGN).bit_length() - 1, -1, -1))
TAIL_BITS = tuple(range((EXPERT_TM // RUN_ALIGN).bit_length() - 2, -1, -1))
SLOT_BITS = tuple(range((SLOT_ROWS // RUN_ALIGN).bit_length() - 1, -1, -1))


def _route_kernel(x_ref, g_ref, router_ref, h_ref, info_ref, cnt_ref):
    T = x_ref.shape[0]
    h = _rms(x_ref[...], g_ref[...])
    h_hi = h.astype(BF16)
    h_ref[...] = h_hi
    h_lo = (h - h_hi.astype(F32)).astype(BF16)
    r = router_ref[...]
    r_hi = r.astype(BF16)
    r_lo = (r - r_hi.astype(F32)).astype(BF16)
    logits = _dot(h_hi, r_hi) + (_dot(h_lo, r_hi) + _dot(h_hi, r_lo))
    lane = lax.broadcasted_iota(jnp.int32, (T, E_LANES), 1)
    logits = jnp.where(lane < N_EXPERTS, logits, -jnp.inf)
    m1 = jnp.max(logits, axis=-1, keepdims=True)
    i1 = jnp.min(jnp.where(logits == m1, lane, E_LANES), axis=-1, keepdims=True)
    rest = jnp.where(lane == i1, -jnp.inf, logits)
    m2 = jnp.max(rest, axis=-1, keepdims=True)
    i2 = jnp.min(jnp.where(rest == m2, lane, E_LANES), axis=-1, keepdims=True)
    e2 = jnp.exp(m2 - m1)
    w1 = 1.0 / (1.0 + e2)
    w2 = e2 / (1.0 + e2)
    oh1 = (lane == i1).astype(F32)
    oh2 = (lane == i2).astype(F32)
    oh = oh1 + oh2
    ri = lax.broadcasted_iota(jnp.int32, (T, T), 0)
    ci = lax.broadcasted_iota(jnp.int32, (T, T), 1)
    rank = _dot((ri > ci).astype(BF16), oh.astype(BF16))
    cnt = jnp.sum(oh, axis=0, keepdims=True)
    padc = jnp.floor((cnt + (RUN_ALIGN - 1.0)) * (1.0 / RUN_ALIGN)) * RUN_ALIGN
    lane1 = lax.broadcasted_iota(jnp.int32, (1, E_LANES), 1)
    start = jnp.zeros((1, E_LANES), F32)
    for e in range(N_EXPERTS - 1):
        start = start + jnp.where(lane1 > e, padc[:, e:e + 1], 0.0)
    slot = start + rank
    q1 = jnp.sum(oh1 * slot, axis=-1, keepdims=True)
    q2 = jnp.sum(oh2 * slot, axis=-1, keepdims=True)
    info_ref[...] = jnp.where(lane == 0, q1, jnp.where(lane == 1, q2, jnp.where(
        lane == 2, w1, jnp.where(lane == 3, w2, 0.0))))
    cnt_ref[0] = padc


def _route(x, g, router_p):
    N, D = x.shape
    T = ROUTE_T
    n_t = N // T
    return pl.pallas_call(
        _route_kernel,
        out_shape=(jax.ShapeDtypeStruct((N, D), BF16),
                   jax.ShapeDtypeStruct((N, E_LANES), F32),
                   jax.ShapeDtypeStruct((n_t, 1, E_LANES), F32)),
        grid=(n_t,),
        in_specs=[pl.BlockSpec((T, D), lambda t: (t, 0)),
                  pl.BlockSpec(g.shape, lambda t: (0, 0)),
                  pl.BlockSpec(router_p.shape, lambda t: (0, 0))],
        out_specs=(pl.BlockSpec((T, D), lambda t: (t, 0)),
                   pl.BlockSpec((T, E_LANES), lambda t: (t, 0)),
                   pl.BlockSpec((1, 1, E_LANES), lambda t: (t, 0, 0))),
        compiler_params=pltpu.CompilerParams(
            dimension_semantics=("arbitrary",),
            vmem_limit_bytes=V7X_VMEM_LIMIT_BYTES),
        name="moe_route",
    )(x, g, router_p)


def _run_copies(src, dst, src0, dst0, rows, bits, sem):
    m = rows // RUN_ALIGN
    off = 0
    out = []
    for b in bits:
        size = RUN_ALIGN << b
        take = (m >> b) & 1
        s0 = 0 if src0 is None else pl.multiple_of(src0 + off, RUN_ALIGN)
        d0 = pl.multiple_of(dst0 + off, RUN_ALIGN)
        out.append((take == 1, pltpu.make_async_copy(
            src.at[pl.ds(s0, size)], dst.at[pl.ds(d0, size)], sem)))
        off = off + take * size
    return out


def _start(copies, enable=True):
    for pred, cp in copies:
        @pl.when(pred & enable)
        def _(cp=cp):
            cp.start()


def _wait(copies, enable=True):
    for pred, cp in copies:
        @pl.when(pred & enable)
        def _(cp=cp):
            cp.wait()


def _tile_runs(tile, table_refs, hbm, buf, sem, *, to_hbm):
    gdst_ref, glen_ref, gsrc_ref = table_refs
    copies = []
    for e in range(N_EXPERTS):
        j = tile * N_EXPERTS + e
        if to_hbm:
            copies += _run_copies(buf, hbm, gsrc_ref[j], gdst_ref[j], glen_ref[j], RUN_BITS, sem)
        else:
            copies += _run_copies(hbm, buf, gdst_ref[j], gsrc_ref[j], glen_ref[j], RUN_BITS, sem)
    return copies


def _tile_rows(tile, table_refs):
    _, glen_ref, gsrc_ref = table_refs
    last = tile * N_EXPERTS + N_EXPERTS - 1
    return gsrc_ref[last] + glen_ref[last]


def _wait_rows(rows, src, dst, sem):
    for pred, cp in _run_copies(src, dst, jnp.int32(0), jnp.int32(0), rows, SLOT_BITS, sem):
        @pl.when(pred)
        def _(cp=cp):
            cp.wait()


def _slots(info, T):
    q1 = info[:, 0:1].astype(jnp.int32)
    q2 = info[:, 1:2].astype(jnp.int32)
    r = lax.broadcasted_iota(jnp.int32, (T, SLOT_ROWS), 1)
    return q1 == r, q2 == r


def _dispatch_kernel(gdst_ref, glen_ref, gsrc_ref, tail0_ref, tailn_ref,
                     h_ref, info_ref, xs_hbm, xbuf, zbuf, sem, *, slack_pieces):
    t = pl.program_id(0)
    n_t = pl.num_programs(0)
    T = h_ref.shape[0]
    slot = t % 2
    tables = (gdst_ref, glen_ref, gsrc_ref)
    hit1, hit2 = _slots(info_ref[...], T)
    onehot = (hit1 | hit2).astype(BF16)
    xbuf[slot] = lax.dot_general(onehot, h_ref[...], (((0,), (0,)), ((), ())),
                                 preferred_element_type=F32)
    _start(_tile_runs(t, tables, xs_hbm, xbuf.at[slot], sem.at[slot], to_hbm=True))

    @pl.when(t > 0)
    def _():
        _wait_rows(_tile_rows(t - 1, tables), xbuf.at[1 - slot], xs_hbm, sem.at[1 - slot])

    @pl.when(t == n_t - 1)
    def _():
        _wait_rows(_tile_rows(t, tables), xbuf.at[slot], xs_hbm, sem.at[slot])

    @pl.when(t == 0)
    def _():
        zbuf[...] = jnp.zeros_like(zbuf)
        tails = []
        for e in range(N_EXPERTS):
            tails += _run_copies(zbuf, xs_hbm, None, tail0_ref[e], tailn_ref[e], TAIL_BITS,
                                 sem.at[2])
        piece = zbuf.shape[0]
        for c in range(slack_pieces):
            d0 = pl.multiple_of(tail0_ref[N_EXPERTS] + c * piece, RUN_ALIGN)
            tails.append((c < tailn_ref[N_EXPERTS], pltpu.make_async_copy(
                zbuf, xs_hbm.at[pl.ds(d0, piece)], sem.at[2])))
        _start(tails)
        _wait(tails)


def _dispatch(h, info, gdst, glen, gsrc, tail0, tailn, rows_max):
    N, D = h.shape
    T = ROUTE_T
    slack_pieces = (rows_max - 2 * N) // (EXPERT_TM // 2)
    return pl.pallas_call(
        functools.partial(_dispatch_kernel, slack_pieces=slack_pieces),
        out_shape=jax.ShapeDtypeStruct((rows_max, D), F32),
        grid_spec=pltpu.PrefetchScalarGridSpec(
            num_scalar_prefetch=5,
            grid=(N // T,),
            in_specs=[pl.BlockSpec((T, D), lambda t, *_: (t, 0)),
                      pl.BlockSpec((T, E_LANES), lambda t, *_: (t, 0))],
            out_specs=pl.BlockSpec(memory_space=pl.ANY),
            scratch_shapes=[pltpu.VMEM((2, SLOT_ROWS, D), F32),
                            pltpu.VMEM((EXPERT_TM // 2, D), F32),
                            pltpu.SemaphoreType.DMA((3,))]),
        compiler_params=pltpu.CompilerParams(
            dimension_semantics=("arbitrary",),
            vmem_limit_bytes=V7X_VMEM_LIMIT_BYTES),
        name="moe_dispatch",
    )(gdst, glen, gsrc, tail0, tailn, h, info)


def _experts_kernel(te_ref, nv_ref, ns_ref, x_ref, wg_ref, wu_ref, wd_ref, o_ref,
                    xb, wgb, wub, wdb):
    i = pl.program_id(0)
    k = pl.program_id(1)
    n_sub = ns_ref[i]

    full = x_ref.shape[0] // EXPERT_SUB

    @pl.when((n_sub > 0) & (k == 0))
    def _():
        xb[...] = x_ref[...].astype(BF16)
        o_ref[...] = jnp.zeros_like(o_ref)

    @pl.when(n_sub == full)
    def _():
        h = xb[...]
        act = (_silu(_dot(h, wg_ref[0].astype(BF16))) * _dot(h, wu_ref[0].astype(BF16))).astype(BF16)
        o_ref[...] += _dot(act, wd_ref[0].astype(BF16))

    @pl.when((n_sub > 0) & (n_sub < full))
    def _():
        wgb[...] = wg_ref[0].astype(BF16)
        wub[...] = wu_ref[0].astype(BF16)
        wdb[...] = wd_ref[0].astype(BF16)

        def piece(s, carry):
            rows = pl.ds(pl.multiple_of(s * EXPERT_SUB, EXPERT_SUB), EXPERT_SUB)
            h = xb[rows, :]
            act = (_silu(_dot(h, wgb[...])) * _dot(h, wub[...])).astype(BF16)
            o_ref[rows, :] += _dot(act, wdb[...])
            return carry

        lax.fori_loop(0, n_sub, piece, 0)

    @pl.when((n_sub == 0) & (k == 0))
    def _():
        o_ref[...] = jnp.zeros_like(o_ref)


def _experts(xs, tile_expert, n_valid, n_sub, wg, wu, wd):
    M, D = xs.shape
    tm, fk = EXPERT_TM, EXPERT_FK
    E, _, F = wg.shape
    nk = F // fk

    def row_map(i, k, te, nv, ns):
        return (jnp.maximum(jnp.minimum(i, nv[0] - 1), 0), 0)

    def kk(i, k, nv):
        return jnp.where(i < nv[0], k, nk - 1)

    return pl.pallas_call(
        _experts_kernel,
        out_shape=jax.ShapeDtypeStruct((M, D), F32),
        grid_spec=pltpu.PrefetchScalarGridSpec(
            num_scalar_prefetch=3,
            grid=(M // tm, nk),
            in_specs=[
                pl.BlockSpec((tm, D), row_map),
                pl.BlockSpec((1, D, fk), lambda i, k, te, nv, ns: (te[i], 0, kk(i, k, nv))),
                pl.BlockSpec((1, D, fk), lambda i, k, te, nv, ns: (te[i], 0, kk(i, k, nv))),
                pl.BlockSpec((1, fk, D), lambda i, k, te, nv, ns: (te[i], kk(i, k, nv), 0)),
            ],
            out_specs=pl.BlockSpec((tm, D), lambda i, k, te, nv, ns: (i, 0)),
            scratch_shapes=[pltpu.VMEM((tm, D), BF16),
                            pltpu.VMEM((D, fk), BF16), pltpu.VMEM((D, fk), BF16),
                            pltpu.VMEM((fk, D), BF16)]),
        compiler_params=pltpu.CompilerParams(
            dimension_semantics=("arbitrary", "arbitrary"),
            vmem_limit_bytes=V7X_VMEM_LIMIT_BYTES),
        name="moe_experts",
    )(tile_expert, n_valid, n_sub, xs, wg, wu, wd)


def _combine_kernel(gdst_ref, glen_ref, gsrc_ref, x_ref, info_ref, gf_ref, ys_hbm, o_ref,
                    ybuf, sem):
    t = pl.program_id(0)
    n_t = pl.num_programs(0)
    T = x_ref.shape[0]
    slot = t % 2
    tables = (gdst_ref, glen_ref, gsrc_ref)
    @pl.when(t == 0)
    def _():
        _start(_tile_runs(t, tables, ys_hbm, ybuf.at[slot], sem.at[slot], to_hbm=False))

    @pl.when(t + 1 < n_t)
    def _():
        _start(_tile_runs(t + 1, tables, ys_hbm, ybuf.at[1 - slot], sem.at[1 - slot],
                          to_hbm=False))

    used = _tile_rows(t, tables)
    _wait_rows(used, ys_hbm, ybuf.at[slot], sem.at[slot])
    rows = lax.broadcasted_iota(jnp.int32, (SLOT_ROWS, 1), 0)
    y = jnp.where(rows < used, ybuf[slot], 0.0).astype(BF16)
    info = info_ref[...]
    hit1, hit2 = _slots(info, T)
    weights = (jnp.where(hit1, info[:, 2:3], 0.0) + jnp.where(hit2, info[:, 3:4], 0.0)).astype(BF16)
    o_ref[...] = _rms(x_ref[...] + _dot(weights, y), gf_ref[...])


def _combine(x, info, gf, ys, gdst, glen, gsrc):
    N, D = x.shape
    T = ROUTE_T
    return pl.pallas_call(
        _combine_kernel,
        out_shape=jax.ShapeDtypeStruct((N, D), F32),
        grid_spec=pltpu.PrefetchScalarGridSpec(
            num_scalar_prefetch=3,
            grid=(N // T,),
            in_specs=[pl.BlockSpec((T, D), lambda t, *_: (t, 0)),
                      pl.BlockSpec((T, E_LANES), lambda t, *_: (t, 0)),
                      pl.BlockSpec(gf.shape, lambda t, *_: (0, 0)),
                      pl.BlockSpec(memory_space=pl.ANY)],
            out_specs=pl.BlockSpec((T, D), lambda t, *_: (t, 0)),
            scratch_shapes=[pltpu.VMEM((2, SLOT_ROWS, D), F32),
                            pltpu.SemaphoreType.DMA((2,))]),
        compiler_params=pltpu.CompilerParams(
            dimension_semantics=("arbitrary",),
            vmem_limit_bytes=V7X_VMEM_LIMIT_BYTES),
        name="moe_combine",
    )(gdst, glen, gsrc, x, info, gf, ys)


def _moe(x, g, router, wg, wu, wd, gf):
    N, D = x.shape
    n_t = N // ROUTE_T
    tm = EXPERT_TM
    router_p = jnp.pad(router, ((0, 0), (0, E_LANES - N_EXPERTS)))
    h, info, cnt = _route(x, g, router_p)
    glen = cnt[:, 0, :N_EXPERTS].astype(jnp.int32)
    total = jnp.sum(glen, axis=0)
    gpad = (total + tm - 1) // tm * tm
    gend = jnp.cumsum(gpad)
    goff = gend - gpad
    gdst = goff[None, :] + jnp.cumsum(glen, axis=0) - glen
    gsrc = jnp.cumsum(glen, axis=1) - glen
    rows_max = -(-(2 * N + n_t * N_EXPERTS * (RUN_ALIGN - 1) + N_EXPERTS * (tm - RUN_ALIGN)) // tm) * tm
    n_tiles = rows_max // tm
    n_valid = (gend[-1] // tm).reshape(1)
    tile_row = jnp.minimum(jnp.arange(n_tiles, dtype=jnp.int32), n_valid[0] - 1) * tm
    tile_expert = jnp.sum((tile_row[:, None] >= gend[None, :]).astype(jnp.int32), axis=1)
    tile_rows = jnp.clip((goff + total)[tile_expert] - tile_row, 0, tm)
    tile_rows = jnp.where(jnp.arange(n_tiles) < n_valid[0], tile_rows, 0)
    n_sub = (tile_rows + EXPERT_SUB - 1) // EXPERT_SUB
    flat = lambda a: a.reshape(-1).astype(jnp.int32)
    tail0 = jnp.concatenate([goff + total, gend[-1:]])
    tailn = jnp.concatenate([gpad - total, (rows_max - gend[-1:]) // (tm // 2)])
    xs = _dispatch(h, info, flat(gdst), flat(glen), flat(gsrc), flat(tail0), flat(tailn), rows_max)
    ys = _experts(xs, flat(tile_expert), flat(n_valid), flat(n_sub), wg, wu, wd)
    return _combine(x, info, gf, ys, flat(gdst), flat(glen), flat(gsrc))


def kernel(x, norm1_g, w_in, pool_w, pool_scale, gm_norm_g, gm_ws, gm_b,
           conv_dw_w, conv_dw_b, conv_ln_g, conv_ln_b, conv_pw_w, conv_pw_b,
           w_out, norm2_g, ffn_wg, ffn_wu, ffn_wd,
           moe_router, moe_wg, moe_wu, moe_wd, final_g):
    B, S, D = x.shape
    depth = w_in.shape[0]
    assert depth == 2, "layer 0 is the dense SwiGLU layer, layer 1 the expert layer + final norm"
    row = lambda t: t.reshape(1, -1)
    for l in range(depth):
        pool_bd = jax.scipy.linalg.block_diag(*[pool_w[l, gi] for gi in range(len(POOL_WINDOWS))])
        gm_wcat = jnp.transpose(gm_ws[l], (1, 0, 2)).reshape(GMLP_BLOCK, GMLP_HEADS * GMLP_BLOCK)
        gm_bias = jnp.repeat(gm_b[l].T, GMLP_HEAD_DIM, axis=1)
        x = _mixer(x, row(norm1_g[l]), w_in[l].astype(BF16), pool_bd.astype(BF16),
                   row(pool_scale[l]), row(gm_norm_g[l]), gm_wcat, gm_bias,
                   conv_dw_w[l], row(conv_dw_b[l]), row(conv_ln_g[l]), row(conv_ln_b[l]),
                   conv_pw_w[l].astype(BF16), row(conv_pw_b[l]), w_out[l].astype(BF16))
        xf = x.reshape(B * S, D)
        j = l // 2
        if l % 2 == 0:
            xf = _ffn(xf, row(norm2_g[l]), ffn_wg[j].astype(BF16), ffn_wu[j].astype(BF16),
                      ffn_wd[j].astype(BF16))
        else:
            xf = _moe(xf, row(norm2_g[l]), moe_router[j], moe_wg[j], moe_wu[j], moe_wd[j],
                      row(final_g))
        x = xf.reshape(B, S, D)
    return x
```

```python
import functools

import jax
import jax.numpy as jnp
from jax import lax
from jax.experimental import pallas as pl
from jax.experimental.pallas import tpu as pltpu

D_MODEL = 1024
CHUNK = 64
D_POOL = 256
POOL_WINDOWS = (2, 4, 8, 16)
POOL_GROUP = 64
D_GMLP = 384
GMLP_HEADS = 4
GMLP_HEAD_DIM = 96
GMLP_BLOCK = 128
D_CONV = 384
CONV_WIDTH = 31
D_IN = D_POOL + 2 * D_GMLP + 2 * D_CONV
D_FF = 2816
N_EXPERTS = 8
D_FF_EXPERT = 3584
EPS = 1e-6

V7X_VMEM_LIMIT_BYTES = 56 * 1024 * 1024

POOL_PAD = 8
POOL_HALO = POOL_PAD + 16
CONV_HALO = 32
MIX_TM = 1024
CONV_ROWS = 256
FFN_TM = 1024
ROUTE_T = 512
EXPERT_TM = 1024
EXPERT_FK = 512
EXPERT_SUB = 256

BF16 = jnp.bfloat16
F32 = jnp.float32


def _rms(x, g):
    return x * lax.rsqrt(jnp.mean(x * x, axis=-1, keepdims=True) + EPS) * g


def _silu(x):
    return x * jax.nn.sigmoid(x)


def _dot(a, b):
    return jnp.dot(a, b, preferred_element_type=F32)


def _mixer_kernel(x_ref, g1_ref, win_ref, poolw_ref, pools_ref, gmg_ref,
                  gmws_ref, gmb_ref, dww_ref, dwb_ref, lng_ref, lnb_ref,
                  pww_ref, pwb_ref, wout_ref, o_ref, abuf, s2buf, s4buf, s8buf, hbuf, hshift,
                  wtap, ycat):
    tm = x_ref.shape[1]
    s = pl.program_id(1)

    @pl.when(s == 0)
    def _():
        abuf[0:POOL_HALO, :] = jnp.zeros((POOL_HALO, D_POOL), F32)
        s2buf[0:POOL_PAD, :] = jnp.zeros((POOL_PAD, D_POOL), F32)
        s4buf[0:POOL_PAD, :] = jnp.zeros((POOL_PAD, D_POOL), F32)
        s8buf[0:POOL_PAD, :] = jnp.zeros((POOL_PAD, 128), F32)
        hbuf[0:CONV_HALO, :] = jnp.zeros((CONV_HALO, D_CONV), F32)
        for j in range(CONV_WIDTH):
            wtap[j] = jnp.broadcast_to(dww_ref[j:j + 1, :], (8, D_CONV))

    x = x_ref[0]
    h = _rms(x, g1_ref[...])
    p = _dot(h.astype(BF16), win_ref[...])
    o1 = D_POOL
    o2 = o1 + D_GMLP
    o3 = o2 + D_GMLP
    o4 = o3 + D_CONV
    u = p[:, o1:o2]
    v = p[:, o2:o3]
    cv = p[:, o3:o4]
    cg = p[:, o4:]

    H = POOL_HALO
    n = tm + H - POOL_PAD
    abuf[H:H + tm, :] = p[:, :o1]
    s2buf[POOL_PAD:POOL_PAD + n, :] = abuf[POOL_PAD:POOL_PAD + n, :] + abuf[POOL_PAD - 1:POOL_PAD - 1 + n, :]
    s4buf[POOL_PAD:POOL_PAD + n, :] = s2buf[POOL_PAD:POOL_PAD + n, :] + s2buf[POOL_PAD - 2:POOL_PAD - 2 + n, :]
    s8buf[POOL_PAD:POOL_PAD + n, :] = (s4buf[POOL_PAD:POOL_PAD + n, 128:]
                                       + s4buf[POOL_PAD - 4:POOL_PAD - 4 + n, 128:])
    s16 = s8buf[H:H + tm, :] + s8buf[H - 8:H - 8 + tm, :]
    lane = lax.broadcasted_iota(jnp.int32, (1, 128), 1)
    row = lax.broadcasted_iota(jnp.int32, (tm, 1), 0)
    tpos = (s * tm + row + 1).astype(F32)
    first = lane < POOL_GROUP
    ys = []
    for lo, (w_small, w_big), s_small, s_big in (
            (0, POOL_WINDOWS[0:2], s2buf[H:H + tm, 0:128], s4buf[H:H + tm, 0:128]),
            (128, POOL_WINDOWS[2:4], s8buf[H:H + tm, :], s16)):
        ssel = jnp.where(first, s_small, s_big)
        win = jnp.where(first, float(w_small), float(w_big))
        mean = ssel / jnp.minimum(tpos, win)
        ys.append(mean - abuf[H:H + tm, lo:lo + 128])
    y = jnp.concatenate(ys, axis=1).astype(BF16)
    ya = _dot(y, poolw_ref[...]) * pools_ref[...]
    ycat[:, 0:o1] = ya.astype(BF16)
    abuf[POOL_PAD:H, :] = abuf[tm + POOL_PAD:tm + H, :]

    vn = _rms(v, gmg_ref[...])
    ri = lax.broadcasted_iota(jnp.int32, (GMLP_BLOCK, GMLP_HEADS * GMLP_BLOCK), 0)
    cj = lax.broadcasted_iota(jnp.int32, (GMLP_BLOCK, GMLP_HEADS * GMLP_BLOCK), 1)
    causal = (ri // CHUNK) >= ((cj % GMLP_BLOCK) // CHUNK)
    wcat = jnp.where(causal, gmws_ref[...], 0.0).astype(BF16)
    hid = lax.broadcasted_iota(jnp.int32, (1, D_GMLP), 1) // GMLP_HEAD_DIM
    gmb = gmb_ref[...]
    for blk in range(tm // GMLP_BLOCK):
        r0 = blk * GMLP_BLOCK
        vb = vn[r0:r0 + GMLP_BLOCK, :]
        vstack = jnp.concatenate(
            [jnp.where(hid == hh, vb, 0.0) for hh in range(GMLP_HEADS)],
            axis=0).astype(BF16)
        z = _dot(wcat, vstack) + gmb
        ycat[r0:r0 + GMLP_BLOCK, o1:o2] = (u[r0:r0 + GMLP_BLOCK, :] * z).astype(BF16)

    hbuf[CONV_HALO:CONV_HALO + tm, :] = cv * jax.nn.sigmoid(cg)
    base = CONV_HALO - (CONV_WIDTH - 1)
    dwb = dwb_ref[...]
    lng = lng_ref[...]
    lnb = lnb_ref[...]
    pwb = pwb_ref[...]
    for b in range(8):
        n_rows = tm + 8 * (len(range(b, CONV_WIDTH, 8)) - 1)
        hshift[b, 0:n_rows, :] = hbuf[base + b:base + b + n_rows, :]
    for c0 in range(0, tm, CONV_ROWS):
        acc = None
        for j in range(CONV_WIDTH):
            r0 = c0 + 8 * (j // 8)
            win = hshift[j % 8, r0:r0 + CONV_ROWS, :].reshape(CONV_ROWS // 8, 8, D_CONV)
            term = wtap[j] * win
            acc = term if acc is None else acc + term
        acc = acc.reshape(CONV_ROWS, D_CONV) + dwb
        mu = jnp.mean(acc, axis=-1, keepdims=True)
        cen = acc - mu
        var = jnp.mean(cen * cen, axis=-1, keepdims=True)
        ln = cen * lax.rsqrt(var + EPS) * lng + lnb
        yc = _dot(_silu(ln).astype(BF16), pww_ref[...]) + pwb
        ycat[c0:c0 + CONV_ROWS, o2:D_MODEL] = yc.astype(BF16)
    hbuf[0:CONV_HALO, :] = hbuf[tm:tm + CONV_HALO, :]

    o_ref[0] = x + _dot(ycat[...], wout_ref[...])


def _const_spec(shape):
    nd = len(shape)
    return pl.BlockSpec(shape, lambda b, s: (0,) * nd, pipeline_mode=pl.Buffered(1))


def _mixer(x, g1, w_in, pool_bd, pool_scale, gm_g, gm_wcat, gm_bias, dw_w, dw_b,
           ln_g, ln_b, pw_w, pw_b, w_out):
    B, S, D = x.shape
    tm = MIX_TM
    weights = (g1, w_in, pool_bd, pool_scale, gm_g, gm_wcat, gm_bias, dw_w, dw_b,
               ln_g, ln_b, pw_w, pw_b, w_out)
    return pl.pallas_call(
        _mixer_kernel,
        out_shape=jax.ShapeDtypeStruct((B, S, D), F32),
        grid=(B, S // tm),
        in_specs=[pl.BlockSpec((1, tm, D), lambda b, s: (b, s, 0))]
        + [_const_spec(w.shape) for w in weights],
        out_specs=pl.BlockSpec((1, tm, D), lambda b, s: (b, s, 0)),
        scratch_shapes=[
            pltpu.VMEM((POOL_HALO + tm, D_POOL), F32),
            pltpu.VMEM((POOL_HALO + tm, D_POOL), F32),
            pltpu.VMEM((POOL_HALO + tm, D_POOL), F32),
            pltpu.VMEM((POOL_HALO + tm, 128), F32),
            pltpu.VMEM((CONV_HALO + tm, D_CONV), F32),
            pltpu.VMEM((8, tm + CONV_HALO - 8, D_CONV), F32),
            pltpu.VMEM((CONV_WIDTH, 8, D_CONV), F32),
            pltpu.VMEM((tm, D_MODEL), BF16),
        ],
        compiler_params=pltpu.CompilerParams(
            dimension_semantics=("arbitrary", "arbitrary"),
            vmem_limit_bytes=V7X_VMEM_LIMIT_BYTES),
        name="token_mixer",
    )(x, *weights)


def _ffn_kernel(x_ref, g_ref, wg_ref, wu_ref, wd_ref, o_ref):
    x = x_ref[...]
    h = _rms(x, g_ref[...]).astype(BF16)
    gate = _dot(h, wg_ref[...])
    up = _dot(h, wu_ref[...])
    act = (_silu(gate) * up).astype(BF16)
    o_ref[...] = x + _dot(act, wd_ref[...])


def _ffn(x, g, wg, wu, wd):
    N, D = x.shape
    tm = FFN_TM
    const = lambda shape: pl.BlockSpec(shape, lambda i: (0, 0), pipeline_mode=pl.Buffered(1))
    return pl.pallas_call(
        _ffn_kernel,
        out_shape=jax.ShapeDtypeStruct((N, D), F32),
        grid=(N // tm,),
        in_specs=[pl.BlockSpec((tm, D), lambda i: (i, 0)), const(g.shape),
                  const(wg.shape), const(wu.shape), const(wd.shape)],
        out_specs=pl.BlockSpec((tm, D), lambda i: (i, 0)),
        compiler_params=pltpu.CompilerParams(
            dimension_semantics=("arbitrary",),
            vmem_limit_bytes=V7X_VMEM_LIMIT_BYTES),
        name="dense_swiglu",
    )(x, g, wg, wu, wd)


E_LANES = 128
RUN_ALIGN = 8
SLOT_ROWS = 2 * ROUTE_T + RUN_ALIGN * N_EXPERTS
RUN_BITS = tuple(range((ROUTE_T // RUN_ALIGN).bit_length() - 1, -1, -1))
TAIL_BITS = tuple(range((EXPERT_TM // RUN_ALIGN).bit_length() - 2, -1, -1))
SLOT_BITS = tuple(range((SLOT_ROWS // RUN_ALIGN).bit_length() - 1, -1, -1))


def _route_kernel(x_ref, g_ref, router_ref, h_ref, info_ref, cnt_ref):
    T = x_ref.shape[0]
    h = _rms(x_ref[...], g_ref[...])
    h_hi = h.astype(BF16)
    h_ref[...] = h_hi
    h_lo = (h - h_hi.astype(F32)).astype(BF16)
    r = router_ref[...]
    r_hi = r.astype(BF16)
    r_lo = (r - r_hi.astype(F32)).astype(BF16)
    logits = _dot(h_hi, r_hi) + (_dot(h_lo, r_hi) + _dot(h_hi, r_lo))
    lane = lax.broadcasted_iota(jnp.int32, (T, E_LANES), 1)
    logits = jnp.where(lane < N_EXPERTS, logits, -jnp.inf)
    m1 = jnp.max(logits, axis=-1, keepdims=True)
    i1 = jnp.min(jnp.where(logits == m1, lane, E_LANES), axis=-1, keepdims=True)
    rest = jnp.where(lane == i1, -jnp.inf, logits)
    m2 = jnp.max(rest, axis=-1, keepdims=True)
    i2 = jnp.min(jnp.where(rest == m2, lane, E_LANES), axis=-1, keepdims=True)
    e2 = jnp.exp(m2 - m1)
    w1 = 1.0 / (1.0 + e2)
    w2 = e2 / (1.0 + e2)
    oh1 = (lane == i1).astype(F32)
    oh2 = (lane == i2).astype(F32)
    oh = oh1 + oh2
    ri = lax.broadcasted_iota(jnp.int32, (T, T), 0)
    ci = lax.broadcasted_iota(jnp.int32, (T, T), 1)
    rank = _dot((ri > ci).astype(BF16), oh.astype(BF16))
    cnt = jnp.sum(oh, axis=0, keepdims=True)
    padc = jnp.floor((cnt + (RUN_ALIGN - 1.0)) * (1.0 / RUN_ALIGN)) * RUN_ALIGN
    lane1 = lax.broadcasted_iota(jnp.int32, (1, E_LANES), 1)
    start = jnp.zeros((1, E_LANES), F32)
    for e in range(N_EXPERTS - 1):
        start = start + jnp.where(lane1 > e, padc[:, e:e + 1], 0.0)
    slot = start + rank
    q1 = jnp.sum(oh1 * slot, axis=-1, keepdims=True)
    q2 = jnp.sum(oh2 * slot, axis=-1, keepdims=True)
    info_ref[...] = jnp.where(lane == 0, q1, jnp.where(lane == 1, q2, jnp.where(
        lane == 2, w1, jnp.where(lane == 3, w2, 0.0))))
    cnt_ref[0] = padc


def _route(x, g, router_p):
    N, D = x.shape
    T = ROUTE_T
    n_t = N // T
    return pl.pallas_call(
        _route_kernel,
        out_shape=(jax.ShapeDtypeStruct((N, D), BF16),
                   jax.ShapeDtypeStruct((N, E_LANES), F32),
                   jax.ShapeDtypeStruct((n_t, 1, E_LANES), F32)),
        grid=(n_t,),
        in_specs=[pl.BlockSpec((T, D), lambda t: (t, 0)),
                  pl.BlockSpec(g.shape, lambda t: (0, 0)),
                  pl.BlockSpec(router_p.shape, lambda t: (0, 0))],
        out_specs=(pl.BlockSpec((T, D), lambda t: (t, 0)),
                   pl.BlockSpec((T, E_LANES), lambda t: (t, 0)),
                   pl.BlockSpec((1, 1, E_LANES), lambda t: (t, 0, 0))),
        compiler_params=pltpu.CompilerParams(
            dimension_semantics=("arbitrary",),
            vmem_limit_bytes=V7X_VMEM_LIMIT_BYTES),
        name="moe_route",
    )(x, g, router_p)


def _run_copies(src, dst, src0, dst0, rows, bits, sem):
    m = rows // RUN_ALIGN
    off = 0
    out = []
    for b in bits:
        size = RUN_ALIGN << b
        take = (m >> b) & 1
        s0 = 0 if src0 is None else pl.multiple_of(src0 + off, RUN_ALIGN)
        d0 = pl.multiple_of(dst0 + off, RUN_ALIGN)
        out.append((take == 1, pltpu.make_async_copy(
            src.at[pl.ds(s0, size)], dst.at[pl.ds(d0, size)], sem)))
        off = off + take * size
    return out


def _start(copies, enable=True):
    for pred, cp in copies:
        @pl.when(pred & enable)
        def _(cp=cp):
            cp.start()


def _wait(copies, enable=True):
    for pred, cp in copies:
        @pl.when(pred & enable)
        def _(cp=cp):
            cp.wait()


def _tile_runs(tile, table_refs, hbm, buf, sem, *, to_hbm):
    gdst_ref, glen_ref, gsrc_ref = table_refs
    copies = []
    for e in range(N_EXPERTS):
        j = tile * N_EXPERTS + e
        if to_hbm:
            copies += _run_copies(buf, hbm, gsrc_ref[j], gdst_ref[j], glen_ref[j], RUN_BITS, sem)
        else:
            copies += _run_copies(hbm, buf, gdst_ref[j], gsrc_ref[j], glen_ref[j], RUN_BITS, sem)
    return copies


def _tile_rows(tile, table_refs):
    _, glen_ref, gsrc_ref = table_refs
    last = tile * N_EXPERTS + N_EXPERTS - 1
    return gsrc_ref[last] + glen_ref[last]


def _wait_rows(rows, src, dst, sem):
    for pred, cp in _run_copies(src, dst, jnp.int32(0), jnp.int32(0), rows, SLOT_BITS, sem):
        @pl.when(pred)
        def _(cp=cp):
            cp.wait()


def _slots(info, T):
    q1 = info[:, 0:1].astype(jnp.int32)
    q2 = info[:, 1:2].astype(jnp.int32)
    r = lax.broadcasted_iota(jnp.int32, (T, SLOT_ROWS), 1)
    return q1 == r, q2 == r


def _dispatch_kernel(gdst_ref, glen_ref, gsrc_ref, tail0_ref, tailn_ref,
                     h_ref, info_ref, xs_hbm, xbuf, zbuf, sem, *, slack_pieces):
    t = pl.program_id(0)
    n_t = pl.num_programs(0)
    T = h_ref.shape[0]
    slot = t % 2
    tables = (gdst_ref, glen_ref, gsrc_ref)
    hit1, hit2 = _slots(info_ref[...], T)
    onehot = (hit1 | hit2).astype(BF16)
    xbuf[slot] = lax.dot_general(onehot, h_ref[...], (((0,), (0,)), ((), ())),
                                 preferred_element_type=F32)
    _start(_tile_runs(t, tables, xs_hbm, xbuf.at[slot], sem.at[slot], to_hbm=True))

    @pl.when(t > 0)
    def _():
        _wait_rows(_tile_rows(t - 1, tables), xbuf.at[1 - slot], xs_hbm, sem.at[1 - slot])

    @pl.when(t == n_t - 1)
    def _():
        _wait_rows(_tile_rows(t, tables), xbuf.at[slot], xs_hbm, sem.at[slot])

    @pl.when(t == 0)
    def _():
        zbuf[...] = jnp.zeros_like(zbuf)
        tails = []
        for e in range(N_EXPERTS):
            tails += _run_copies(zbuf, xs_hbm, None, tail0_ref[e], tailn_ref[e], TAIL_BITS,
                                 sem.at[2])
        piece = zbuf.shape[0]
        for c in range(slack_pieces):
            d0 = pl.multiple_of(tail0_ref[N_EXPERTS] + c * piece, RUN_ALIGN)
            tails.append((c < tailn_ref[N_EXPERTS], pltpu.make_async_copy(
                zbuf, xs_hbm.at[pl.ds(d0, piece)], sem.at[2])))
        _start(tails)
        _wait(tails)


def _dispatch(h, info, gdst, glen, gsrc, tail0, tailn, rows_max):
    N, D = h.shape
    T = ROUTE_T
    slack_pieces = (rows_max - 2 * N) // (EXPERT_TM // 2)
    return pl.pallas_call(
        functools.partial(_dispatch_kernel, slack_pieces=slack_pieces),
        out_shape=jax.ShapeDtypeStruct((rows_max, D), F32),
        grid_spec=pltpu.PrefetchScalarGridSpec(
            num_scalar_prefetch=5,
            grid=(N // T,),
            in_specs=[pl.BlockSpec((T, D), lambda t, *_: (t, 0)),
                      pl.BlockSpec((T, E_LANES), lambda t, *_: (t, 0))],
            out_specs=pl.BlockSpec(memory_space=pl.ANY),
            scratch_shapes=[pltpu.VMEM((2, SLOT_ROWS, D), F32),
                            pltpu.VMEM((EXPERT_TM // 2, D), F32),
                            pltpu.SemaphoreType.DMA((3,))]),
        compiler_params=pltpu.CompilerParams(
            dimension_semantics=("arbitrary",),
            vmem_limit_bytes=V7X_VMEM_LIMIT_BYTES),
        name="moe_dispatch",
    )(gdst, glen, gsrc, tail0, tailn, h, info)


def _experts_kernel(te_ref, nv_ref, ns_ref, x_ref, wg_ref, wu_ref, wd_ref, o_ref,
                    xb, wgb, wub, wdb, dbuf):
    i = pl.program_id(0)
    k = pl.program_id(1)
    n_sub = ns_ref[i]

    full = x_ref.shape[0] // EXPERT_SUB

    @pl.when((n_sub > 0) & (k == 0))
    def _():
        xb[...] = x_ref[...].astype(BF16)
        o_ref[...] = jnp.zeros_like(o_ref)
        dbuf[...] = jnp.zeros_like(dbuf)

    @pl.when(n_sub == full)
    def _():
        o_ref[...] += dbuf[...]
        h = xb[...]
        act = (_silu(_dot(h, wg_ref[0].astype(BF16))) * _dot(h, wu_ref[0].astype(BF16))).astype(BF16)
        dbuf[...] = _dot(act, wd_ref[0].astype(BF16))

    @pl.when((n_sub == full) & (k == pl.num_programs(1) - 1))
    def _():
        o_ref[...] += dbuf[...]

    @pl.when((n_sub > 0) & (n_sub < full))
    def _():
        wgb[...] = wg_ref[0].astype(BF16)
        wub[...] = wu_ref[0].astype(BF16)
        wdb[...] = wd_ref[0].astype(BF16)

        def piece(s, carry):
            rows = pl.ds(pl.multiple_of(s * EXPERT_SUB, EXPERT_SUB), EXPERT_SUB)
            h = xb[rows, :]
            act = (_silu(_dot(h, wgb[...])) * _dot(h, wub[...])).astype(BF16)
            o_ref[rows, :] += _dot(act, wdb[...])
            return carry

        lax.fori_loop(0, n_sub, piece, 0)

    @pl.when((n_sub == 0) & (k == 0))
    def _():
        o_ref[...] = jnp.zeros_like(o_ref)


def _experts(xs, tile_expert, n_valid, n_sub, wg, wu, wd):
    M, D = xs.shape
    tm, fk = EXPERT_TM, EXPERT_FK
    E, _, F = wg.shape
    nk = F // fk

    def row_map(i, k, te, nv, ns):
        return (jnp.maximum(jnp.minimum(i, nv[0] - 1), 0), 0)

    def kk(i, k, nv):
        return jnp.where(i < nv[0], k, nk - 1)

    return pl.pallas_call(
        _experts_kernel,
        out_shape=jax.ShapeDtypeStruct((M, D), F32),
        grid_spec=pltpu.PrefetchScalarGridSpec(
            num_scalar_prefetch=3,
            grid=(M // tm, nk),
            in_specs=[
                pl.BlockSpec((tm, D), row_map),
                pl.BlockSpec((1, D, fk), lambda i, k, te, nv, ns: (te[i], 0, kk(i, k, nv))),
                pl.BlockSpec((1, D, fk), lambda i, k, te, nv, ns: (te[i], 0, kk(i, k, nv))),
                pl.BlockSpec((1, fk, D), lambda i, k, te, nv, ns: (te[i], kk(i, k, nv), 0)),
            ],
            out_specs=pl.BlockSpec((tm, D), lambda i, k, te, nv, ns: (i, 0)),
            scratch_shapes=[pltpu.VMEM((tm, D), BF16),
                            pltpu.VMEM((D, fk), BF16), pltpu.VMEM((D, fk), BF16),
                            pltpu.VMEM((fk, D), BF16),
                            pltpu.VMEM((tm, D), F32)]),
        compiler_params=pltpu.CompilerParams(
            dimension_semantics=("arbitrary", "arbitrary"),
            vmem_limit_bytes=V7X_VMEM_LIMIT_BYTES),
        name="moe_experts",
    )(tile_expert, n_valid, n_sub, xs, wg, wu, wd)


def _combine_kernel(gdst_ref, glen_ref, gsrc_ref, x_ref, info_ref, gf_ref, ys_hbm, o_ref,
                    ybuf, sem):
    t = pl.program_id(0)
    n_t = pl.num_programs(0)
    T = x_ref.shape[0]
    slot = t % 2
    tables = (gdst_ref, glen_ref, gsrc_ref)
    def fetch(tile, into):
        ybuf[into, 2 * T:SLOT_ROWS, :] = jnp.zeros((SLOT_ROWS - 2 * T, ybuf.shape[2]), F32)
        _start(_tile_runs(tile, tables, ys_hbm, ybuf.at[into], sem.at[into], to_hbm=False))

    @pl.when(t == 0)
    def _():
        fetch(t, slot)

    @pl.when(t + 1 < n_t)
    def _():
        fetch(t + 1, 1 - slot)

    _wait_rows(_tile_rows(t, tables), ys_hbm, ybuf.at[slot], sem.at[slot])
    y = ybuf[slot].astype(BF16)
    info = info_ref[...]
    hit1, hit2 = _slots(info, T)
    weights = (jnp.where(hit1, info[:, 2:3], 0.0) + jnp.where(hit2, info[:, 3:4], 0.0)).astype(BF16)
    o_ref[...] = _rms(x_ref[...] + _dot(weights, y), gf_ref[...])


def _combine(x, info, gf, ys, gdst, glen, gsrc):
    N, D = x.shape
    T = ROUTE_T
    return pl.pallas_call(
        _combine_kernel,
        out_shape=jax.ShapeDtypeStruct((N, D), F32),
        grid_spec=pltpu.PrefetchScalarGridSpec(
            num_scalar_prefetch=3,
            grid=(N // T,),
            in_specs=[pl.BlockSpec((T, D), lambda t, *_: (t, 0)),
                      pl.BlockSpec((T, E_LANES), lambda t, *_: (t, 0)),
                      pl.BlockSpec(gf.shape, lambda t, *_: (0, 0)),
                      pl.BlockSpec(memory_space=pl.ANY)],
            out_specs=pl.BlockSpec((T, D), lambda t, *_: (t, 0)),
            scratch_shapes=[pltpu.VMEM((2, SLOT_ROWS, D), F32),
                            pltpu.SemaphoreType.DMA((2,))]),
        compiler_params=pltpu.CompilerParams(
            dimension_semantics=("arbitrary",),
            vmem_limit_bytes=V7X_VMEM_LIMIT_BYTES),
        name="moe_combine",
    )(gdst, glen, gsrc, x, info, gf, ys)


def _moe(x, g, router, wg, wu, wd, gf):
    N, D = x.shape
    n_t = N // ROUTE_T
    tm = EXPERT_TM
    router_p = jnp.pad(router, ((0, 0), (0, E_LANES - N_EXPERTS)))
    h, info, cnt = _route(x, g, router_p)
    glen = cnt[:, 0, :N_EXPERTS].astype(jnp.int32)
    total = jnp.sum(glen, axis=0)
    gpad = (total + tm - 1) // tm * tm
    gend = jnp.cumsum(gpad)
    goff = gend - gpad
    gdst = goff[None, :] + jnp.cumsum(glen, axis=0) - glen
    gsrc = jnp.cumsum(glen, axis=1) - glen
    rows_max = -(-(2 * N + n_t * N_EXPERTS * (RUN_ALIGN - 1) + N_EXPERTS * (tm - RUN_ALIGN)) // tm) * tm
    n_tiles = rows_max // tm
    n_valid = (gend[-1] // tm).reshape(1)
    tile_row = jnp.minimum(jnp.arange(n_tiles, dtype=jnp.int32), n_valid[0] - 1) * tm
    tile_expert = jnp.sum((tile_row[:, None] >= gend[None, :]).astype(jnp.int32), axis=1)
    tile_rows = jnp.clip((goff + total)[tile_expert] - tile_row, 0, tm)
    tile_rows = jnp.where(jnp.arange(n_tiles) < n_valid[0], tile_rows, 0)
    n_sub = (tile_rows + EXPERT_SUB - 1) // EXPERT_SUB
    flat = lambda a: a.reshape(-1).astype(jnp.int32)
    tail0 = jnp.concatenate([goff + total, gend[-1:]])
    tailn = jnp.concatenate([gpad - total, (rows_max - gend[-1:]) // (tm // 2)])
    xs = _dispatch(h, info, flat(gdst), flat(glen), flat(gsrc), flat(tail0), flat(tailn), rows_max)
    ys = _experts(xs, flat(tile_expert), flat(n_valid), flat(n_sub), wg, wu, wd)
    return _combine(x, info, gf, ys, flat(gdst), flat(glen), flat(gsrc))


def kernel(x, norm1_g, w_in, pool_w, pool_scale, gm_norm_g, gm_ws, gm_b,
           conv_dw_w, conv_dw_b, conv_ln_g, conv_ln_b, conv_pw_w, conv_pw_b,
           w_out, norm2_g, ffn_wg, ffn_wu, ffn_wd,
           moe_router, moe_wg, moe_wu, moe_wd, final_g):
    B, S, D = x.shape
    depth = w_in.shape[0]
    assert depth == 2, "layer 0 is the dense SwiGLU layer, layer 1 the expert layer + final norm"
    row = lambda t: t.reshape(1, -1)
    for l in range(depth):
        pool_bd = jax.scipy.linalg.block_diag(*[pool_w[l, gi] for gi in range(len(POOL_WINDOWS))])
        gm_wcat = jnp.transpose(gm_ws[l], (1, 0, 2)).reshape(GMLP_BLOCK, GMLP_HEADS * GMLP_BLOCK)
        gm_bias = jnp.repeat(gm_b[l].T, GMLP_HEAD_DIM, axis=1)
        x = _mixer(x, row(norm1_g[l]), w_in[l].astype(BF16), pool_bd.astype(BF16),
                   row(pool_scale[l]), row(gm_norm_g[l]), gm_wcat, gm_bias,
                   conv_dw_w[l], row(conv_dw_b[l]), row(conv_ln_g[l]), row(conv_ln_b[l]),
                   conv_pw_w[l].astype(BF16), row(conv_pw_b[l]), w_out[l].astype(BF16))
        xf = x.reshape(B * S, D)
        j = l // 2
        if l % 2 == 0:
            xf = _ffn(xf, row(norm2_g[l]), ffn_wg[j].astype(BF16), ffn_wu[j].astype(BF16),
                      ffn_wd[j].astype(BF16))
        else:
            xf = _moe(xf, row(norm2_g[l]), moe_router[j], moe_wg[j], moe_wu[j], moe_wd[j],
                      row(final_g))
        x = xf.reshape(B, S, D)
    return x
```

```python
import functools

import jax
import jax.numpy as jnp
from jax import lax
from jax.experimental import pallas as pl
from jax.experimental.pallas import tpu as pltpu

D_MODEL = 1024
CHUNK = 64
D_POOL = 256
POOL_WINDOWS = (2, 4, 8, 16)
POOL_GROUP = 64
D_GMLP = 384
GMLP_HEADS = 4
GMLP_HEAD_DIM = 96
GMLP_BLOCK = 128
D_CONV = 384
CONV_WIDTH = 31
D_IN = D_POOL + 2 * D_GMLP + 2 * D_CONV
D_FF = 2816
N_EXPERTS = 8
D_FF_EXPERT = 3584
EPS = 1e-6

V7X_VMEM_LIMIT_BYTES = 56 * 1024 * 1024

POOL_PAD = 8
POOL_HALO = POOL_PAD + 16
CONV_HALO = 32
MIX_TM = 1024
CONV_ROWS = 256
FFN_TM = 1024
ROUTE_T = 512
EXPERT_TM = 1024
EXPERT_FK = 896
EXPERT_SUB = 256

BF16 = jnp.bfloat16
F32 = jnp.float32


def _rms(x, g):
    return x * lax.rsqrt(jnp.mean(x * x, axis=-1, keepdims=True) + EPS) * g


def _silu(x):
    return x * jax.nn.sigmoid(x)


def _dot(a, b):
    return jnp.dot(a, b, preferred_element_type=F32)


def _mixer_kernel(x_ref, g1_ref, win_ref, poolw_ref, pools_ref, gmg_ref,
                  gmws_ref, gmb_ref, dww_ref, dwb_ref, lng_ref, lnb_ref,
                  pww_ref, pwb_ref, wout_ref, o_ref, abuf, s2buf, s4buf, s8buf, hbuf, hshift,
                  wtap, ycat):
    tm = x_ref.shape[1]
    s = pl.program_id(1)

    @pl.when(s == 0)
    def _():
        abuf[0:POOL_HALO, :] = jnp.zeros((POOL_HALO, D_POOL), F32)
        s2buf[0:POOL_PAD, :] = jnp.zeros((POOL_PAD, D_POOL), F32)
        s4buf[0:POOL_PAD, :] = jnp.zeros((POOL_PAD, D_POOL), F32)
        s8buf[0:POOL_PAD, :] = jnp.zeros((POOL_PAD, 128), F32)
        hbuf[0:CONV_HALO, :] = jnp.zeros((CONV_HALO, D_CONV), F32)
        for j in range(CONV_WIDTH):
            wtap[j] = jnp.broadcast_to(dww_ref[j:j + 1, :], (8, D_CONV))

    x = x_ref[0]
    h = _rms(x, g1_ref[...])
    p = _dot(h.astype(BF16), win_ref[...])
    o1 = D_POOL
    o2 = o1 + D_GMLP
    o3 = o2 + D_GMLP
    o4 = o3 + D_CONV
    u = p[:, o1:o2]
    v = p[:, o2:o3]
    cv = p[:, o3:o4]
    cg = p[:, o4:]

    H = POOL_HALO
    n = tm + H - POOL_PAD
    abuf[H:H + tm, :] = p[:, :o1]
    s2buf[POOL_PAD:POOL_PAD + n, :] = abuf[POOL_PAD:POOL_PAD + n, :] + abuf[POOL_PAD - 1:POOL_PAD - 1 + n, :]
    s4buf[POOL_PAD:POOL_PAD + n, :] = s2buf[POOL_PAD:POOL_PAD + n, :] + s2buf[POOL_PAD - 2:POOL_PAD - 2 + n, :]
    s8buf[POOL_PAD:POOL_PAD + n, :] = (s4buf[POOL_PAD:POOL_PAD + n, 128:]
                                       + s4buf[POOL_PAD - 4:POOL_PAD - 4 + n, 128:])
    s16 = s8buf[H:H + tm, :] + s8buf[H - 8:H - 8 + tm, :]
    lane = lax.broadcasted_iota(jnp.int32, (1, 128), 1)
    row = lax.broadcasted_iota(jnp.int32, (tm, 1), 0)
    tpos = (s * tm + row + 1).astype(F32)
    first = lane < POOL_GROUP
    ys = []
    for lo, (w_small, w_big), s_small, s_big in (
            (0, POOL_WINDOWS[0:2], s2buf[H:H + tm, 0:128], s4buf[H:H + tm, 0:128]),
            (128, POOL_WINDOWS[2:4], s8buf[H:H + tm, :], s16)):
        ssel = jnp.where(first, s_small, s_big)
        win = jnp.where(first, float(w_small), float(w_big))
        mean = ssel / jnp.minimum(tpos, win)
        ys.append(mean - abuf[H:H + tm, lo:lo + 128])
    y = jnp.concatenate(ys, axis=1).astype(BF16)
    ya = _dot(y, poolw_ref[...]) * pools_ref[...]
    ycat[:, 0:o1] = ya.astype(BF16)
    abuf[POOL_PAD:H, :] = abuf[tm + POOL_PAD:tm + H, :]

    vn = _rms(v, gmg_ref[...])
    ri = lax.broadcasted_iota(jnp.int32, (GMLP_BLOCK, GMLP_HEADS * GMLP_BLOCK), 0)
    cj = lax.broadcasted_iota(jnp.int32, (GMLP_BLOCK, GMLP_HEADS * GMLP_BLOCK), 1)
    causal = (ri // CHUNK) >= ((cj % GMLP_BLOCK) // CHUNK)
    wcat = jnp.where(causal, gmws_ref[...], 0.0).astype(BF16)
    hid = lax.broadcasted_iota(jnp.int32, (1, D_GMLP), 1) // GMLP_HEAD_DIM
    gmb = gmb_ref[...]
    for blk in range(tm // GMLP_BLOCK):
        r0 = blk * GMLP_BLOCK
        vb = vn[r0:r0 + GMLP_BLOCK, :]
        vstack = jnp.concatenate(
            [jnp.where(hid == hh, vb, 0.0) for hh in range(GMLP_HEADS)],
            axis=0).astype(BF16)
        z = _dot(wcat, vstack) + gmb
        ycat[r0:r0 + GMLP_BLOCK, o1:o2] = (u[r0:r0 + GMLP_BLOCK, :] * z).astype(BF16)

    hbuf[CONV_HALO:CONV_HALO + tm, :] = cv * jax.nn.sigmoid(cg)
    base = CONV_HALO - (CONV_WIDTH - 1)
    dwb = dwb_ref[...]
    lng = lng_ref[...]
    lnb = lnb_ref[...]
    pwb = pwb_ref[...]
    for b in range(8):
        n_rows = tm + 8 * (len(range(b, CONV_WIDTH, 8)) - 1)
        hshift[b, 0:n_rows, :] = hbuf[base + b:base + b + n_rows, :]
    for c0 in range(0, tm, CONV_ROWS):
        acc = None
        for j in range(CONV_WIDTH):
            r0 = c0 + 8 * (j // 8)
            win = hshift[j % 8, r0:r0 + CONV_ROWS, :].reshape(CONV_ROWS // 8, 8, D_CONV)
            term = wtap[j] * win
            acc = term if acc is None else acc + term
        acc = acc.reshape(CONV_ROWS, D_CONV) + dwb
        mu = jnp.mean(acc, axis=-1, keepdims=True)
        cen = acc - mu
        var = jnp.mean(cen * cen, axis=-1, keepdims=True)
        ln = cen * lax.rsqrt(var + EPS) * lng + lnb
        yc = _dot(_silu(ln).astype(BF16), pww_ref[...]) + pwb
        ycat[c0:c0 + CONV_ROWS, o2:D_MODEL] = yc.astype(BF16)
    hbuf[0:CONV_HALO, :] = hbuf[tm:tm + CONV_HALO, :]

    o_ref[0] = x + _dot(ycat[...], wout_ref[...])


def _const_spec(shape):
    nd = len(shape)
    return pl.BlockSpec(shape, lambda b, s: (0,) * nd, pipeline_mode=pl.Buffered(1))


def _mixer(x, g1, w_in, pool_bd, pool_scale, gm_g, gm_wcat, gm_bias, dw_w, dw_b,
           ln_g, ln_b, pw_w, pw_b, w_out):
    B, S, D = x.shape
    tm = MIX_TM
    weights = (g1, w_in, pool_bd, pool_scale, gm_g, gm_wcat, gm_bias, dw_w, dw_b,
               ln_g, ln_b, pw_w, pw_b, w_out)
    return pl.pallas_call(
        _mixer_kernel,
        out_shape=jax.ShapeDtypeStruct((B, S, D), F32),
        grid=(B, S // tm),
        in_specs=[pl.BlockSpec((1, tm, D), lambda b, s: (b, s, 0))]
        + [_const_spec(w.shape) for w in weights],
        out_specs=pl.BlockSpec((1, tm, D), lambda b, s: (b, s, 0)),
        scratch_shapes=[
            pltpu.VMEM((POOL_HALO + tm, D_POOL), F32),
            pltpu.VMEM((POOL_HALO + tm, D_POOL), F32),
            pltpu.VMEM((POOL_HALO + tm, D_POOL), F32),
            pltpu.VMEM((POOL_HALO + tm, 128), F32),
            pltpu.VMEM((CONV_HALO + tm, D_CONV), F32),
            pltpu.VMEM((8, tm + CONV_HALO - 8, D_CONV), F32),
            pltpu.VMEM((CONV_WIDTH, 8, D_CONV), F32),
            pltpu.VMEM((tm, D_MODEL), BF16),
        ],
        compiler_params=pltpu.CompilerParams(
            dimension_semantics=("arbitrary", "arbitrary"),
            vmem_limit_bytes=V7X_VMEM_LIMIT_BYTES),
        name="token_mixer",
    )(x, *weights)


def _ffn_kernel(x_ref, g_ref, wg_ref, wu_ref, wd_ref, o_ref):
    x = x_ref[...]
    h = _rms(x, g_ref[...]).astype(BF16)
    gate = _dot(h, wg_ref[...])
    up = _dot(h, wu_ref[...])
    act = (_silu(gate) * up).astype(BF16)
    o_ref[...] = x + _dot(act, wd_ref[...])


def _ffn(x, g, wg, wu, wd):
    N, D = x.shape
    tm = FFN_TM
    const = lambda shape: pl.BlockSpec(shape, lambda i: (0, 0), pipeline_mode=pl.Buffered(1))
    return pl.pallas_call(
        _ffn_kernel,
        out_shape=jax.ShapeDtypeStruct((N, D), F32),
        grid=(N // tm,),
        in_specs=[pl.BlockSpec((tm, D), lambda i: (i, 0)), const(g.shape),
                  const(wg.shape), const(wu.shape), const(wd.shape)],
        out_specs=pl.BlockSpec((tm, D), lambda i: (i, 0)),
        compiler_params=pltpu.CompilerParams(
            dimension_semantics=("arbitrary",),
            vmem_limit_bytes=V7X_VMEM_LIMIT_BYTES),
        name="dense_swiglu",
    )(x, g, wg, wu, wd)


E_LANES = 128
RUN_ALIGN = 8
SLOT_ROWS = 2 * ROUTE_T + RUN_ALIGN * N_EXPERTS
RUN_BITS = tuple(range((ROUTE_T // RUN_ALIGN).bit_length() - 1, -1, -1))
TAIL_BITS = tuple(range((EXPERT_TM // RUN_ALIGN).bit_length() - 2, -1, -1))
SLOT_BITS = tuple(range((SLOT_ROWS // RUN_ALIGN).bit_length() - 1, -1, -1))


def _route_kernel(x_ref, g_ref, router_ref, h_ref, info_ref, cnt_ref):
    T = x_ref.shape[0]
    h = _rms(x_ref[...], g_ref[...])
    h_hi = h.astype(BF16)
    h_ref[...] = h_hi
    h_lo = (h - h_hi.astype(F32)).astype(BF16)
    r = router_ref[...]
    r_hi = r.astype(BF16)
    r_lo = (r - r_hi.astype(F32)).astype(BF16)
    hi_both = _dot(h_hi, jnp.concatenate([r_hi, r_lo], axis=1))
    logits = hi_both[:, :E_LANES] + (_dot(h_lo, r_hi) + hi_both[:, E_LANES:])
    lane = lax.broadcasted_iota(jnp.int32, (T, E_LANES), 1)
    logits = jnp.where(lane < N_EXPERTS, logits, -jnp.inf)
    m1 = jnp.max(logits, axis=-1, keepdims=True)
    i1 = jnp.min(jnp.where(logits == m1, lane, E_LANES), axis=-1, keepdims=True)
    rest = jnp.where(lane == i1, -jnp.inf, logits)
    m2 = jnp.max(rest, axis=-1, keepdims=True)
    i2 = jnp.min(jnp.where(rest == m2, lane, E_LANES), axis=-1, keepdims=True)
    e2 = jnp.exp(m2 - m1)
    w1 = 1.0 / (1.0 + e2)
    w2 = e2 / (1.0 + e2)
    oh1 = (lane == i1).astype(F32)
    oh2 = (lane == i2).astype(F32)
    oh = oh1 + oh2
    ri = lax.broadcasted_iota(jnp.int32, (T, T), 0)
    ci = lax.broadcasted_iota(jnp.int32, (T, T), 1)
    rank = _dot((ri > ci).astype(BF16), oh.astype(BF16))
    cnt = jnp.sum(oh, axis=0, keepdims=True)
    padc = jnp.floor((cnt + (RUN_ALIGN - 1.0)) * (1.0 / RUN_ALIGN)) * RUN_ALIGN
    lane1 = lax.broadcasted_iota(jnp.int32, (1, E_LANES), 1)
    start = jnp.zeros((1, E_LANES), F32)
    for e in range(N_EXPERTS - 1):
        start = start + jnp.where(lane1 > e, padc[:, e:e + 1], 0.0)
    slot = start + rank
    q1 = jnp.sum(oh1 * slot, axis=-1, keepdims=True)
    q2 = jnp.sum(oh2 * slot, axis=-1, keepdims=True)
    info_ref[...] = jnp.where(lane == 0, q1, jnp.where(lane == 1, q2, jnp.where(
        lane == 2, w1, jnp.where(lane == 3, w2, 0.0))))
    cnt_ref[0] = padc


def _route(x, g, router_p):
    N, D = x.shape
    T = ROUTE_T
    n_t = N // T
    return pl.pallas_call(
        _route_kernel,
        out_shape=(jax.ShapeDtypeStruct((N, D), BF16),
                   jax.ShapeDtypeStruct((N, E_LANES), F32),
                   jax.ShapeDtypeStruct((n_t, 1, E_LANES), F32)),
        grid=(n_t,),
        in_specs=[pl.BlockSpec((T, D), lambda t: (t, 0)),
                  pl.BlockSpec(g.shape, lambda t: (0, 0)),
                  pl.BlockSpec(router_p.shape, lambda t: (0, 0))],
        out_specs=(pl.BlockSpec((T, D), lambda t: (t, 0)),
                   pl.BlockSpec((T, E_LANES), lambda t: (t, 0)),
                   pl.BlockSpec((1, 1, E_LANES), lambda t: (t, 0, 0))),
        compiler_params=pltpu.CompilerParams(
            dimension_semantics=("arbitrary",),
            vmem_limit_bytes=V7X_VMEM_LIMIT_BYTES),
        name="moe_route",
    )(x, g, router_p)


def _run_copies(src, dst, src0, dst0, rows, bits, sem):
    m = rows // RUN_ALIGN
    off = 0
    out = []
    for b in bits:
        size = RUN_ALIGN << b
        take = (m >> b) & 1
        s0 = 0 if src0 is None else pl.multiple_of(src0 + off, RUN_ALIGN)
        d0 = pl.multiple_of(dst0 + off, RUN_ALIGN)
        out.append((take == 1, pltpu.make_async_copy(
            src.at[pl.ds(s0, size)], dst.at[pl.ds(d0, size)], sem)))
        off = off + take * size
    return out


def _start(copies, enable=True):
    for pred, cp in copies:
        @pl.when(pred & enable)
        def _(cp=cp):
            cp.start()


def _wait(copies, enable=True):
    for pred, cp in copies:
        @pl.when(pred & enable)
        def _(cp=cp):
            cp.wait()


def _tile_runs(tile, table_refs, hbm, buf, sem, *, to_hbm):
    gdst_ref, glen_ref, gsrc_ref = table_refs
    copies = []
    for e in range(N_EXPERTS):
        j = tile * N_EXPERTS + e
        if to_hbm:
            copies += _run_copies(buf, hbm, gsrc_ref[j], gdst_ref[j], glen_ref[j], RUN_BITS, sem)
        else:
            copies += _run_copies(hbm, buf, gdst_ref[j], gsrc_ref[j], glen_ref[j], RUN_BITS, sem)
    return copies


def _tile_rows(tile, table_refs):
    _, glen_ref, gsrc_ref = table_refs
    last = tile * N_EXPERTS + N_EXPERTS - 1
    return gsrc_ref[last] + glen_ref[last]


def _wait_rows(rows, src, dst, sem):
    for pred, cp in _run_copies(src, dst, jnp.int32(0), jnp.int32(0), rows, SLOT_BITS, sem):
        @pl.when(pred)
        def _(cp=cp):
            cp.wait()


def _slots(info, T):
    q1 = info[:, 0:1].astype(jnp.int32)
    q2 = info[:, 1:2].astype(jnp.int32)
    r = lax.broadcasted_iota(jnp.int32, (T, SLOT_ROWS), 1)
    return q1 == r, q2 == r


def _dispatch_kernel(gdst_ref, glen_ref, gsrc_ref, tail0_ref, tailn_ref,
                     h_ref, info_ref, xs_hbm, xbuf, zbuf, sem, *, slack_pieces):
    t = pl.program_id(0)
    n_t = pl.num_programs(0)
    T = h_ref.shape[0]
    slot = t % 2
    tables = (gdst_ref, glen_ref, gsrc_ref)
    hit1, hit2 = _slots(info_ref[...], T)
    onehot = (hit1 | hit2).astype(BF16)
    xbuf[slot] = lax.dot_general(onehot, h_ref[...], (((0,), (0,)), ((), ())),
                                 preferred_element_type=F32)
    _start(_tile_runs(t, tables, xs_hbm, xbuf.at[slot], sem.at[slot], to_hbm=True))

    @pl.when(t > 0)
    def _():
        _wait_rows(_tile_rows(t - 1, tables), xbuf.at[1 - slot], xs_hbm, sem.at[1 - slot])

    @pl.when(t == n_t - 1)
    def _():
        _wait_rows(_tile_rows(t, tables), xbuf.at[slot], xs_hbm, sem.at[slot])

    @pl.when(t == 0)
    def _():
        zbuf[...] = jnp.zeros_like(zbuf)
        tails = []
        for e in range(N_EXPERTS):
            tails += _run_copies(zbuf, xs_hbm, None, tail0_ref[e], tailn_ref[e], TAIL_BITS,
                                 sem.at[2])
        piece = zbuf.shape[0]
        for c in range(slack_pieces):
            d0 = pl.multiple_of(tail0_ref[N_EXPERTS] + c * piece, RUN_ALIGN)
            tails.append((c < tailn_ref[N_EXPERTS], pltpu.make_async_copy(
                zbuf, xs_hbm.at[pl.ds(d0, piece)], sem.at[2])))
        _start(tails)
        _wait(tails)


def _dispatch(h, info, gdst, glen, gsrc, tail0, tailn, rows_max):
    N, D = h.shape
    T = ROUTE_T
    slack_pieces = (rows_max - 2 * N) // (EXPERT_TM // 2)
    return pl.pallas_call(
        functools.partial(_dispatch_kernel, slack_pieces=slack_pieces),
        out_shape=jax.ShapeDtypeStruct((rows_max, D), F32),
        grid_spec=pltpu.PrefetchScalarGridSpec(
            num_scalar_prefetch=5,
            grid=(N // T,),
            in_specs=[pl.BlockSpec((T, D), lambda t, *_: (t, 0)),
                      pl.BlockSpec((T, E_LANES), lambda t, *_: (t, 0))],
            out_specs=pl.BlockSpec(memory_space=pl.ANY),
            scratch_shapes=[pltpu.VMEM((2, SLOT_ROWS, D), F32),
                            pltpu.VMEM((EXPERT_TM // 2, D), F32),
                            pltpu.SemaphoreType.DMA((3,))]),
        compiler_params=pltpu.CompilerParams(
            dimension_semantics=("arbitrary",),
            vmem_limit_bytes=V7X_VMEM_LIMIT_BYTES),
        name="moe_dispatch",
    )(gdst, glen, gsrc, tail0, tailn, h, info)


def _experts_kernel(te_ref, nv_ref, ns_ref, x_ref, wg_ref, wu_ref, wd_ref, o_ref,
                    xb):
    i = pl.program_id(0)
    k = pl.program_id(1)
    n_sub = ns_ref[i]

    full = x_ref.shape[0] // EXPERT_SUB

    @pl.when((n_sub > 0) & (k == 0))
    def _():
        xb[...] = x_ref[...].astype(BF16)
        o_ref[...] = jnp.zeros_like(o_ref)

    @pl.when(n_sub == full)
    def _():
        h = xb[...]
        act = (_silu(_dot(h, wg_ref[0].astype(BF16))) * _dot(h, wu_ref[0].astype(BF16))).astype(BF16)
        o_ref[...] += _dot(act, wd_ref[0].astype(BF16))

    @pl.when((n_sub > 0) & (n_sub < full))
    def _():
        def piece(s, carry):
            rows = pl.ds(pl.multiple_of(s * EXPERT_SUB, EXPERT_SUB), EXPERT_SUB)
            h = xb[rows, :]
            act = (_silu(_dot(h, wg_ref[0].astype(BF16)))
                   * _dot(h, wu_ref[0].astype(BF16))).astype(BF16)
            o_ref[rows, :] += _dot(act, wd_ref[0].astype(BF16))
            return carry

        lax.fori_loop(0, n_sub, piece, 0)

    @pl.when((n_sub == 0) & (k == 0))
    def _():
        o_ref[...] = jnp.zeros_like(o_ref)


def _experts(xs, tile_expert, n_valid, n_sub, wg, wu, wd):
    M, D = xs.shape
    tm, fk = EXPERT_TM, EXPERT_FK
    E, _, F = wg.shape
    nk = F // fk

    def row_map(i, k, te, nv, ns):
        return (jnp.maximum(jnp.minimum(i, nv[0] - 1), 0), 0)

    def kk(i, k, nv):
        return jnp.where(i < nv[0], k, nk - 1)

    return pl.pallas_call(
        _experts_kernel,
        out_shape=jax.ShapeDtypeStruct((M, D), F32),
        grid_spec=pltpu.PrefetchScalarGridSpec(
            num_scalar_prefetch=3,
            grid=(M // tm, nk),
            in_specs=[
                pl.BlockSpec((tm, D), row_map),
                pl.BlockSpec((1, D, fk), lambda i, k, te, nv, ns: (te[i], 0, kk(i, k, nv))),
                pl.BlockSpec((1, D, fk), lambda i, k, te, nv, ns: (te[i], 0, kk(i, k, nv))),
                pl.BlockSpec((1, fk, D), lambda i, k, te, nv, ns: (te[i], kk(i, k, nv), 0)),
            ],
            out_specs=pl.BlockSpec((tm, D), lambda i, k, te, nv, ns: (i, 0)),
            scratch_shapes=[pltpu.VMEM((tm, D), BF16)]),
        compiler_params=pltpu.CompilerParams(
            dimension_semantics=("arbitrary", "arbitrary"),
            vmem_limit_bytes=V7X_VMEM_LIMIT_BYTES),
        name="moe_experts",
    )(tile_expert, n_valid, n_sub, xs, wg, wu, wd)


def _combine_kernel(gdst_ref, glen_ref, gsrc_ref, x_ref, info_ref, gf_ref, ys_hbm, o_ref,
                    ybuf, sem):
    t = pl.program_id(0)
    n_t = pl.num_programs(0)
    T = x_ref.shape[0]
    slot = t % 2
    tables = (gdst_ref, glen_ref, gsrc_ref)
    def fetch(tile, into):
        ybuf[into, 2 * T:SLOT_ROWS, :] = jnp.zeros((SLOT_ROWS - 2 * T, ybuf.shape[2]), F32)
        _start(_tile_runs(tile, tables, ys_hbm, ybuf.at[into], sem.at[into], to_hbm=False))

    @pl.when(t == 0)
    def _():
        fetch(t, slot)

    @pl.when(t + 1 < n_t)
    def _():
        fetch(t + 1, 1 - slot)

    _wait_rows(_tile_rows(t, tables), ys_hbm, ybuf.at[slot], sem.at[slot])
    y = ybuf[slot].astype(BF16)
    info = info_ref[...]
    hit1, hit2 = _slots(info, T)
    weights = (jnp.where(hit1, info[:, 2:3], 0.0) + jnp.where(hit2, info[:, 3:4], 0.0)).astype(BF16)
    o_ref[...] = _rms(x_ref[...] + _dot(weights, y), gf_ref[...])


def _combine(x, info, gf, ys, gdst, glen, gsrc):
    N, D = x.shape
    T = ROUTE_T
    return pl.pallas_call(
        _combine_kernel,
        out_shape=jax.ShapeDtypeStruct((N, D), F32),
        grid_spec=pltpu.PrefetchScalarGridSpec(
            num_scalar_prefetch=3,
            grid=(N // T,),
            in_specs=[pl.BlockSpec((T, D), lambda t, *_: (t, 0)),
                      pl.BlockSpec((T, E_LANES), lambda t, *_: (t, 0)),
                      pl.BlockSpec(gf.shape, lambda t, *_: (0, 0)),
                      pl.BlockSpec(memory_space=pl.ANY)],
            out_specs=pl.BlockSpec((T, D), lambda t, *_: (t, 0)),
            scratch_shapes=[pltpu.VMEM((2, SLOT_ROWS, D), F32),
                            pltpu.SemaphoreType.DMA((2,))]),
        compiler_params=pltpu.CompilerParams(
            dimension_semantics=("arbitrary",),
            vmem_limit_bytes=V7X_VMEM_LIMIT_BYTES),
        name="moe_combine",
    )(gdst, glen, gsrc, x, info, gf, ys)


def _moe(x, g, router, wg, wu, wd, gf):
    N, D = x.shape
    n_t = N // ROUTE_T
    tm = EXPERT_TM
    router_p = jnp.pad(router, ((0, 0), (0, E_LANES - N_EXPERTS)))
    h, info, cnt = _route(x, g, router_p)
    glen = cnt[:, 0, :N_EXPERTS].astype(jnp.int32)
    total = jnp.sum(glen, axis=0)
    gpad = (total + tm - 1) // tm * tm
    gend = jnp.cumsum(gpad)
    goff = gend - gpad
    gdst = goff[None, :] + jnp.cumsum(glen, axis=0) - glen
    gsrc = jnp.cumsum(glen, axis=1) - glen
    rows_max = -(-(2 * N + n_t * N_EXPERTS * (RUN_ALIGN - 1) + N_EXPERTS * (tm - RUN_ALIGN)) // tm) * tm
    n_tiles = rows_max // tm
    n_valid = (gend[-1] // tm).reshape(1)
    tile_row = jnp.minimum(jnp.arange(n_tiles, dtype=jnp.int32), n_valid[0] - 1) * tm
    tile_expert = jnp.sum((tile_row[:, None] >= gend[None, :]).astype(jnp.int32), axis=1)
    tile_rows = jnp.clip((goff + total)[tile_expert] - tile_row, 0, tm)
    tile_rows = jnp.where(jnp.arange(n_tiles) < n_valid[0], tile_rows, 0)
    n_sub = (tile_rows + EXPERT_SUB - 1) // EXPERT_SUB
    flat = lambda a: a.reshape(-1).astype(jnp.int32)
    tail0 = jnp.concatenate([goff + total, gend[-1:]])
    tailn = jnp.concatenate([gpad - total, (rows_max - gend[-1:]) // (tm // 2)])
    xs = _dispatch(h, info, flat(gdst), flat(glen), flat(gsrc), flat(tail0), flat(tailn), rows_max)
    ys = _experts(xs, flat(tile_expert), flat(n_valid), flat(n_sub), wg, wu, wd)
    return _combine(x, info, gf, ys, flat(gdst), flat(glen), flat(gsrc))


def kernel(x, norm1_g, w_in, pool_w, pool_scale, gm_norm_g, gm_ws, gm_b,
           conv_dw_w, conv_dw_b, conv_ln_g, conv_ln_b, conv_pw_w, conv_pw_b,
           w_out, norm2_g, ffn_wg, ffn_wu, ffn_wd,
           moe_router, moe_wg, moe_wu, moe_wd, final_g):
    B, S, D = x.shape
    depth = w_in.shape[0]
    assert depth == 2, "layer 0 is the dense SwiGLU layer, layer 1 the expert layer + final norm"
    row = lambda t: t.reshape(1, -1)
    for l in range(depth):
        pool_bd = jax.scipy.linalg.block_diag(*[pool_w[l, gi] for gi in range(len(POOL_WINDOWS))])
        gm_wcat = jnp.transpose(gm_ws[l], (1, 0, 2)).reshape(GMLP_BLOCK, GMLP_HEADS * GMLP_BLOCK)
        gm_bias = jnp.repeat(gm_b[l].T, GMLP_HEAD_DIM, axis=1)
        x = _mixer(x, row(norm1_g[l]), w_in[l].astype(BF16), pool_bd.astype(BF16),
                   row(pool_scale[l]), row(gm_norm_g[l]), gm_wcat, gm_bias,
                   conv_dw_w[l], row(conv_dw_b[l]), row(conv_ln_g[l]), row(conv_ln_b[l]),
                   conv_pw_w[l].astype(BF16), row(conv_pw_b[l]), w_out[l].astype(BF16))
        xf = x.reshape(B * S, D)
        j = l // 2
        if l % 2 == 0:
            xf = _ffn(xf, row(norm2_g[l]), ffn_wg[j].astype(BF16), ffn_wu[j].astype(BF16),
                      ffn_wd[j].astype(BF16))
        else:
            xf = _moe(xf, row(norm2_g[l]), moe_router[j], moe_wg[j], moe_wu[j], moe_wd[j],
                      row(final_g))
        x = xf.reshape(B, S, D)
    return x
```

```python
import functools

import jax
import jax.numpy as jnp
from jax import lax
from jax.experimental import pallas as pl
from jax.experimental.pallas import tpu as pltpu

D_MODEL = 1024
CHUNK = 64
D_POOL = 256
POOL_WINDOWS = (2, 4, 8, 16)
POOL_GROUP = 64
D_GMLP = 384
GMLP_HEADS = 4
GMLP_HEAD_DIM = 96
GMLP_BLOCK = 128
D_CONV = 384
CONV_WIDTH = 31
D_IN = D_POOL + 2 * D_GMLP + 2 * D_CONV
D_FF = 2816
N_EXPERTS = 8
D_FF_EXPERT = 3584
EPS = 1e-6

V7X_VMEM_LIMIT_BYTES = 56 * 1024 * 1024

POOL_PAD = 8
POOL_HALO = POOL_PAD + 16
CONV_HALO = 32
MIX_TM = 1024
CONV_ROWS = 256
FFN_TM = 1024
ROUTE_T = 512
EXPERT_TM = 1024
EXPERT_FK = 512
EXPERT_SUB = 256

BF16 = jnp.bfloat16
F32 = jnp.float32


def _rms(x, g):
    return x * lax.rsqrt(jnp.mean(x * x, axis=-1, keepdims=True) + EPS) * g


def _silu(x):
    return x * jax.nn.sigmoid(x)


def _dot(a, b):
    return jnp.dot(a, b, preferred_element_type=F32)


def _mixer_kernel(x_ref, g1_ref, win_ref, poolw_ref, pools_ref, gmg_ref,
                  gmws_ref, gmb_ref, dww_ref, dwb_ref, lng_ref, lnb_ref,
                  pww_ref, pwb_ref, wout_ref, *rest, route):
    if route:
        g2_ref, router_ref, o_ref, h_ref, info_ref, cnt_ref = rest[:6]
        rest = rest[6:]
    else:
        o_ref = rest[0]
        rest = rest[1:]
    abuf, s2buf, s4buf, s8buf, hbuf, hshift, wtap, ycat = rest
    tm = x_ref.shape[1]
    s = pl.program_id(1)

    @pl.when(s == 0)
    def _():
        abuf[0:POOL_HALO, :] = jnp.zeros((POOL_HALO, D_POOL), F32)
        s2buf[0:POOL_PAD, :] = jnp.zeros((POOL_PAD, D_POOL), F32)
        s4buf[0:POOL_PAD, :] = jnp.zeros((POOL_PAD, D_POOL), F32)
        s8buf[0:POOL_PAD, :] = jnp.zeros((POOL_PAD, 128), F32)
        hbuf[0:CONV_HALO, :] = jnp.zeros((CONV_HALO, D_CONV), F32)
        for j in range(CONV_WIDTH):
            wtap[j] = jnp.broadcast_to(dww_ref[j:j + 1, :], (8, D_CONV))

    x = x_ref[0]
    h = _rms(x, g1_ref[...])
    p = _dot(h.astype(BF16), win_ref[...])
    o1 = D_POOL
    o2 = o1 + D_GMLP
    o3 = o2 + D_GMLP
    o4 = o3 + D_CONV
    u = p[:, o1:o2]
    v = p[:, o2:o3]
    cv = p[:, o3:o4]
    cg = p[:, o4:]

    H = POOL_HALO
    n = tm + H - POOL_PAD
    abuf[H:H + tm, :] = p[:, :o1]
    s2buf[POOL_PAD:POOL_PAD + n, :] = abuf[POOL_PAD:POOL_PAD + n, :] + abuf[POOL_PAD - 1:POOL_PAD - 1 + n, :]
    s4buf[POOL_PAD:POOL_PAD + n, :] = s2buf[POOL_PAD:POOL_PAD + n, :] + s2buf[POOL_PAD - 2:POOL_PAD - 2 + n, :]
    s8buf[POOL_PAD:POOL_PAD + n, :] = (s4buf[POOL_PAD:POOL_PAD + n, 128:]
                                       + s4buf[POOL_PAD - 4:POOL_PAD - 4 + n, 128:])
    s16 = s8buf[H:H + tm, :] + s8buf[H - 8:H - 8 + tm, :]
    lane = lax.broadcasted_iota(jnp.int32, (1, 128), 1)
    row = lax.broadcasted_iota(jnp.int32, (tm, 1), 0)
    tpos = (s * tm + row + 1).astype(F32)
    first = lane < POOL_GROUP
    ys = []
    for lo, (w_small, w_big), s_small, s_big in (
            (0, POOL_WINDOWS[0:2], s2buf[H:H + tm, 0:128], s4buf[H:H + tm, 0:128]),
            (128, POOL_WINDOWS[2:4], s8buf[H:H + tm, :], s16)):
        ssel = jnp.where(first, s_small, s_big)
        win = jnp.where(first, float(w_small), float(w_big))
        mean = ssel / jnp.minimum(tpos, win)
        ys.append(mean - abuf[H:H + tm, lo:lo + 128])
    y = jnp.concatenate(ys, axis=1).astype(BF16)
    ya = _dot(y, poolw_ref[...]) * pools_ref[...]
    ycat[:, 0:o1] = ya.astype(BF16)
    abuf[POOL_PAD:H, :] = abuf[tm + POOL_PAD:tm + H, :]

    vn = _rms(v, gmg_ref[...])
    ri = lax.broadcasted_iota(jnp.int32, (GMLP_BLOCK, GMLP_HEADS * GMLP_BLOCK), 0)
    cj = lax.broadcasted_iota(jnp.int32, (GMLP_BLOCK, GMLP_HEADS * GMLP_BLOCK), 1)
    causal = (ri // CHUNK) >= ((cj % GMLP_BLOCK) // CHUNK)
    wcat = jnp.where(causal, gmws_ref[...], 0.0).astype(BF16)
    hid = lax.broadcasted_iota(jnp.int32, (1, D_GMLP), 1) // GMLP_HEAD_DIM
    gmb = gmb_ref[...]
    for blk in range(tm // GMLP_BLOCK):
        r0 = blk * GMLP_BLOCK
        vb = vn[r0:r0 + GMLP_BLOCK, :]
        vstack = jnp.concatenate(
            [jnp.where(hid == hh, vb, 0.0) for hh in range(GMLP_HEADS)],
            axis=0).astype(BF16)
        z = _dot(wcat, vstack) + gmb
        ycat[r0:r0 + GMLP_BLOCK, o1:o2] = (u[r0:r0 + GMLP_BLOCK, :] * z).astype(BF16)

    hbuf[CONV_HALO:CONV_HALO + tm, :] = cv * jax.nn.sigmoid(cg)
    base = CONV_HALO - (CONV_WIDTH - 1)
    dwb = dwb_ref[...]
    lng = lng_ref[...]
    lnb = lnb_ref[...]
    pwb = pwb_ref[...]
    for b in range(8):
        n_rows = tm + 8 * (len(range(b, CONV_WIDTH, 8)) - 1)
        hshift[b, 0:n_rows, :] = hbuf[base + b:base + b + n_rows, :]
    for c0 in range(0, tm, CONV_ROWS):
        acc = None
        for j in range(CONV_WIDTH):
            r0 = c0 + 8 * (j // 8)
            win = hshift[j % 8, r0:r0 + CONV_ROWS, :].reshape(CONV_ROWS // 8, 8, D_CONV)
            term = wtap[j] * win
            acc = term if acc is None else acc + term
        acc = acc.reshape(CONV_ROWS, D_CONV) + dwb
        mu = jnp.mean(acc, axis=-1, keepdims=True)
        cen = acc - mu
        var = jnp.mean(cen * cen, axis=-1, keepdims=True)
        ln = cen * lax.rsqrt(var + EPS) * lng + lnb
        yc = _dot(_silu(ln).astype(BF16), pww_ref[...]) + pwb
        ycat[c0:c0 + CONV_ROWS, o2:D_MODEL] = yc.astype(BF16)
    hbuf[0:CONV_HALO, :] = hbuf[tm:tm + CONV_HALO, :]

    o_ref[0] = x + _dot(ycat[...], wout_ref[...])

    if route:
        for r in range(tm // ROUTE_T):
            rows = slice(r * ROUTE_T, (r + 1) * ROUTE_T)
            h_hi, info, padc = _route_tile(o_ref[0, rows, :], g2_ref[...], router_ref[...])
            h_ref[rows, :] = h_hi
            info_ref[rows, :] = info
            cnt_ref[r] = padc


def _const_spec(shape):
    nd = len(shape)
    return pl.BlockSpec(shape, lambda b, s: (0,) * nd, pipeline_mode=pl.Buffered(1))


def _mixer(x, g1, w_in, pool_bd, pool_scale, gm_g, gm_wcat, gm_bias, dw_w, dw_b,
           ln_g, ln_b, pw_w, pw_b, w_out, route_params=None):
    B, S, D = x.shape
    tm = MIX_TM
    tps = S // tm
    weights = (g1, w_in, pool_bd, pool_scale, gm_g, gm_wcat, gm_bias, dw_w, dw_b,
               ln_g, ln_b, pw_w, pw_b, w_out)
    out_shape = jax.ShapeDtypeStruct((B, S, D), F32)
    out_specs = pl.BlockSpec((1, tm, D), lambda b, s: (b, s, 0))
    if route_params is not None:
        weights += tuple(route_params)
        per_step = tm // ROUTE_T
        out_shape = (out_shape,
                     jax.ShapeDtypeStruct((B * S, D), BF16),
                     jax.ShapeDtypeStruct((B * S, E_LANES), F32),
                     jax.ShapeDtypeStruct((B * S // ROUTE_T, 1, E_LANES), F32))
        out_specs = (out_specs,
                     pl.BlockSpec((tm, D), lambda b, s: (b * tps + s, 0)),
                     pl.BlockSpec((tm, E_LANES), lambda b, s: (b * tps + s, 0)),
                     pl.BlockSpec((per_step, 1, E_LANES), lambda b, s: (b * tps + s, 0, 0)))
    return pl.pallas_call(
        functools.partial(_mixer_kernel, route=route_params is not None),
        out_shape=out_shape,
        grid=(B, S // tm),
        in_specs=[pl.BlockSpec((1, tm, D), lambda b, s: (b, s, 0))]
        + [_const_spec(w.shape) for w in weights],
        out_specs=out_specs,
        scratch_shapes=[
            pltpu.VMEM((POOL_HALO + tm, D_POOL), F32),
            pltpu.VMEM((POOL_HALO + tm, D_POOL), F32),
            pltpu.VMEM((POOL_HALO + tm, D_POOL), F32),
            pltpu.VMEM((POOL_HALO + tm, 128), F32),
            pltpu.VMEM((CONV_HALO + tm, D_CONV), F32),
            pltpu.VMEM((8, tm + CONV_HALO - 8, D_CONV), F32),
            pltpu.VMEM((CONV_WIDTH, 8, D_CONV), F32),
            pltpu.VMEM((tm, D_MODEL), BF16),
        ],
        compiler_params=pltpu.CompilerParams(
            dimension_semantics=("arbitrary", "arbitrary"),
            vmem_limit_bytes=V7X_VMEM_LIMIT_BYTES),
        name="token_mixer",
    )(x, *weights)


def _ffn_kernel(x_ref, g_ref, wg_ref, wu_ref, wd_ref, o_ref):
    x = x_ref[...]
    h = _rms(x, g_ref[...]).astype(BF16)
    gate = _dot(h, wg_ref[...])
    up = _dot(h, wu_ref[...])
    act = (_silu(gate) * up).astype(BF16)
    o_ref[...] = x + _dot(act, wd_ref[...])


def _ffn(x, g, wg, wu, wd):
    N, D = x.shape
    tm = FFN_TM
    const = lambda shape: pl.BlockSpec(shape, lambda i: (0, 0), pipeline_mode=pl.Buffered(1))
    return pl.pallas_call(
        _ffn_kernel,
        out_shape=jax.ShapeDtypeStruct((N, D), F32),
        grid=(N // tm,),
        in_specs=[pl.BlockSpec((tm, D), lambda i: (i, 0)), const(g.shape),
                  const(wg.shape), const(wu.shape), const(wd.shape)],
        out_specs=pl.BlockSpec((tm, D), lambda i: (i, 0)),
        compiler_params=pltpu.CompilerParams(
            dimension_semantics=("arbitrary",),
            vmem_limit_bytes=V7X_VMEM_LIMIT_BYTES),
        name="dense_swiglu",
    )(x, g, wg, wu, wd)


E_LANES = 128
RUN_ALIGN = 8
SLOT_ROWS = 2 * ROUTE_T + RUN_ALIGN * N_EXPERTS
RUN_BITS = tuple(range((ROUTE_T // RUN_ALIGN).bit_length() - 1, -1, -1))
TAIL_BITS = tuple(range((EXPERT_TM // RUN_ALIGN).bit_length() - 2, -1, -1))
SLOT_BITS = tuple(range((SLOT_ROWS // RUN_ALIGN).bit_length() - 1, -1, -1))


def _route_tile(x, g, r):
    T = x.shape[0]
    h = _rms(x, g)
    h_hi = h.astype(BF16)
    h_lo = (h - h_hi.astype(F32)).astype(BF16)
    r_hi = r.astype(BF16)
    r_lo = (r - r_hi.astype(F32)).astype(BF16)
    hi_both = _dot(h_hi, jnp.concatenate([r_hi, r_lo], axis=1))
    logits = hi_both[:, :E_LANES] + (_dot(h_lo, r_hi) + hi_both[:, E_LANES:])
    lane = lax.broadcasted_iota(jnp.int32, (T, E_LANES), 1)
    logits = jnp.where(lane < N_EXPERTS, logits, -jnp.inf)
    m1 = jnp.max(logits, axis=-1, keepdims=True)
    i1 = jnp.min(jnp.where(logits == m1, lane, E_LANES), axis=-1, keepdims=True)
    rest = jnp.where(lane == i1, -jnp.inf, logits)
    m2 = jnp.max(rest, axis=-1, keepdims=True)
    i2 = jnp.min(jnp.where(rest == m2, lane, E_LANES), axis=-1, keepdims=True)
    e2 = jnp.exp(m2 - m1)
    w1 = 1.0 / (1.0 + e2)
    w2 = e2 / (1.0 + e2)
    oh1 = (lane == i1).astype(F32)
    oh2 = (lane == i2).astype(F32)
    oh = oh1 + oh2
    ri = lax.broadcasted_iota(jnp.int32, (T, T), 0)
    ci = lax.broadcasted_iota(jnp.int32, (T, T), 1)
    rank = _dot((ri > ci).astype(BF16), oh.astype(BF16))
    cnt = jnp.sum(oh, axis=0, keepdims=True)
    padc = jnp.floor((cnt + (RUN_ALIGN - 1.0)) * (1.0 / RUN_ALIGN)) * RUN_ALIGN
    lane1 = lax.broadcasted_iota(jnp.int32, (1, E_LANES), 1)
    start = jnp.zeros((1, E_LANES), F32)
    for e in range(N_EXPERTS - 1):
        start = start + jnp.where(lane1 > e, padc[:, e:e + 1], 0.0)
    slot = start + rank
    q1 = jnp.sum(oh1 * slot, axis=-1, keepdims=True)
    q2 = jnp.sum(oh2 * slot, axis=-1, keepdims=True)
    info = jnp.where(lane == 0, q1, jnp.where(lane == 1, q2, jnp.where(
        lane == 2, w1, jnp.where(lane == 3, w2, 0.0))))
    return h_hi, info, padc


def _run_copies(src, dst, src0, dst0, rows, bits, sem):
    m = rows // RUN_ALIGN
    off = 0
    out = []
    for b in bits:
        size = RUN_ALIGN << b
        take = (m >> b) & 1
        s0 = 0 if src0 is None else pl.multiple_of(src0 + off, RUN_ALIGN)
        d0 = pl.multiple_of(dst0 + off, RUN_ALIGN)
        out.append((take == 1, pltpu.make_async_copy(
            src.at[pl.ds(s0, size)], dst.at[pl.ds(d0, size)], sem)))
        off = off + take * size
    return out


def _start(copies, enable=True):
    for pred, cp in copies:
        @pl.when(pred & enable)
        def _(cp=cp):
            cp.start()


def _wait(copies, enable=True):
    for pred, cp in copies:
        @pl.when(pred & enable)
        def _(cp=cp):
            cp.wait()


def _tile_runs(tile, table_refs, hbm, buf, sem, *, to_hbm):
    gdst_ref, glen_ref, gsrc_ref = table_refs
    copies = []
    for e in range(N_EXPERTS):
        j = tile * N_EXPERTS + e
        if to_hbm:
            copies += _run_copies(buf, hbm, gsrc_ref[j], gdst_ref[j], glen_ref[j], RUN_BITS, sem)
        else:
            copies += _run_copies(hbm, buf, gdst_ref[j], gsrc_ref[j], glen_ref[j], RUN_BITS, sem)
    return copies


def _tile_rows(tile, table_refs):
    _, glen_ref, gsrc_ref = table_refs
    last = tile * N_EXPERTS + N_EXPERTS - 1
    return gsrc_ref[last] + glen_ref[last]


def _wait_rows(rows, src, dst, sem):
    for pred, cp in _run_copies(src, dst, jnp.int32(0), jnp.int32(0), rows, SLOT_BITS, sem):
        @pl.when(pred)
        def _(cp=cp):
            cp.wait()


def _slots(info, T):
    q1 = info[:, 0:1].astype(jnp.int32)
    q2 = info[:, 1:2].astype(jnp.int32)
    r = lax.broadcasted_iota(jnp.int32, (T, SLOT_ROWS), 1)
    return q1 == r, q2 == r


def _dispatch_kernel(gdst_ref, glen_ref, gsrc_ref, tail0_ref, tailn_ref,
                     h_ref, info_ref, xs_hbm, xbuf, zbuf, sem, *, slack_pieces):
    t = pl.program_id(0)
    n_t = pl.num_programs(0)
    T = h_ref.shape[0]
    slot = t % 2
    tables = (gdst_ref, glen_ref, gsrc_ref)
    hit1, hit2 = _slots(info_ref[...], T)
    onehot = (hit1 | hit2).astype(BF16)
    xbuf[slot] = lax.dot_general(onehot, h_ref[...], (((0,), (0,)), ((), ())),
                                 preferred_element_type=F32)
    _start(_tile_runs(t, tables, xs_hbm, xbuf.at[slot], sem.at[slot], to_hbm=True))

    @pl.when(t > 0)
    def _():
        _wait_rows(_tile_rows(t - 1, tables), xbuf.at[1 - slot], xs_hbm, sem.at[1 - slot])

    @pl.when(t == n_t - 1)
    def _():
        _wait_rows(_tile_rows(t, tables), xbuf.at[slot], xs_hbm, sem.at[slot])

    @pl.when(t == 0)
    def _():
        zbuf[...] = jnp.zeros_like(zbuf)
        tails = []
        for e in range(N_EXPERTS):
            tails += _run_copies(zbuf, xs_hbm, None, tail0_ref[e], tailn_ref[e], TAIL_BITS,
                                 sem.at[2])
        piece = zbuf.shape[0]
        for c in range(slack_pieces):
            d0 = pl.multiple_of(tail0_ref[N_EXPERTS] + c * piece, RUN_ALIGN)
            tails.append((c < tailn_ref[N_EXPERTS], pltpu.make_async_copy(
                zbuf, xs_hbm.at[pl.ds(d0, piece)], sem.at[2])))
        _start(tails)
        _wait(tails)


def _dispatch(h, info, gdst, glen, gsrc, tail0, tailn, rows_max):
    N, D = h.shape
    T = ROUTE_T
    slack_pieces = (rows_max - 2 * N) // (EXPERT_TM // 2)
    return pl.pallas_call(
        functools.partial(_dispatch_kernel, slack_pieces=slack_pieces),
        out_shape=jax.ShapeDtypeStruct((rows_max, D), F32),
        grid_spec=pltpu.PrefetchScalarGridSpec(
            num_scalar_prefetch=5,
            grid=(N // T,),
            in_specs=[pl.BlockSpec((T, D), lambda t, *_: (t, 0)),
                      pl.BlockSpec((T, E_LANES), lambda t, *_: (t, 0))],
            out_specs=pl.BlockSpec(memory_space=pl.ANY),
            scratch_shapes=[pltpu.VMEM((2, SLOT_ROWS, D), F32),
                            pltpu.VMEM((EXPERT_TM // 2, D), F32),
                            pltpu.SemaphoreType.DMA((3,))]),
        compiler_params=pltpu.CompilerParams(
            dimension_semantics=("arbitrary",),
            vmem_limit_bytes=V7X_VMEM_LIMIT_BYTES),
        name="moe_dispatch",
    )(gdst, glen, gsrc, tail0, tailn, h, info)


def _experts_kernel(te_ref, nv_ref, ns_ref, x_ref, wg_ref, wu_ref, wd_ref, o_ref,
                    xb):
    i = pl.program_id(0)
    k = pl.program_id(1)
    n_sub = ns_ref[i]

    full = x_ref.shape[0] // EXPERT_SUB

    @pl.when((n_sub > 0) & (k == 0))
    def _():
        xb[...] = x_ref[...].astype(BF16)
        o_ref[...] = jnp.zeros_like(o_ref)

    @pl.when(n_sub == full)
    def _():
        h = xb[...]
        act = (_silu(_dot(h, wg_ref[0].astype(BF16))) * _dot(h, wu_ref[0].astype(BF16))).astype(BF16)
        o_ref[...] += _dot(act, wd_ref[0].astype(BF16))

    @pl.when((n_sub > 0) & (n_sub < full))
    def _():
        def piece(s, carry):
            rows = pl.ds(pl.multiple_of(s * EXPERT_SUB, EXPERT_SUB), EXPERT_SUB)
            h = xb[rows, :]
            act = (_silu(_dot(h, wg_ref[0].astype(BF16)))
                   * _dot(h, wu_ref[0].astype(BF16))).astype(BF16)
            o_ref[rows, :] += _dot(act, wd_ref[0].astype(BF16))
            return carry

        lax.fori_loop(0, n_sub, piece, 0)

    @pl.when((n_sub == 0) & (k == 0))
    def _():
        o_ref[...] = jnp.zeros_like(o_ref)


def _experts(xs, tile_expert, n_valid, n_sub, wg, wu, wd):
    M, D = xs.shape
    tm, fk = EXPERT_TM, EXPERT_FK
    E, _, F = wg.shape
    nk = F // fk

    def row_map(i, k, te, nv, ns):
        return (jnp.maximum(jnp.minimum(i, nv[0] - 1), 0), 0)

    def kk(i, k, nv):
        return jnp.where(i < nv[0], k, nk - 1)

    return pl.pallas_call(
        _experts_kernel,
        out_shape=jax.ShapeDtypeStruct((M, D), F32),
        grid_spec=pltpu.PrefetchScalarGridSpec(
            num_scalar_prefetch=3,
            grid=(M // tm, nk),
            in_specs=[
                pl.BlockSpec((tm, D), row_map),
                pl.BlockSpec((1, D, fk), lambda i, k, te, nv, ns: (te[i], 0, kk(i, k, nv))),
                pl.BlockSpec((1, D, fk), lambda i, k, te, nv, ns: (te[i], 0, kk(i, k, nv))),
                pl.BlockSpec((1, fk, D), lambda i, k, te, nv, ns: (te[i], kk(i, k, nv), 0)),
            ],
            out_specs=pl.BlockSpec((tm, D), lambda i, k, te, nv, ns: (i, 0)),
            scratch_shapes=[pltpu.VMEM((tm, D), BF16)]),
        compiler_params=pltpu.CompilerParams(
            dimension_semantics=("arbitrary", "arbitrary"),
            vmem_limit_bytes=V7X_VMEM_LIMIT_BYTES),
        name="moe_experts",
    )(tile_expert, n_valid, n_sub, xs, wg, wu, wd)


def _combine_kernel(gdst_ref, glen_ref, gsrc_ref, x_ref, info_ref, gf_ref, ys_hbm, o_ref,
                    ybuf, sem):
    t = pl.program_id(0)
    n_t = pl.num_programs(0)
    T = x_ref.shape[0]
    slot = t % 2
    tables = (gdst_ref, glen_ref, gsrc_ref)
    def fetch(tile, into):
        ybuf[into, 2 * T:SLOT_ROWS, :] = jnp.zeros((SLOT_ROWS - 2 * T, ybuf.shape[2]), F32)
        _start(_tile_runs(tile, tables, ys_hbm, ybuf.at[into], sem.at[into], to_hbm=False))

    @pl.when(t == 0)
    def _():
        fetch(t, slot)

    @pl.when(t + 1 < n_t)
    def _():
        fetch(t + 1, 1 - slot)

    _wait_rows(_tile_rows(t, tables), ys_hbm, ybuf.at[slot], sem.at[slot])
    y = ybuf[slot].astype(BF16)
    info = info_ref[...]
    hit1, hit2 = _slots(info, T)
    weights = (jnp.where(hit1, info[:, 2:3], 0.0) + jnp.where(hit2, info[:, 3:4], 0.0)).astype(BF16)
    o_ref[...] = _rms(x_ref[...] + _dot(weights, y), gf_ref[...])


def _combine(x, info, gf, ys, gdst, glen, gsrc):
    N, D = x.shape
    T = ROUTE_T
    return pl.pallas_call(
        _combine_kernel,
        out_shape=jax.ShapeDtypeStruct((N, D), F32),
        grid_spec=pltpu.PrefetchScalarGridSpec(
            num_scalar_prefetch=3,
            grid=(N // T,),
            in_specs=[pl.BlockSpec((T, D), lambda t, *_: (t, 0)),
                      pl.BlockSpec((T, E_LANES), lambda t, *_: (t, 0)),
                      pl.BlockSpec(gf.shape, lambda t, *_: (0, 0)),
                      pl.BlockSpec(memory_space=pl.ANY)],
            out_specs=pl.BlockSpec((T, D), lambda t, *_: (t, 0)),
            scratch_shapes=[pltpu.VMEM((2, SLOT_ROWS, D), F32),
                            pltpu.SemaphoreType.DMA((2,))]),
        compiler_params=pltpu.CompilerParams(
            dimension_semantics=("arbitrary",),
            vmem_limit_bytes=V7X_VMEM_LIMIT_BYTES),
        name="moe_combine",
    )(gdst, glen, gsrc, x, info, gf, ys)


def _moe(x, h, info, cnt, wg, wu, wd, gf):
    N, D = x.shape
    n_t = N // ROUTE_T
    tm = EXPERT_TM
    glen = cnt[:, 0, :N_EXPERTS].astype(jnp.int32)
    total = jnp.sum(glen, axis=0)
    gpad = (total + tm - 1) // tm * tm
    gend = jnp.cumsum(gpad)
    goff = gend - gpad
    gdst = goff[None, :] + jnp.cumsum(glen, axis=0) - glen
    gsrc = jnp.cumsum(glen, axis=1) - glen
    rows_max = -(-(2 * N + n_t * N_EXPERTS * (RUN_ALIGN - 1) + N_EXPERTS * (tm - RUN_ALIGN)) // tm) * tm
    n_tiles = rows_max // tm
    n_valid = (gend[-1] // tm).reshape(1)
    tile_row = jnp.minimum(jnp.arange(n_tiles, dtype=jnp.int32), n_valid[0] - 1) * tm
    tile_expert = jnp.sum((tile_row[:, None] >= gend[None, :]).astype(jnp.int32), axis=1)
    tile_rows = jnp.clip((goff + total)[tile_expert] - tile_row, 0, tm)
    tile_rows = jnp.where(jnp.arange(n_tiles) < n_valid[0], tile_rows, 0)
    n_sub = (tile_rows + EXPERT_SUB - 1) // EXPERT_SUB
    flat = lambda a: a.reshape(-1).astype(jnp.int32)
    tail0 = jnp.concatenate([goff + total, gend[-1:]])
    tailn = jnp.concatenate([gpad - total, (rows_max - gend[-1:]) // (tm // 2)])
    xs = _dispatch(h, info, flat(gdst), flat(glen), flat(gsrc), flat(tail0), flat(tailn), rows_max)
    ys = _experts(xs, flat(tile_expert), flat(n_valid), flat(n_sub), wg, wu, wd)
    return _combine(x, info, gf, ys, flat(gdst), flat(glen), flat(gsrc))


def kernel(x, norm1_g, w_in, pool_w, pool_scale, gm_norm_g, gm_ws, gm_b,
           conv_dw_w, conv_dw_b, conv_ln_g, conv_ln_b, conv_pw_w, conv_pw_b,
           w_out, norm2_g, ffn_wg, ffn_wu, ffn_wd,
           moe_router, moe_wg, moe_wu, moe_wd, final_g):
    B, S, D = x.shape
    depth = w_in.shape[0]
    assert depth == 2, "layer 0 is the dense SwiGLU layer, layer 1 the expert layer + final norm"
    row = lambda t: t.reshape(1, -1)
    for l in range(depth):
        pool_bd = jax.scipy.linalg.block_diag(*[pool_w[l, gi] for gi in range(len(POOL_WINDOWS))])
        gm_wcat = jnp.transpose(gm_ws[l], (1, 0, 2)).reshape(GMLP_BLOCK, GMLP_HEADS * GMLP_BLOCK)
        gm_bias = jnp.repeat(gm_b[l].T, GMLP_HEAD_DIM, axis=1)
        j = l // 2
        expert_layer = l % 2 == 1
        route_params = None
        if expert_layer:
            router_p = jnp.pad(moe_router[j], ((0, 0), (0, E_LANES - N_EXPERTS)))
            route_params = (row(norm2_g[l]), router_p)
        mixed = _mixer(x, row(norm1_g[l]), w_in[l].astype(BF16), pool_bd.astype(BF16),
                       row(pool_scale[l]), row(gm_norm_g[l]), gm_wcat, gm_bias,
                       conv_dw_w[l], row(conv_dw_b[l]), row(conv_ln_g[l]), row(conv_ln_b[l]),
                       conv_pw_w[l].astype(BF16), row(conv_pw_b[l]), w_out[l].astype(BF16),
                       route_params=route_params)
        if expert_layer:
            x, h, info, cnt = mixed
            xf = _moe(x.reshape(B * S, D), h, info, cnt, moe_wg[j], moe_wu[j], moe_wd[j],
                      row(final_g))
        else:
            xf = _ffn(mixed.reshape(B * S, D), row(norm2_g[l]), ffn_wg[j].astype(BF16),
                      ffn_wu[j].astype(BF16), ffn_wd[j].astype(BF16))
        x = xf.reshape(B, S, D)
    return x
```

```python
import functools

import jax
import jax.numpy as jnp
from jax import lax
from jax.experimental import pallas as pl
from jax.experimental.pallas import tpu as pltpu

D_MODEL = 1024
CHUNK = 64
D_POOL = 256
POOL_WINDOWS = (2, 4, 8, 16)
POOL_GROUP = 64
D_GMLP = 384
GMLP_HEADS = 4
GMLP_HEAD_DIM = 96
GMLP_BLOCK = 128
D_CONV = 384
CONV_WIDTH = 31
D_IN = D_POOL + 2 * D_GMLP + 2 * D_CONV
D_FF = 2816
N_EXPERTS = 8
D_FF_EXPERT = 3584
EPS = 1e-6

V7X_VMEM_LIMIT_BYTES = 56 * 1024 * 1024

POOL_PAD = 8
POOL_HALO = POOL_PAD + 16
CONV_HALO = 32
MIX_TM = 1024
CONV_ROWS = 256
FFN_TM = 1024
ROUTE_T = 512
EXPERT_TM = 1024
EXPERT_FK = 512
EXPERT_SUB = 256

BF16 = jnp.bfloat16
F32 = jnp.float32


def _rms(x, g):
    return x * lax.rsqrt(jnp.mean(x * x, axis=-1, keepdims=True) + EPS) * g


def _silu(x):
    return x * jax.nn.sigmoid(x)


def _dot(a, b):
    return jnp.dot(a, b, preferred_element_type=F32)


def _mixer_kernel(x_ref, g1_ref, win_ref, poolw_ref, pools_ref, gmg_ref,
                  gmws_ref, gmb_ref, dww_ref, dwb_ref, lng_ref, lnb_ref,
                  pww_ref, pwb_ref, wout_ref, *rest, route):
    if route:
        g2_ref, router_ref, o_ref, h_ref, info_ref, cnt_ref = rest[:6]
        rest = rest[6:]
    else:
        o_ref = rest[0]
        rest = rest[1:]
    abuf, s2buf, s4buf, s8buf, hbuf, hshift, wtap, ycat = rest
    tm = x_ref.shape[1]
    s = pl.program_id(1)

    @pl.when(s == 0)
    def _():
        abuf[0:POOL_HALO, :] = jnp.zeros((POOL_HALO, D_POOL), F32)
        s2buf[0:POOL_PAD, :] = jnp.zeros((POOL_PAD, D_POOL), F32)
        s4buf[0:POOL_PAD, :] = jnp.zeros((POOL_PAD, D_POOL), F32)
        s8buf[0:POOL_PAD, :] = jnp.zeros((POOL_PAD, 128), F32)
        hbuf[0:CONV_HALO, :] = jnp.zeros((CONV_HALO, D_CONV), F32)
        for j in range(CONV_WIDTH):
            wtap[j] = jnp.broadcast_to(dww_ref[j:j + 1, :], (8, D_CONV))

    x = x_ref[0]
    h = _rms(x, g1_ref[...])
    p = _dot(h.astype(BF16), win_ref[...])
    o1 = D_POOL
    o2 = o1 + D_GMLP
    o3 = o2 + D_GMLP
    o4 = o3 + D_CONV
    u = p[:, o1:o2]
    v = p[:, o2:o3]
    cv = p[:, o3:o4]
    cg = p[:, o4:]

    H = POOL_HALO
    n = tm + H - POOL_PAD
    abuf[H:H + tm, :] = p[:, :o1]
    s2buf[POOL_PAD:POOL_PAD + n, :] = abuf[POOL_PAD:POOL_PAD + n, :] + abuf[POOL_PAD - 1:POOL_PAD - 1 + n, :]
    s4buf[POOL_PAD:POOL_PAD + n, :] = s2buf[POOL_PAD:POOL_PAD + n, :] + s2buf[POOL_PAD - 2:POOL_PAD - 2 + n, :]
    s8buf[POOL_PAD:POOL_PAD + n, :] = (s4buf[POOL_PAD:POOL_PAD + n, 128:]
                                       + s4buf[POOL_PAD - 4:POOL_PAD - 4 + n, 128:])
    s16 = s8buf[H:H + tm, :] + s8buf[H - 8:H - 8 + tm, :]
    lane = lax.broadcasted_iota(jnp.int32, (1, 128), 1)
    row = lax.broadcasted_iota(jnp.int32, (tm, 1), 0)
    tpos = (s * tm + row + 1).astype(F32)
    first = lane < POOL_GROUP
    ys = []
    for lo, (w_small, w_big), s_small, s_big in (
            (0, POOL_WINDOWS[0:2], s2buf[H:H + tm, 0:128], s4buf[H:H + tm, 0:128]),
            (128, POOL_WINDOWS[2:4], s8buf[H:H + tm, :], s16)):
        ssel = jnp.where(first, s_small, s_big)
        win = jnp.where(first, float(w_small), float(w_big))
        mean = ssel / jnp.minimum(tpos, win)
        ys.append(mean - abuf[H:H + tm, lo:lo + 128])
    y = jnp.concatenate(ys, axis=1).astype(BF16)
    ya = _dot(y, poolw_ref[...]) * pools_ref[...]
    ycat[:, 0:o1] = ya.astype(BF16)
    abuf[POOL_PAD:H, :] = abuf[tm + POOL_PAD:tm + H, :]

    vn = _rms(v, gmg_ref[...])
    ri = lax.broadcasted_iota(jnp.int32, (GMLP_BLOCK, GMLP_HEADS * GMLP_BLOCK), 0)
    cj = lax.broadcasted_iota(jnp.int32, (GMLP_BLOCK, GMLP_HEADS * GMLP_BLOCK), 1)
    causal = (ri // CHUNK) >= ((cj % GMLP_BLOCK) // CHUNK)
    wcat = jnp.where(causal, gmws_ref[...], 0.0).astype(BF16)
    hid = lax.broadcasted_iota(jnp.int32, (1, D_GMLP), 1) // GMLP_HEAD_DIM
    gmb = gmb_ref[...]
    for blk in range(tm // GMLP_BLOCK):
        r0 = blk * GMLP_BLOCK
        vb = vn[r0:r0 + GMLP_BLOCK, :]
        vstack = jnp.concatenate(
            [jnp.where(hid == hh, vb, 0.0) for hh in range(GMLP_HEADS)],
            axis=0).astype(BF16)
        z = _dot(wcat, vstack) + gmb
        ycat[r0:r0 + GMLP_BLOCK, o1:o2] = (u[r0:r0 + GMLP_BLOCK, :] * z).astype(BF16)

    hbuf[CONV_HALO:CONV_HALO + tm, :] = cv * jax.nn.sigmoid(cg)
    base = CONV_HALO - (CONV_WIDTH - 1)
    dwb = dwb_ref[...]
    lng = lng_ref[...]
    lnb = lnb_ref[...]
    pwb = pwb_ref[...]
    for b in range(8):
        n_rows = tm + 8 * (len(range(b, CONV_WIDTH, 8)) - 1)
        hshift[b, 0:n_rows, :] = hbuf[base + b:base + b + n_rows, :]
    for c0 in range(0, tm, CONV_ROWS):
        acc = None
        for j in range(CONV_WIDTH):
            r0 = c0 + 8 * (j // 8)
            win = hshift[j % 8, r0:r0 + CONV_ROWS, :].reshape(CONV_ROWS // 8, 8, D_CONV)
            term = wtap[j] * win
            acc = term if acc is None else acc + term
        acc = acc.reshape(CONV_ROWS, D_CONV) + dwb
        mu = jnp.mean(acc, axis=-1, keepdims=True)
        cen = acc - mu
        var = jnp.mean(cen * cen, axis=-1, keepdims=True)
        ln = cen * lax.rsqrt(var + EPS) * lng + lnb
        yc = _dot(_silu(ln).astype(BF16), pww_ref[...]) + pwb
        ycat[c0:c0 + CONV_ROWS, o2:D_MODEL] = yc.astype(BF16)
    hbuf[0:CONV_HALO, :] = hbuf[tm:tm + CONV_HALO, :]

    o_ref[0] = x + _dot(ycat[...], wout_ref[...])

    if route:
        for r in range(tm // ROUTE_T):
            rows = slice(r * ROUTE_T, (r + 1) * ROUTE_T)
            h_hi, info, padc = _route_tile(o_ref[0, rows, :], g2_ref[...], router_ref[...])
            h_ref[rows, :] = h_hi
            info_ref[rows, :] = info
            cnt_ref[r] = padc


def _const_spec(shape):
    nd = len(shape)
    return pl.BlockSpec(shape, lambda b, s: (0,) * nd, pipeline_mode=pl.Buffered(1))


def _mixer(x, g1, w_in, pool_bd, pool_scale, gm_g, gm_wcat, gm_bias, dw_w, dw_b,
           ln_g, ln_b, pw_w, pw_b, w_out, route_params=None):
    B, S, D = x.shape
    tm = MIX_TM
    tps = S // tm
    weights = (g1, w_in, pool_bd, pool_scale, gm_g, gm_wcat, gm_bias, dw_w, dw_b,
               ln_g, ln_b, pw_w, pw_b, w_out)
    out_shape = jax.ShapeDtypeStruct((B, S, D), F32)
    out_specs = pl.BlockSpec((1, tm, D), lambda b, s: (b, s, 0))
    if route_params is not None:
        weights += tuple(route_params)
        per_step = tm // ROUTE_T
        out_shape = (out_shape,
                     jax.ShapeDtypeStruct((B * S, D), BF16),
                     jax.ShapeDtypeStruct((B * S, E_LANES), F32),
                     jax.ShapeDtypeStruct((B * S // ROUTE_T, 1, E_LANES), F32))
        out_specs = (out_specs,
                     pl.BlockSpec((tm, D), lambda b, s: (b * tps + s, 0)),
                     pl.BlockSpec((tm, E_LANES), lambda b, s: (b * tps + s, 0)),
                     pl.BlockSpec((per_step, 1, E_LANES), lambda b, s: (b * tps + s, 0, 0)))
    return pl.pallas_call(
        functools.partial(_mixer_kernel, route=route_params is not None),
        out_shape=out_shape,
        grid=(B, S // tm),
        in_specs=[pl.BlockSpec((1, tm, D), lambda b, s: (b, s, 0))]
        + [_const_spec(w.shape) for w in weights],
        out_specs=out_specs,
        scratch_shapes=[
            pltpu.VMEM((POOL_HALO + tm, D_POOL), F32),
            pltpu.VMEM((POOL_HALO + tm, D_POOL), F32),
            pltpu.VMEM((POOL_HALO + tm, D_POOL), F32),
            pltpu.VMEM((POOL_HALO + tm, 128), F32),
            pltpu.VMEM((CONV_HALO + tm, D_CONV), F32),
            pltpu.VMEM((8, tm + CONV_HALO - 8, D_CONV), F32),
            pltpu.VMEM((CONV_WIDTH, 8, D_CONV), F32),
            pltpu.VMEM((tm, D_MODEL), BF16),
        ],
        compiler_params=pltpu.CompilerParams(
            dimension_semantics=("arbitrary", "arbitrary"),
            vmem_limit_bytes=V7X_VMEM_LIMIT_BYTES),
        name="token_mixer",
    )(x, *weights)


WEIGHT_STAGE_CHUNKS = 16
WEIGHT_STAGE_SLOTS = 4


def _load_bf16(w_hbm, w_bf, stage, sem):
    rows = stage.shape[1]

    def piece(c):
        slot = c % WEIGHT_STAGE_SLOTS
        return pltpu.make_async_copy(w_hbm.at[pl.ds(c * rows, rows)], stage.at[slot], sem.at[slot])

    for c in range(WEIGHT_STAGE_SLOTS - 1):
        piece(c).start()
    for c in range(WEIGHT_STAGE_CHUNKS):
        ahead = c + WEIGHT_STAGE_SLOTS - 1
        if ahead < WEIGHT_STAGE_CHUNKS:
            piece(ahead).start()
        piece(c).wait()
        w_bf[c * rows:(c + 1) * rows, :] = stage[c % WEIGHT_STAGE_SLOTS].astype(BF16)


def _ffn_kernel(x_ref, g_ref, wg_hbm, wu_hbm, wd_hbm, o_ref, wg, wu, wd, stage_in, stage_out, sem):
    @pl.when(pl.program_id(0) == 0)
    def _():
        _load_bf16(wg_hbm, wg, stage_in, sem)
        _load_bf16(wu_hbm, wu, stage_in, sem)
        _load_bf16(wd_hbm, wd, stage_out, sem)

    x = x_ref[...]
    h = _rms(x, g_ref[...]).astype(BF16)
    gate = _dot(h, wg[...])
    up = _dot(h, wu[...])
    act = (_silu(gate) * up).astype(BF16)
    o_ref[...] = x + _dot(act, wd[...])


def _ffn(x, g, wg, wu, wd):
    N, D = x.shape
    F = wg.shape[1]
    tm = FFN_TM
    hbm = pl.BlockSpec(memory_space=pl.ANY)
    return pl.pallas_call(
        _ffn_kernel,
        out_shape=jax.ShapeDtypeStruct((N, D), F32),
        grid=(N // tm,),
        in_specs=[pl.BlockSpec((tm, D), lambda i: (i, 0)),
                  pl.BlockSpec(g.shape, lambda i: (0, 0)), hbm, hbm, hbm],
        out_specs=pl.BlockSpec((tm, D), lambda i: (i, 0)),
        scratch_shapes=[pltpu.VMEM((D, F), BF16), pltpu.VMEM((D, F), BF16), pltpu.VMEM((F, D), BF16),
                        pltpu.VMEM((WEIGHT_STAGE_SLOTS, D // WEIGHT_STAGE_CHUNKS, F), F32),
                        pltpu.VMEM((WEIGHT_STAGE_SLOTS, F // WEIGHT_STAGE_CHUNKS, D), F32),
                        pltpu.SemaphoreType.DMA((WEIGHT_STAGE_SLOTS,))],
        compiler_params=pltpu.CompilerParams(
            dimension_semantics=("arbitrary",),
            vmem_limit_bytes=V7X_VMEM_LIMIT_BYTES),
        name="dense_swiglu",
    )(x, g, wg, wu, wd)


E_LANES = 128
RUN_ALIGN = 8
SLOT_ROWS = 2 * ROUTE_T + RUN_ALIGN * N_EXPERTS
RUN_BITS = tuple(range((ROUTE_T // RUN_ALIGN).bit_length() - 1, -1, -1))
TAIL_BITS = tuple(range((EXPERT_TM // RUN_ALIGN).bit_length() - 2, -1, -1))
SLOT_BITS = tuple(range((SLOT_ROWS // RUN_ALIGN).bit_length() - 1, -1, -1))


def _route_tile(x, g, r):
    T = x.shape[0]
    h = _rms(x, g)
    h_hi = h.astype(BF16)
    h_lo = (h - h_hi.astype(F32)).astype(BF16)
    r_hi = r.astype(BF16)
    r_lo = (r - r_hi.astype(F32)).astype(BF16)
    hi_both = _dot(h_hi, jnp.concatenate([r_hi, r_lo], axis=1))
    logits = hi_both[:, :E_LANES] + (_dot(h_lo, r_hi) + hi_both[:, E_LANES:])
    lane = lax.broadcasted_iota(jnp.int32, (T, E_LANES), 1)
    logits = jnp.where(lane < N_EXPERTS, logits, -jnp.inf)
    m1 = jnp.max(logits, axis=-1, keepdims=True)
    i1 = jnp.min(jnp.where(logits == m1, lane, E_LANES), axis=-1, keepdims=True)
    rest = jnp.where(lane == i1, -jnp.inf, logits)
    m2 = jnp.max(rest, axis=-1, keepdims=True)
    i2 = jnp.min(jnp.where(rest == m2, lane, E_LANES), axis=-1, keepdims=True)
    e2 = jnp.exp(m2 - m1)
    w1 = 1.0 / (1.0 + e2)
    w2 = e2 / (1.0 + e2)
    oh1 = (lane == i1).astype(F32)
    oh2 = (lane == i2).astype(F32)
    oh = oh1 + oh2
    ri = lax.broadcasted_iota(jnp.int32, (T, T), 0)
    ci = lax.broadcasted_iota(jnp.int32, (T, T), 1)
    rank = _dot((ri > ci).astype(BF16), oh.astype(BF16))
    cnt = jnp.sum(oh, axis=0, keepdims=True)
    padc = jnp.floor((cnt + (RUN_ALIGN - 1.0)) * (1.0 / RUN_ALIGN)) * RUN_ALIGN
    lane1 = lax.broadcasted_iota(jnp.int32, (1, E_LANES), 1)
    start = jnp.zeros((1, E_LANES), F32)
    for e in range(N_EXPERTS - 1):
        start = start + jnp.where(lane1 > e, padc[:, e:e + 1], 0.0)
    slot = start + rank
    q1 = jnp.sum(oh1 * slot, axis=-1, keepdims=True)
    q2 = jnp.sum(oh2 * slot, axis=-1, keepdims=True)
    info = jnp.where(lane == 0, q1, jnp.where(lane == 1, q2, jnp.where(
        lane == 2, w1, jnp.where(lane == 3, w2, 0.0))))
    return h_hi, info, padc


def _run_copies(src, dst, src0, dst0, rows, bits, sem):
    m = rows // RUN_ALIGN
    off = 0
    out = []
    for b in bits:
        size = RUN_ALIGN << b
        take = (m >> b) & 1
        s0 = 0 if src0 is None else pl.multiple_of(src0 + off, RUN_ALIGN)
        d0 = pl.multiple_of(dst0 + off, RUN_ALIGN)
        out.append((take == 1, pltpu.make_async_copy(
            src.at[pl.ds(s0, size)], dst.at[pl.ds(d0, size)], sem)))
        off = off + take * size
    return out


def _start(copies, enable=True):
    for pred, cp in copies:
        @pl.when(pred & enable)
        def _(cp=cp):
            cp.start()


def _wait(copies, enable=True):
    for pred, cp in copies:
        @pl.when(pred & enable)
        def _(cp=cp):
            cp.wait()


def _tile_runs(tile, table_refs, hbm, buf, sem, *, to_hbm):
    gdst_ref, glen_ref, gsrc_ref = table_refs
    copies = []
    for e in range(N_EXPERTS):
        j = tile * N_EXPERTS + e
        if to_hbm:
            copies += _run_copies(buf, hbm, gsrc_ref[j], gdst_ref[j], glen_ref[j], RUN_BITS, sem)
        else:
            copies += _run_copies(hbm, buf, gdst_ref[j], gsrc_ref[j], glen_ref[j], RUN_BITS, sem)
    return copies


def _tile_rows(tile, table_refs):
    _, glen_ref, gsrc_ref = table_refs
    last = tile * N_EXPERTS + N_EXPERTS - 1
    return gsrc_ref[last] + glen_ref[last]


def _wait_rows(rows, src, dst, sem):
    for pred, cp in _run_copies(src, dst, jnp.int32(0), jnp.int32(0), rows, SLOT_BITS, sem):
        @pl.when(pred)
        def _(cp=cp):
            cp.wait()


def _slots(info, T):
    q1 = info[:, 0:1].astype(jnp.int32)
    q2 = info[:, 1:2].astype(jnp.int32)
    r = lax.broadcasted_iota(jnp.int32, (T, SLOT_ROWS), 1)
    return q1 == r, q2 == r


def _dispatch_kernel(gdst_ref, glen_ref, gsrc_ref, tail0_ref, tailn_ref,
                     h_ref, info_ref, xs_hbm, xbuf, zbuf, sem, *, slack_pieces):
    t = pl.program_id(0)
    n_t = pl.num_programs(0)
    T = h_ref.shape[0]
    slot = t % 2
    tables = (gdst_ref, glen_ref, gsrc_ref)
    hit1, hit2 = _slots(info_ref[...], T)
    onehot = (hit1 | hit2).astype(BF16)
    xbuf[slot] = lax.dot_general(onehot, h_ref[...], (((0,), (0,)), ((), ())),
                                 preferred_element_type=F32)
    _start(_tile_runs(t, tables, xs_hbm, xbuf.at[slot], sem.at[slot], to_hbm=True))

    @pl.when(t > 0)
    def _():
        _wait_rows(_tile_rows(t - 1, tables), xbuf.at[1 - slot], xs_hbm, sem.at[1 - slot])

    @pl.when(t == n_t - 1)
    def _():
        _wait_rows(_tile_rows(t, tables), xbuf.at[slot], xs_hbm, sem.at[slot])

    @pl.when(t == 0)
    def _():
        zbuf[...] = jnp.zeros_like(zbuf)
        tails = []
        for e in range(N_EXPERTS):
            tails += _run_copies(zbuf, xs_hbm, None, tail0_ref[e], tailn_ref[e], TAIL_BITS,
                                 sem.at[2])
        piece = zbuf.shape[0]
        for c in range(slack_pieces):
            d0 = pl.multiple_of(tail0_ref[N_EXPERTS] + c * piece, RUN_ALIGN)
            tails.append((c < tailn_ref[N_EXPERTS], pltpu.make_async_copy(
                zbuf, xs_hbm.at[pl.ds(d0, piece)], sem.at[2])))
        _start(tails)
        _wait(tails)


def _dispatch(h, info, gdst, glen, gsrc, tail0, tailn, rows_max):
    N, D = h.shape
    T = ROUTE_T
    slack_pieces = (rows_max - 2 * N) // (EXPERT_TM // 2)
    return pl.pallas_call(
        functools.partial(_dispatch_kernel, slack_pieces=slack_pieces),
        out_shape=jax.ShapeDtypeStruct((rows_max, D), F32),
        grid_spec=pltpu.PrefetchScalarGridSpec(
            num_scalar_prefetch=5,
            grid=(N // T,),
            in_specs=[pl.BlockSpec((T, D), lambda t, *_: (t, 0)),
                      pl.BlockSpec((T, E_LANES), lambda t, *_: (t, 0))],
            out_specs=pl.BlockSpec(memory_space=pl.ANY),
            scratch_shapes=[pltpu.VMEM((2, SLOT_ROWS, D), F32),
                            pltpu.VMEM((EXPERT_TM // 2, D), F32),
                            pltpu.SemaphoreType.DMA((3,))]),
        compiler_params=pltpu.CompilerParams(
            dimension_semantics=("arbitrary",),
            vmem_limit_bytes=V7X_VMEM_LIMIT_BYTES),
        name="moe_dispatch",
    )(gdst, glen, gsrc, tail0, tailn, h, info)


def _experts_kernel(te_ref, nv_ref, ns_ref, x_ref, wg_ref, wu_ref, wd_ref, o_ref,
                    xb):
    i = pl.program_id(0)
    k = pl.program_id(1)
    n_sub = ns_ref[i]

    full = x_ref.shape[0] // EXPERT_SUB

    @pl.when((n_sub > 0) & (k == 0))
    def _():
        xb[...] = x_ref[...].astype(BF16)
        o_ref[...] = jnp.zeros_like(o_ref)

    @pl.when(n_sub == full)
    def _():
        h = xb[...]
        act = (_silu(_dot(h, wg_ref[0].astype(BF16))) * _dot(h, wu_ref[0].astype(BF16))).astype(BF16)
        o_ref[...] += _dot(act, wd_ref[0].astype(BF16))

    @pl.when((n_sub > 0) & (n_sub < full))
    def _():
        def piece(s, carry):
            rows = pl.ds(pl.multiple_of(s * EXPERT_SUB, EXPERT_SUB), EXPERT_SUB)
            h = xb[rows, :]
            act = (_silu(_dot(h, wg_ref[0].astype(BF16)))
                   * _dot(h, wu_ref[0].astype(BF16))).astype(BF16)
            o_ref[rows, :] += _dot(act, wd_ref[0].astype(BF16))
            return carry

        lax.fori_loop(0, n_sub, piece, 0)

    @pl.when((n_sub == 0) & (k == 0))
    def _():
        o_ref[...] = jnp.zeros_like(o_ref)


def _experts(xs, tile_expert, n_valid, n_sub, wg, wu, wd):
    M, D = xs.shape
    tm, fk = EXPERT_TM, EXPERT_FK
    E, _, F = wg.shape
    nk = F // fk

    def row_map(i, k, te, nv, ns):
        return (jnp.maximum(jnp.minimum(i, nv[0] - 1), 0), 0)

    def kk(i, k, nv):
        return jnp.where(i < nv[0], k, nk - 1)

    return pl.pallas_call(
        _experts_kernel,
        out_shape=jax.ShapeDtypeStruct((M, D), F32),
        grid_spec=pltpu.PrefetchScalarGridSpec(
            num_scalar_prefetch=3,
            grid=(M // tm, nk),
            in_specs=[
                pl.BlockSpec((tm, D), row_map),
                pl.BlockSpec((1, D, fk), lambda i, k, te, nv, ns: (te[i], 0, kk(i, k, nv))),
                pl.BlockSpec((1, D, fk), lambda i, k, te, nv, ns: (te[i], 0, kk(i, k, nv))),
                pl.BlockSpec((1, fk, D), lambda i, k, te, nv, ns: (te[i], kk(i, k, nv), 0)),
            ],
            out_specs=pl.BlockSpec((tm, D), lambda i, k, te, nv, ns: (i, 0)),
            scratch_shapes=[pltpu.VMEM((tm, D), BF16)]),
        compiler_params=pltpu.CompilerParams(
            dimension_semantics=("arbitrary", "arbitrary"),
            vmem_limit_bytes=V7X_VMEM_LIMIT_BYTES),
        name="moe_experts",
    )(tile_expert, n_valid, n_sub, xs, wg, wu, wd)


def _combine_kernel(gdst_ref, glen_ref, gsrc_ref, x_ref, info_ref, gf_ref, ys_hbm, o_ref,
                    ybuf, sem):
    t = pl.program_id(0)
    n_t = pl.num_programs(0)
    T = x_ref.shape[0]
    slot = t % 2
    tables = (gdst_ref, glen_ref, gsrc_ref)
    def fetch(tile, into):
        ybuf[into, 2 * T:SLOT_ROWS, :] = jnp.zeros((SLOT_ROWS - 2 * T, ybuf.shape[2]), F32)
        _start(_tile_runs(tile, tables, ys_hbm, ybuf.at[into], sem.at[into], to_hbm=False))

    @pl.when(t == 0)
    def _():
        fetch(t, slot)

    @pl.when(t + 1 < n_t)
    def _():
        fetch(t + 1, 1 - slot)

    _wait_rows(_tile_rows(t, tables), ys_hbm, ybuf.at[slot], sem.at[slot])
    y = ybuf[slot].astype(BF16)
    info = info_ref[...]
    hit1, hit2 = _slots(info, T)
    weights = (jnp.where(hit1, info[:, 2:3], 0.0) + jnp.where(hit2, info[:, 3:4], 0.0)).astype(BF16)
    o_ref[...] = _rms(x_ref[...] + _dot(weights, y), gf_ref[...])


def _combine(x, info, gf, ys, gdst, glen, gsrc):
    N, D = x.shape
    T = ROUTE_T
    return pl.pallas_call(
        _combine_kernel,
        out_shape=jax.ShapeDtypeStruct((N, D), F32),
        grid_spec=pltpu.PrefetchScalarGridSpec(
            num_scalar_prefetch=3,
            grid=(N // T,),
            in_specs=[pl.BlockSpec((T, D), lambda t, *_: (t, 0)),
                      pl.BlockSpec((T, E_LANES), lambda t, *_: (t, 0)),
                      pl.BlockSpec(gf.shape, lambda t, *_: (0, 0)),
                      pl.BlockSpec(memory_space=pl.ANY)],
            out_specs=pl.BlockSpec((T, D), lambda t, *_: (t, 0)),
            scratch_shapes=[pltpu.VMEM((2, SLOT_ROWS, D), F32),
                            pltpu.SemaphoreType.DMA((2,))]),
        compiler_params=pltpu.CompilerParams(
            dimension_semantics=("arbitrary",),
            vmem_limit_bytes=V7X_VMEM_LIMIT_BYTES),
        name="moe_combine",
    )(gdst, glen, gsrc, x, info, gf, ys)


def _moe(x, h, info, cnt, wg, wu, wd, gf):
    N, D = x.shape
    n_t = N // ROUTE_T
    tm = EXPERT_TM
    glen = cnt[:, 0, :N_EXPERTS].astype(jnp.int32)
    total = jnp.sum(glen, axis=0)
    gpad = (total + tm - 1) // tm * tm
    gend = jnp.cumsum(gpad)
    goff = gend - gpad
    gdst = goff[None, :] + jnp.cumsum(glen, axis=0) - glen
    gsrc = jnp.cumsum(glen, axis=1) - glen
    rows_max = -(-(2 * N + n_t * N_EXPERTS * (RUN_ALIGN - 1) + N_EXPERTS * (tm - RUN_ALIGN)) // tm) * tm
    n_tiles = rows_max // tm
    n_valid = (gend[-1] // tm).reshape(1)
    tile_row = jnp.minimum(jnp.arange(n_tiles, dtype=jnp.int32), n_valid[0] - 1) * tm
    tile_expert = jnp.sum((tile_row[:, None] >= gend[None, :]).astype(jnp.int32), axis=1)
    tile_rows = jnp.clip((goff + total)[tile_expert] - tile_row, 0, tm)
    tile_rows = jnp.where(jnp.arange(n_tiles) < n_valid[0], tile_rows, 0)
    n_sub = (tile_rows + EXPERT_SUB - 1) // EXPERT_SUB
    flat = lambda a: a.reshape(-1).astype(jnp.int32)
    tail0 = jnp.concatenate([goff + total, gend[-1:]])
    tailn = jnp.concatenate([gpad - total, (rows_max - gend[-1:]) // (tm // 2)])
    xs = _dispatch(h, info, flat(gdst), flat(glen), flat(gsrc), flat(tail0), flat(tailn), rows_max)
    ys = _experts(xs, flat(tile_expert), flat(n_valid), flat(n_sub), wg, wu, wd)
    return _combine(x, info, gf, ys, flat(gdst), flat(glen), flat(gsrc))


def kernel(x, norm1_g, w_in, pool_w, pool_scale, gm_norm_g, gm_ws, gm_b,
           conv_dw_w, conv_dw_b, conv_ln_g, conv_ln_b, conv_pw_w, conv_pw_b,
           w_out, norm2_g, ffn_wg, ffn_wu, ffn_wd,
           moe_router, moe_wg, moe_wu, moe_wd, final_g):
    B, S, D = x.shape
    depth = w_in.shape[0]
    assert depth == 2, "layer 0 is the dense SwiGLU layer, layer 1 the expert layer + final norm"
    row = lambda t: t.reshape(1, -1)
    for l in range(depth):
        pool_bd = jax.scipy.linalg.block_diag(*[pool_w[l, gi] for gi in range(len(POOL_WINDOWS))])
        gm_wcat = jnp.transpose(gm_ws[l], (1, 0, 2)).reshape(GMLP_BLOCK, GMLP_HEADS * GMLP_BLOCK)
        gm_bias = jnp.repeat(gm_b[l].T, GMLP_HEAD_DIM, axis=1)
        j = l // 2
        expert_layer = l % 2 == 1
        route_params = None
        if expert_layer:
            router_p = jnp.pad(moe_router[j], ((0, 0), (0, E_LANES - N_EXPERTS)))
            route_params = (row(norm2_g[l]), router_p)
        mixed = _mixer(x, row(norm1_g[l]), w_in[l].astype(BF16), pool_bd.astype(BF16),
                       row(pool_scale[l]), row(gm_norm_g[l]), gm_wcat, gm_bias,
                       conv_dw_w[l], row(conv_dw_b[l]), row(conv_ln_g[l]), row(conv_ln_b[l]),
                       conv_pw_w[l].astype(BF16), row(conv_pw_b[l]), w_out[l].astype(BF16),
                       route_params=route_params)
        if expert_layer:
            x, h, info, cnt = mixed
            xf = _moe(x.reshape(B * S, D), h, info, cnt, moe_wg[j], moe_wu[j], moe_wd[j],
                      row(final_g))
        else:
            xf = _ffn(mixed.reshape(B * S, D), row(norm2_g[l]), ffn_wg[j], ffn_wu[j], ffn_wd[j])
        x = xf.reshape(B, S, D)
    return x
```

```python
import functools

import jax
import jax.numpy as jnp
from jax import lax
from jax.experimental import pallas as pl
from jax.experimental.pallas import tpu as pltpu

D_MODEL = 1024
CHUNK = 64
D_POOL = 256
POOL_WINDOWS = (2, 4, 8, 16)
POOL_GROUP = 64
D_GMLP = 384
GMLP_HEADS = 4
GMLP_HEAD_DIM = 96
GMLP_BLOCK = 128
D_CONV = 384
CONV_WIDTH = 31
D_IN = D_POOL + 2 * D_GMLP + 2 * D_CONV
D_FF = 2816
N_EXPERTS = 8
D_FF_EXPERT = 3584
EPS = 1e-6

V7X_VMEM_LIMIT_BYTES = 56 * 1024 * 1024

POOL_PAD = 8
POOL_HALO = POOL_PAD + 16
CONV_HALO = 32
MIX_TM = 1024
CONV_ROWS = 256
FFN_TM = 1024
ROUTE_T = 512
EXPERT_TM = 1024
EXPERT_FK = 512
EXPERT_SUB = 256

BF16 = jnp.bfloat16
F32 = jnp.float32


def _rms(x, g):
    return x * lax.rsqrt(jnp.mean(x * x, axis=-1, keepdims=True) + EPS) * g


def _silu(x):
    return x * jax.nn.sigmoid(x)


def _dot(a, b):
    return jnp.dot(a, b, preferred_element_type=F32)


def _mixer_kernel(x_ref, g1_ref, win_hbm, poolw_ref, pools_ref, gmg_ref,
                  gmws_ref, gmb_ref, dww_ref, dwb_ref, lng_ref, lnb_ref,
                  pww_hbm, pwb_ref, wout_hbm, *rest, layer, route):
    if route:
        g2_ref, router_ref, o_ref, h_ref, info_ref, cnt_ref = rest[:6]
        rest = rest[6:]
    else:
        o_ref = rest[0]
        rest = rest[1:]
    (abuf, s2buf, s4buf, s8buf, hbuf, hshift, wtap, ycat,
     win_ref, pww_ref, wout_ref, stage_in, stage_pw, stage_out, wsem) = rest
    tm = x_ref.shape[1]
    s = pl.program_id(1)

    @pl.when((pl.program_id(0) == 0) & (s == 0))
    def _():
        _load_bf16(win_hbm.at[layer], win_ref, stage_in, wsem)
        _load_bf16(pww_hbm.at[layer], pww_ref, stage_pw, wsem)
        _load_bf16(wout_hbm.at[layer], wout_ref, stage_out, wsem)

    @pl.when(s == 0)
    def _():
        abuf[0:POOL_HALO, :] = jnp.zeros((POOL_HALO, D_POOL), F32)
        s2buf[0:POOL_PAD, :] = jnp.zeros((POOL_PAD, D_POOL), F32)
        s4buf[0:POOL_PAD, :] = jnp.zeros((POOL_PAD, D_POOL), F32)
        s8buf[0:POOL_PAD, :] = jnp.zeros((POOL_PAD, 128), F32)
        hbuf[0:CONV_HALO, :] = jnp.zeros((CONV_HALO, D_CONV), F32)
        for j in range(CONV_WIDTH):
            wtap[j] = jnp.broadcast_to(dww_ref[j:j + 1, :], (8, D_CONV))

    x = x_ref[0]
    h = _rms(x, g1_ref[...])
    p = _dot(h.astype(BF16), win_ref[...])
    o1 = D_POOL
    o2 = o1 + D_GMLP
    o3 = o2 + D_GMLP
    o4 = o3 + D_CONV
    u = p[:, o1:o2]
    v = p[:, o2:o3]
    cv = p[:, o3:o4]
    cg = p[:, o4:]

    H = POOL_HALO
    n = tm + H - POOL_PAD
    abuf[H:H + tm, :] = p[:, :o1]
    s2buf[POOL_PAD:POOL_PAD + n, :] = abuf[POOL_PAD:POOL_PAD + n, :] + abuf[POOL_PAD - 1:POOL_PAD - 1 + n, :]
    s4buf[POOL_PAD:POOL_PAD + n, :] = s2buf[POOL_PAD:POOL_PAD + n, :] + s2buf[POOL_PAD - 2:POOL_PAD - 2 + n, :]
    s8buf[POOL_PAD:POOL_PAD + n, :] = (s4buf[POOL_PAD:POOL_PAD + n, 128:]
                                       + s4buf[POOL_PAD - 4:POOL_PAD - 4 + n, 128:])
    s16 = s8buf[H:H + tm, :] + s8buf[H - 8:H - 8 + tm, :]
    lane = lax.broadcasted_iota(jnp.int32, (1, 128), 1)
    row = lax.broadcasted_iota(jnp.int32, (tm, 1), 0)
    tpos = (s * tm + row + 1).astype(F32)
    first = lane < POOL_GROUP
    ys = []
    for lo, (w_small, w_big), s_small, s_big in (
            (0, POOL_WINDOWS[0:2], s2buf[H:H + tm, 0:128], s4buf[H:H + tm, 0:128]),
            (128, POOL_WINDOWS[2:4], s8buf[H:H + tm, :], s16)):
        ssel = jnp.where(first, s_small, s_big)
        win = jnp.where(first, float(w_small), float(w_big))
        mean = ssel / jnp.minimum(tpos, win)
        ys.append(mean - abuf[H:H + tm, lo:lo + 128])
    y = jnp.concatenate(ys, axis=1).astype(BF16)
    ya = _dot(y, poolw_ref[...]) * pools_ref[...]
    ycat[:, 0:o1] = ya.astype(BF16)
    abuf[POOL_PAD:H, :] = abuf[tm + POOL_PAD:tm + H, :]

    vn = _rms(v, gmg_ref[...])
    ri = lax.broadcasted_iota(jnp.int32, (GMLP_BLOCK, GMLP_HEADS * GMLP_BLOCK), 0)
    cj = lax.broadcasted_iota(jnp.int32, (GMLP_BLOCK, GMLP_HEADS * GMLP_BLOCK), 1)
    causal = (ri // CHUNK) >= ((cj % GMLP_BLOCK) // CHUNK)
    wcat = jnp.where(causal, gmws_ref[...], 0.0).astype(BF16)
    hid = lax.broadcasted_iota(jnp.int32, (1, D_GMLP), 1) // GMLP_HEAD_DIM
    gmb = gmb_ref[...]
    for blk in range(tm // GMLP_BLOCK):
        r0 = blk * GMLP_BLOCK
        vb = vn[r0:r0 + GMLP_BLOCK, :]
        vstack = jnp.concatenate(
            [jnp.where(hid == hh, vb, 0.0) for hh in range(GMLP_HEADS)],
            axis=0).astype(BF16)
        z = _dot(wcat, vstack) + gmb
        ycat[r0:r0 + GMLP_BLOCK, o1:o2] = (u[r0:r0 + GMLP_BLOCK, :] * z).astype(BF16)

    hbuf[CONV_HALO:CONV_HALO + tm, :] = cv * jax.nn.sigmoid(cg)
    base = CONV_HALO - (CONV_WIDTH - 1)
    dwb = dwb_ref[...]
    lng = lng_ref[...]
    lnb = lnb_ref[...]
    pwb = pwb_ref[...]
    for b in range(8):
        n_rows = tm + 8 * (len(range(b, CONV_WIDTH, 8)) - 1)
        hshift[b, 0:n_rows, :] = hbuf[base + b:base + b + n_rows, :]
    for c0 in range(0, tm, CONV_ROWS):
        acc = None
        for j in range(CONV_WIDTH):
            r0 = c0 + 8 * (j // 8)
            win = hshift[j % 8, r0:r0 + CONV_ROWS, :].reshape(CONV_ROWS // 8, 8, D_CONV)
            term = wtap[j] * win
            acc = term if acc is None else acc + term
        acc = acc.reshape(CONV_ROWS, D_CONV) + dwb
        mu = jnp.mean(acc, axis=-1, keepdims=True)
        cen = acc - mu
        var = jnp.mean(cen * cen, axis=-1, keepdims=True)
        ln = cen * lax.rsqrt(var + EPS) * lng + lnb
        yc = _dot(_silu(ln).astype(BF16), pww_ref[...]) + pwb
        ycat[c0:c0 + CONV_ROWS, o2:D_MODEL] = yc.astype(BF16)
    hbuf[0:CONV_HALO, :] = hbuf[tm:tm + CONV_HALO, :]

    o_ref[0] = x + _dot(ycat[...], wout_ref[...])

    if route:
        for r in range(tm // ROUTE_T):
            rows = slice(r * ROUTE_T, (r + 1) * ROUTE_T)
            h_hi, info, padc = _route_tile(o_ref[0, rows, :], g2_ref[...], router_ref[...])
            h_ref[rows, :] = h_hi
            info_ref[rows, :] = info
            cnt_ref[r] = padc


def _const_spec(shape):
    nd = len(shape)
    return pl.BlockSpec(shape, lambda b, s: (0,) * nd, pipeline_mode=pl.Buffered(1))


def _mixer(x, layer, g1, w_in, pool_bd, pool_scale, gm_g, gm_wcat, gm_bias, dw_w, dw_b,
           ln_g, ln_b, pw_w, pw_b, w_out, route_params=None):
    B, S, D = x.shape
    tm = MIX_TM
    tps = S // tm
    weights = (g1, w_in, pool_bd, pool_scale, gm_g, gm_wcat, gm_bias, dw_w, dw_b,
               ln_g, ln_b, pw_w, pw_b, w_out)
    in_hbm = (1, 11, 13)
    out_shape = jax.ShapeDtypeStruct((B, S, D), F32)
    out_specs = pl.BlockSpec((1, tm, D), lambda b, s: (b, s, 0))
    if route_params is not None:
        weights += tuple(route_params)
        per_step = tm // ROUTE_T
        out_shape = (out_shape,
                     jax.ShapeDtypeStruct((B * S, D), BF16),
                     jax.ShapeDtypeStruct((B * S, E_LANES), F32),
                     jax.ShapeDtypeStruct((B * S // ROUTE_T, 1, E_LANES), F32))
        out_specs = (out_specs,
                     pl.BlockSpec((tm, D), lambda b, s: (b * tps + s, 0)),
                     pl.BlockSpec((tm, E_LANES), lambda b, s: (b * tps + s, 0)),
                     pl.BlockSpec((per_step, 1, E_LANES), lambda b, s: (b * tps + s, 0, 0)))
    stage = lambda rows, cols: pltpu.VMEM(
        (WEIGHT_STAGE_SLOTS, rows // WEIGHT_STAGE_CHUNKS, cols), F32)
    return pl.pallas_call(
        functools.partial(_mixer_kernel, layer=layer, route=route_params is not None),
        out_shape=out_shape,
        grid=(B, S // tm),
        in_specs=[pl.BlockSpec((1, tm, D), lambda b, s: (b, s, 0))]
        + [pl.BlockSpec(memory_space=pl.ANY) if i in in_hbm else _const_spec(w.shape)
           for i, w in enumerate(weights)],
        out_specs=out_specs,
        scratch_shapes=[
            pltpu.VMEM((POOL_HALO + tm, D_POOL), F32),
            pltpu.VMEM((POOL_HALO + tm, D_POOL), F32),
            pltpu.VMEM((POOL_HALO + tm, D_POOL), F32),
            pltpu.VMEM((POOL_HALO + tm, 128), F32),
            pltpu.VMEM((CONV_HALO + tm, D_CONV), F32),
            pltpu.VMEM((8, tm + CONV_HALO - 8, D_CONV), F32),
            pltpu.VMEM((CONV_WIDTH, 8, D_CONV), F32),
            pltpu.VMEM((tm, D_MODEL), BF16),
            pltpu.VMEM((D_MODEL, D_IN), BF16),
            pltpu.VMEM((D_CONV, D_CONV), BF16),
            pltpu.VMEM((D_MODEL, D_MODEL), BF16),
            stage(D_MODEL, D_IN), stage(D_CONV, D_CONV), stage(D_MODEL, D_MODEL),
            pltpu.SemaphoreType.DMA((WEIGHT_STAGE_SLOTS,)),
        ],
        compiler_params=pltpu.CompilerParams(
            dimension_semantics=("arbitrary", "arbitrary"),
            vmem_limit_bytes=V7X_VMEM_LIMIT_BYTES),
        name="token_mixer",
    )(x, *weights)


WEIGHT_STAGE_CHUNKS = 16
WEIGHT_STAGE_SLOTS = 4


def _load_bf16(w_hbm, w_bf, stage, sem):
    rows = stage.shape[1]

    def piece(c):
        slot = c % WEIGHT_STAGE_SLOTS
        return pltpu.make_async_copy(w_hbm.at[pl.ds(c * rows, rows)], stage.at[slot], sem.at[slot])

    for c in range(WEIGHT_STAGE_SLOTS - 1):
        piece(c).start()
    for c in range(WEIGHT_STAGE_CHUNKS):
        ahead = c + WEIGHT_STAGE_SLOTS - 1
        if ahead < WEIGHT_STAGE_CHUNKS:
            piece(ahead).start()
        piece(c).wait()
        w_bf[c * rows:(c + 1) * rows, :] = stage[c % WEIGHT_STAGE_SLOTS].astype(BF16)


def _ffn_kernel(x_ref, g_ref, wg_hbm, wu_hbm, wd_hbm, o_ref, wg, wu, wd, stage_in, stage_out, sem):
    @pl.when(pl.program_id(0) == 0)
    def _():
        _load_bf16(wg_hbm, wg, stage_in, sem)
        _load_bf16(wu_hbm, wu, stage_in, sem)
        _load_bf16(wd_hbm, wd, stage_out, sem)

    x = x_ref[...]
    h = _rms(x, g_ref[...]).astype(BF16)
    gate = _dot(h, wg[...])
    up = _dot(h, wu[...])
    act = (_silu(gate) * up).astype(BF16)
    o_ref[...] = x + _dot(act, wd[...])


def _ffn(x, g, wg, wu, wd):
    N, D = x.shape
    F = wg.shape[1]
    tm = FFN_TM
    hbm = pl.BlockSpec(memory_space=pl.ANY)
    return pl.pallas_call(
        _ffn_kernel,
        out_shape=jax.ShapeDtypeStruct((N, D), F32),
        grid=(N // tm,),
        in_specs=[pl.BlockSpec((tm, D), lambda i: (i, 0)),
                  pl.BlockSpec(g.shape, lambda i: (0, 0)), hbm, hbm, hbm],
        out_specs=pl.BlockSpec((tm, D), lambda i: (i, 0)),
        scratch_shapes=[pltpu.VMEM((D, F), BF16), pltpu.VMEM((D, F), BF16), pltpu.VMEM((F, D), BF16),
                        pltpu.VMEM((WEIGHT_STAGE_SLOTS, D // WEIGHT_STAGE_CHUNKS, F), F32),
                        pltpu.VMEM((WEIGHT_STAGE_SLOTS, F // WEIGHT_STAGE_CHUNKS, D), F32),
                        pltpu.SemaphoreType.DMA((WEIGHT_STAGE_SLOTS,))],
        compiler_params=pltpu.CompilerParams(
            dimension_semantics=("arbitrary",),
            vmem_limit_bytes=V7X_VMEM_LIMIT_BYTES),
        name="dense_swiglu",
    )(x, g, wg, wu, wd)


E_LANES = 128
RUN_ALIGN = 8
SLOT_ROWS = 2 * ROUTE_T + RUN_ALIGN * N_EXPERTS
RUN_BITS = tuple(range((ROUTE_T // RUN_ALIGN).bit_length() - 1, -1, -1))
TAIL_BITS = tuple(range((EXPERT_TM // RUN_ALIGN).bit_length() - 2, -1, -1))
SLOT_BITS = tuple(range((SLOT_ROWS // RUN_ALIGN).bit_length() - 1, -1, -1))


def _route_tile(x, g, r):
    T = x.shape[0]
    h = _rms(x, g)
    h_hi = h.astype(BF16)
    h_lo = (h - h_hi.astype(F32)).astype(BF16)
    r_hi = r.astype(BF16)
    r_lo = (r - r_hi.astype(F32)).astype(BF16)
    hi_both = _dot(h_hi, jnp.concatenate([r_hi, r_lo], axis=1))
    logits = hi_both[:, :E_LANES] + (_dot(h_lo, r_hi) + hi_both[:, E_LANES:])
    lane = lax.broadcasted_iota(jnp.int32, (T, E_LANES), 1)
    logits = jnp.where(lane < N_EXPERTS, logits, -jnp.inf)
    m1 = jnp.max(logits, axis=-1, keepdims=True)
    i1 = jnp.min(jnp.where(logits == m1, lane, E_LANES), axis=-1, keepdims=True)
    rest = jnp.where(lane == i1, -jnp.inf, logits)
    m2 = jnp.max(rest, axis=-1, keepdims=True)
    i2 = jnp.min(jnp.where(rest == m2, lane, E_LANES), axis=-1, keepdims=True)
    e2 = jnp.exp(m2 - m1)
    w1 = 1.0 / (1.0 + e2)
    w2 = e2 / (1.0 + e2)
    oh1 = (lane == i1).astype(F32)
    oh2 = (lane == i2).astype(F32)
    oh = oh1 + oh2
    ri = lax.broadcasted_iota(jnp.int32, (T, T), 0)
    ci = lax.broadcasted_iota(jnp.int32, (T, T), 1)
    rank = _dot((ri > ci).astype(BF16), oh.astype(BF16))
    cnt = jnp.sum(oh, axis=0, keepdims=True)
    padc = jnp.floor((cnt + (RUN_ALIGN - 1.0)) * (1.0 / RUN_ALIGN)) * RUN_ALIGN
    lane1 = lax.broadcasted_iota(jnp.int32, (1, E_LANES), 1)
    start = jnp.zeros((1, E_LANES), F32)
    for e in range(N_EXPERTS - 1):
        start = start + jnp.where(lane1 > e, padc[:, e:e + 1], 0.0)
    slot = start + rank
    q1 = jnp.sum(oh1 * slot, axis=-1, keepdims=True)
    q2 = jnp.sum(oh2 * slot, axis=-1, keepdims=True)
    info = jnp.where(lane == 0, q1, jnp.where(lane == 1, q2, jnp.where(
        lane == 2, w1, jnp.where(lane == 3, w2, 0.0))))
    return h_hi, info, padc


def _run_copies(src, dst, src0, dst0, rows, bits, sem):
    m = rows // RUN_ALIGN
    off = 0
    out = []
    for b in bits:
        size = RUN_ALIGN << b
        take = (m >> b) & 1
        s0 = 0 if src0 is None else pl.multiple_of(src0 + off, RUN_ALIGN)
        d0 = pl.multiple_of(dst0 + off, RUN_ALIGN)
        out.append((take == 1, pltpu.make_async_copy(
            src.at[pl.ds(s0, size)], dst.at[pl.ds(d0, size)], sem)))
        off = off + take * size
    return out


def _start(copies, enable=True):
    for pred, cp in copies:
        @pl.when(pred & enable)
        def _(cp=cp):
            cp.start()


def _wait(copies, enable=True):
    for pred, cp in copies:
        @pl.when(pred & enable)
        def _(cp=cp):
            cp.wait()


def _tile_runs(tile, table_refs, hbm, buf, sem, *, to_hbm):
    gdst_ref, glen_ref, gsrc_ref = table_refs
    copies = []
    for e in range(N_EXPERTS):
        j = tile * N_EXPERTS + e
        if to_hbm:
            copies += _run_copies(buf, hbm, gsrc_ref[j], gdst_ref[j], glen_ref[j], RUN_BITS, sem)
        else:
            copies += _run_copies(hbm, buf, gdst_ref[j], gsrc_ref[j], glen_ref[j], RUN_BITS, sem)
    return copies


def _tile_rows(tile, table_refs):
    _, glen_ref, gsrc_ref = table_refs
    last = tile * N_EXPERTS + N_EXPERTS - 1
    return gsrc_ref[last] + glen_ref[last]


def _wait_rows(rows, src, dst, sem):
    for pred, cp in _run_copies(src, dst, jnp.int32(0), jnp.int32(0), rows, SLOT_BITS, sem):
        @pl.when(pred)
        def _(cp=cp):
            cp.wait()


def _slots(info, T):
    q1 = info[:, 0:1].astype(jnp.int32)
    q2 = info[:, 1:2].astype(jnp.int32)
    r = lax.broadcasted_iota(jnp.int32, (T, SLOT_ROWS), 1)
    return q1 == r, q2 == r


def _dispatch_kernel(gdst_ref, glen_ref, gsrc_ref, tail0_ref, tailn_ref,
                     h_ref, info_ref, xs_hbm, xbuf, zbuf, sem, *, slack_pieces):
    t = pl.program_id(0)
    n_t = pl.num_programs(0)
    T = h_ref.shape[0]
    slot = t % 2
    tables = (gdst_ref, glen_ref, gsrc_ref)
    hit1, hit2 = _slots(info_ref[...], T)
    onehot = (hit1 | hit2).astype(BF16)
    xbuf[slot] = lax.dot_general(onehot, h_ref[...], (((0,), (0,)), ((), ())),
                                 preferred_element_type=F32)
    _start(_tile_runs(t, tables, xs_hbm, xbuf.at[slot], sem.at[slot], to_hbm=True))

    @pl.when(t > 0)
    def _():
        _wait_rows(_tile_rows(t - 1, tables), xbuf.at[1 - slot], xs_hbm, sem.at[1 - slot])

    @pl.when(t == n_t - 1)
    def _():
        _wait_rows(_tile_rows(t, tables), xbuf.at[slot], xs_hbm, sem.at[slot])

    @pl.when(t == 0)
    def _():
        zbuf[...] = jnp.zeros_like(zbuf)
        tails = []
        for e in range(N_EXPERTS):
            tails += _run_copies(zbuf, xs_hbm, None, tail0_ref[e], tailn_ref[e], TAIL_BITS,
                                 sem.at[2])
        piece = zbuf.shape[0]
        for c in range(slack_pieces):
            d0 = pl.multiple_of(tail0_ref[N_EXPERTS] + c * piece, RUN_ALIGN)
            tails.append((c < tailn_ref[N_EXPERTS], pltpu.make_async_copy(
                zbuf, xs_hbm.at[pl.ds(d0, piece)], sem.at[2])))
        _start(tails)
        _wait(tails)


def _dispatch(h, info, gdst, glen, gsrc, tail0, tailn, rows_max):
    N, D = h.shape
    T = ROUTE_T
    slack_pieces = (rows_max - 2 * N) // (EXPERT_TM // 2)
    return pl.pallas_call(
        functools.partial(_dispatch_kernel, slack_pieces=slack_pieces),
        out_shape=jax.ShapeDtypeStruct((rows_max, D), F32),
        grid_spec=pltpu.PrefetchScalarGridSpec(
            num_scalar_prefetch=5,
            grid=(N // T,),
            in_specs=[pl.BlockSpec((T, D), lambda t, *_: (t, 0)),
                      pl.BlockSpec((T, E_LANES), lambda t, *_: (t, 0))],
            out_specs=pl.BlockSpec(memory_space=pl.ANY),
            scratch_shapes=[pltpu.VMEM((2, SLOT_ROWS, D), F32),
                            pltpu.VMEM((EXPERT_TM // 2, D), F32),
                            pltpu.SemaphoreType.DMA((3,))]),
        compiler_params=pltpu.CompilerParams(
            dimension_semantics=("arbitrary",),
            vmem_limit_bytes=V7X_VMEM_LIMIT_BYTES),
        name="moe_dispatch",
    )(gdst, glen, gsrc, tail0, tailn, h, info)


def _experts_kernel(te_ref, nv_ref, ns_ref, x_ref, wg_ref, wu_ref, wd_ref, o_ref,
                    xb):
    i = pl.program_id(0)
    k = pl.program_id(1)
    n_sub = ns_ref[i]

    full = x_ref.shape[0] // EXPERT_SUB

    @pl.when((n_sub > 0) & (k == 0))
    def _():
        xb[...] = x_ref[...].astype(BF16)
        o_ref[...] = jnp.zeros_like(o_ref)

    @pl.when(n_sub == full)
    def _():
        h = xb[...]
        act = (_silu(_dot(h, wg_ref[0].astype(BF16))) * _dot(h, wu_ref[0].astype(BF16))).astype(BF16)
        o_ref[...] += _dot(act, wd_ref[0].astype(BF16))

    @pl.when((n_sub > 0) & (n_sub < full))
    def _():
        def piece(s, carry):
            rows = pl.ds(pl.multiple_of(s * EXPERT_SUB, EXPERT_SUB), EXPERT_SUB)
            h = xb[rows, :]
            act = (_silu(_dot(h, wg_ref[0].astype(BF16)))
                   * _dot(h, wu_ref[0].astype(BF16))).astype(BF16)
            o_ref[rows, :] += _dot(act, wd_ref[0].astype(BF16))
            return carry

        lax.fori_loop(0, n_sub, piece, 0)

    @pl.when((n_sub == 0) & (k == 0))
    def _():
        o_ref[...] = jnp.zeros_like(o_ref)


def _experts(xs, tile_expert, n_valid, n_sub, wg, wu, wd):
    M, D = xs.shape
    tm, fk = EXPERT_TM, EXPERT_FK
    E, _, F = wg.shape
    nk = F // fk

    def row_map(i, k, te, nv, ns):
        return (jnp.maximum(jnp.minimum(i, nv[0] - 1), 0), 0)

    def kk(i, k, nv):
        return jnp.where(i < nv[0], k, nk - 1)

    return pl.pallas_call(
        _experts_kernel,
        out_shape=jax.ShapeDtypeStruct((M, D), F32),
        grid_spec=pltpu.PrefetchScalarGridSpec(
            num_scalar_prefetch=3,
            grid=(M // tm, nk),
            in_specs=[
                pl.BlockSpec((tm, D), row_map),
                pl.BlockSpec((1, D, fk), lambda i, k, te, nv, ns: (te[i], 0, kk(i, k, nv))),
                pl.BlockSpec((1, D, fk), lambda i, k, te, nv, ns: (te[i], 0, kk(i, k, nv))),
                pl.BlockSpec((1, fk, D), lambda i, k, te, nv, ns: (te[i], kk(i, k, nv), 0)),
            ],
            out_specs=pl.BlockSpec((tm, D), lambda i, k, te, nv, ns: (i, 0)),
            scratch_shapes=[pltpu.VMEM((tm, D), BF16)]),
        compiler_params=pltpu.CompilerParams(
            dimension_semantics=("arbitrary", "arbitrary"),
            vmem_limit_bytes=V7X_VMEM_LIMIT_BYTES),
        name="moe_experts",
    )(tile_expert, n_valid, n_sub, xs, wg, wu, wd)


def _combine_kernel(gdst_ref, glen_ref, gsrc_ref, x_ref, info_ref, gf_ref, ys_hbm, o_ref,
                    ybuf, sem):
    t = pl.program_id(0)
    n_t = pl.num_programs(0)
    T = x_ref.shape[0]
    slot = t % 2
    tables = (gdst_ref, glen_ref, gsrc_ref)
    def fetch(tile, into):
        ybuf[into, 2 * T:SLOT_ROWS, :] = jnp.zeros((SLOT_ROWS - 2 * T, ybuf.shape[2]), F32)
        _start(_tile_runs(tile, tables, ys_hbm, ybuf.at[into], sem.at[into], to_hbm=False))

    @pl.when(t == 0)
    def _():
        fetch(t, slot)

    @pl.when(t + 1 < n_t)
    def _():
        fetch(t + 1, 1 - slot)

    _wait_rows(_tile_rows(t, tables), ys_hbm, ybuf.at[slot], sem.at[slot])
    y = ybuf[slot].astype(BF16)
    info = info_ref[...]
    hit1, hit2 = _slots(info, T)
    weights = (jnp.where(hit1, info[:, 2:3], 0.0) + jnp.where(hit2, info[:, 3:4], 0.0)).astype(BF16)
    o_ref[...] = _rms(x_ref[...] + _dot(weights, y), gf_ref[...])


def _combine(x, info, gf, ys, gdst, glen, gsrc):
    N, D = x.shape
    T = ROUTE_T
    return pl.pallas_call(
        _combine_kernel,
        out_shape=jax.ShapeDtypeStruct((N, D), F32),
        grid_spec=pltpu.PrefetchScalarGridSpec(
            num_scalar_prefetch=3,
            grid=(N // T,),
            in_specs=[pl.BlockSpec((T, D), lambda t, *_: (t, 0)),
                      pl.BlockSpec((T, E_LANES), lambda t, *_: (t, 0)),
                      pl.BlockSpec(gf.shape, lambda t, *_: (0, 0)),
                      pl.BlockSpec(memory_space=pl.ANY)],
            out_specs=pl.BlockSpec((T, D), lambda t, *_: (t, 0)),
            scratch_shapes=[pltpu.VMEM((2, SLOT_ROWS, D), F32),
                            pltpu.SemaphoreType.DMA((2,))]),
        compiler_params=pltpu.CompilerParams(
            dimension_semantics=("arbitrary",),
            vmem_limit_bytes=V7X_VMEM_LIMIT_BYTES),
        name="moe_combine",
    )(gdst, glen, gsrc, x, info, gf, ys)


def _moe(x, h, info, cnt, wg, wu, wd, gf):
    N, D = x.shape
    n_t = N // ROUTE_T
    tm = EXPERT_TM
    glen = cnt[:, 0, :N_EXPERTS].astype(jnp.int32)
    total = jnp.sum(glen, axis=0)
    gpad = (total + tm - 1) // tm * tm
    gend = jnp.cumsum(gpad)
    goff = gend - gpad
    gdst = goff[None, :] + jnp.cumsum(glen, axis=0) - glen
    gsrc = jnp.cumsum(glen, axis=1) - glen
    rows_max = -(-(2 * N + n_t * N_EXPERTS * (RUN_ALIGN - 1) + N_EXPERTS * (tm - RUN_ALIGN)) // tm) * tm
    n_tiles = rows_max // tm
    n_valid = (gend[-1] // tm).reshape(1)
    tile_row = jnp.minimum(jnp.arange(n_tiles, dtype=jnp.int32), n_valid[0] - 1) * tm
    tile_expert = jnp.sum((tile_row[:, None] >= gend[None, :]).astype(jnp.int32), axis=1)
    tile_rows = jnp.clip((goff + total)[tile_expert] - tile_row, 0, tm)
    tile_rows = jnp.where(jnp.arange(n_tiles) < n_valid[0], tile_rows, 0)
    n_sub = (tile_rows + EXPERT_SUB - 1) // EXPERT_SUB
    flat = lambda a: a.reshape(-1).astype(jnp.int32)
    tail0 = jnp.concatenate([goff + total, gend[-1:]])
    tailn = jnp.concatenate([gpad - total, (rows_max - gend[-1:]) // (tm // 2)])
    xs = _dispatch(h, info, flat(gdst), flat(glen), flat(gsrc), flat(tail0), flat(tailn), rows_max)
    ys = _experts(xs, flat(tile_expert), flat(n_valid), flat(n_sub), wg, wu, wd)
    return _combine(x, info, gf, ys, flat(gdst), flat(glen), flat(gsrc))


def kernel(x, norm1_g, w_in, pool_w, pool_scale, gm_norm_g, gm_ws, gm_b,
           conv_dw_w, conv_dw_b, conv_ln_g, conv_ln_b, conv_pw_w, conv_pw_b,
           w_out, norm2_g, ffn_wg, ffn_wu, ffn_wd,
           moe_router, moe_wg, moe_wu, moe_wd, final_g):
    B, S, D = x.shape
    depth = w_in.shape[0]
    assert depth == 2, "layer 0 is the dense SwiGLU layer, layer 1 the expert layer + final norm"
    row = lambda t: t.reshape(1, -1)
    for l in range(depth):
        pool_bd = jax.scipy.linalg.block_diag(*[pool_w[l, gi] for gi in range(len(POOL_WINDOWS))])
        gm_wcat = jnp.transpose(gm_ws[l], (1, 0, 2)).reshape(GMLP_BLOCK, GMLP_HEADS * GMLP_BLOCK)
        gm_bias = jnp.repeat(gm_b[l].T, GMLP_HEAD_DIM, axis=1)
        j = l // 2
        expert_layer = l % 2 == 1
        route_params = None
        if expert_layer:
            router_p = jnp.pad(moe_router[j], ((0, 0), (0, E_LANES - N_EXPERTS)))
            route_params = (row(norm2_g[l]), router_p)
        mixed = _mixer(x, l, row(norm1_g[l]), w_in, pool_bd.astype(BF16),
                       row(pool_scale[l]), row(gm_norm_g[l]), gm_wcat, gm_bias,
                       conv_dw_w[l], row(conv_dw_b[l]), row(conv_ln_g[l]), row(conv_ln_b[l]),
                       conv_pw_w, row(conv_pw_b[l]), w_out,
                       route_params=route_params)
        if expert_layer:
            x, h, info, cnt = mixed
            xf = _moe(x.reshape(B * S, D), h, info, cnt, moe_wg[j], moe_wu[j], moe_wd[j],
                      row(final_g))
        else:
            xf = _ffn(mixed.reshape(B * S, D), row(norm2_g[l]), ffn_wg[j], ffn_wu[j], ffn_wd[j])
        x = xf.reshape(B, S, D)
    return x
```

```python
import functools

import jax
import jax.numpy as jnp
from jax import lax
from jax.experimental import pallas as pl
from jax.experimental.pallas import tpu as pltpu

D_MODEL = 1024
CHUNK = 64
D_POOL = 256
POOL_WINDOWS = (2, 4, 8, 16)
POOL_GROUP = 64
D_GMLP = 384
GMLP_HEADS = 4
GMLP_HEAD_DIM = 96
GMLP_BLOCK = 128
D_CONV = 384
CONV_WIDTH = 31
D_IN = D_POOL + 2 * D_GMLP + 2 * D_CONV
D_FF = 2816
N_EXPERTS = 8
D_FF_EXPERT = 3584
EPS = 1e-6

V7X_VMEM_LIMIT_BYTES = 56 * 1024 * 1024
LANES = 128
SUBLANES = 8

POOL_PAD = SUBLANES
POOL_HALO = POOL_PAD + max(POOL_WINDOWS)
CONV_HALO = 32
MIX_TM = 1024
CONV_ROWS = 256
FFN_TM = 1024
ROUTE_T = 512
EXPERT_TM = 1024
EXPERT_FK = 512
EXPERT_SUB = 256

BF16 = jnp.bfloat16
F32 = jnp.float32


def _rms(x, g):
    return x * lax.rsqrt(jnp.mean(x * x, axis=-1, keepdims=True) + EPS) * g


def _silu(x):
    return x * jax.nn.sigmoid(x)


def _dot(a, b):
    return jnp.dot(a, b, preferred_element_type=F32)


def _mixer_kernel(x_ref, g1_ref, win_ref, poolw_ref, pools_ref, gmg_ref,
                  gmws_ref, gmb_ref, dww_ref, dwb_ref, lng_ref, lnb_ref,
                  pww_ref, pwb_ref, wout_ref, *rest, route):
    if route:
        g2_ref, router_ref, o_ref, h_ref, info_ref, cnt_ref = rest[:6]
        rest = rest[6:]
    else:
        o_ref = rest[0]
        rest = rest[1:]
    abuf, s2buf, s4buf, s8buf, hbuf, hshift, wtap, ycat = rest
    tm = x_ref.shape[1]
    s = pl.program_id(1)

    @pl.when(s == 0)
    def _():
        abuf[0:POOL_HALO, :] = jnp.zeros((POOL_HALO, D_POOL), F32)
        s2buf[0:POOL_PAD, :] = jnp.zeros((POOL_PAD, D_POOL), F32)
        s4buf[0:POOL_PAD, :] = jnp.zeros((POOL_PAD, D_POOL), F32)
        s8buf[0:POOL_PAD, :] = jnp.zeros((POOL_PAD, LANES), F32)
        hbuf[0:CONV_HALO, :] = jnp.zeros((CONV_HALO, D_CONV), F32)
        for j in range(CONV_WIDTH):
            wtap[j] = jnp.broadcast_to(dww_ref[j:j + 1, :], (SUBLANES, D_CONV))

    x = x_ref[0]
    h = _rms(x, g1_ref[...])
    p = _dot(h.astype(BF16), win_ref[...])
    o1 = D_POOL
    o2 = o1 + D_GMLP
    o3 = o2 + D_GMLP
    o4 = o3 + D_CONV
    u = p[:, o1:o2]
    v = p[:, o2:o3]
    cv = p[:, o3:o4]
    cg = p[:, o4:]

    H = POOL_HALO
    n = tm + H - POOL_PAD
    abuf[H:H + tm, :] = p[:, :o1]
    s2buf[POOL_PAD:POOL_PAD + n, :] = abuf[POOL_PAD:POOL_PAD + n, :] + abuf[POOL_PAD - 1:POOL_PAD - 1 + n, :]
    s4buf[POOL_PAD:POOL_PAD + n, :] = s2buf[POOL_PAD:POOL_PAD + n, :] + s2buf[POOL_PAD - 2:POOL_PAD - 2 + n, :]
    s8buf[POOL_PAD:POOL_PAD + n, :] = (s4buf[POOL_PAD:POOL_PAD + n, LANES:]
                                       + s4buf[POOL_PAD - 4:POOL_PAD - 4 + n, LANES:])
    s16 = s8buf[H:H + tm, :] + s8buf[H - 8:H - 8 + tm, :]
    lane = lax.broadcasted_iota(jnp.int32, (1, LANES), 1)
    row = lax.broadcasted_iota(jnp.int32, (tm, 1), 0)
    tpos = (s * tm + row + 1).astype(F32)
    first = lane < POOL_GROUP
    ys = []
    for lo, (w_small, w_big), s_small, s_big in (
            (0, POOL_WINDOWS[0:2], s2buf[H:H + tm, 0:LANES], s4buf[H:H + tm, 0:LANES]),
            (LANES, POOL_WINDOWS[2:4], s8buf[H:H + tm, :], s16)):
        ssel = jnp.where(first, s_small, s_big)
        win = jnp.where(first, float(w_small), float(w_big))
        mean = ssel / jnp.minimum(tpos, win)
        ys.append(mean - abuf[H:H + tm, lo:lo + LANES])
    y = jnp.concatenate(ys, axis=1).astype(BF16)
    ya = _dot(y, poolw_ref[...]) * pools_ref[...]
    ycat[:, 0:o1] = ya.astype(BF16)
    abuf[POOL_PAD:H, :] = abuf[tm + POOL_PAD:tm + H, :]

    vn = _rms(v, gmg_ref[...])
    ri = lax.broadcasted_iota(jnp.int32, (GMLP_BLOCK, GMLP_HEADS * GMLP_BLOCK), 0)
    cj = lax.broadcasted_iota(jnp.int32, (GMLP_BLOCK, GMLP_HEADS * GMLP_BLOCK), 1)
    causal = (ri // CHUNK) >= ((cj % GMLP_BLOCK) // CHUNK)
    wcat = jnp.where(causal, gmws_ref[...], 0.0).astype(BF16)
    hid = lax.broadcasted_iota(jnp.int32, (1, D_GMLP), 1) // GMLP_HEAD_DIM
    gmb = gmb_ref[...]
    for blk in range(tm // GMLP_BLOCK):
        r0 = blk * GMLP_BLOCK
        vb = vn[r0:r0 + GMLP_BLOCK, :]
        vstack = jnp.concatenate(
            [jnp.where(hid == hh, vb, 0.0) for hh in range(GMLP_HEADS)],
            axis=0).astype(BF16)
        z = _dot(wcat, vstack) + gmb
        ycat[r0:r0 + GMLP_BLOCK, o1:o2] = (u[r0:r0 + GMLP_BLOCK, :] * z).astype(BF16)

    hbuf[CONV_HALO:CONV_HALO + tm, :] = cv * jax.nn.sigmoid(cg)
    base = CONV_HALO - (CONV_WIDTH - 1)
    dwb = dwb_ref[...]
    lng = lng_ref[...]
    lnb = lnb_ref[...]
    pwb = pwb_ref[...]
    for b in range(SUBLANES):
        n_rows = tm + SUBLANES * (len(range(b, CONV_WIDTH, SUBLANES)) - 1)
        hshift[b, 0:n_rows, :] = hbuf[base + b:base + b + n_rows, :]
    for c0 in range(0, tm, CONV_ROWS):
        acc = None
        for j in range(CONV_WIDTH):
            r0 = c0 + SUBLANES * (j // SUBLANES)
            win = hshift[j % SUBLANES, r0:r0 + CONV_ROWS, :].reshape(
                CONV_ROWS // SUBLANES, SUBLANES, D_CONV)
            term = wtap[j] * win
            acc = term if acc is None else acc + term
        acc = acc.reshape(CONV_ROWS, D_CONV) + dwb
        mu = jnp.mean(acc, axis=-1, keepdims=True)
        cen = acc - mu
        var = jnp.mean(cen * cen, axis=-1, keepdims=True)
        ln = cen * lax.rsqrt(var + EPS) * lng + lnb
        yc = _dot(_silu(ln).astype(BF16), pww_ref[...]) + pwb
        ycat[c0:c0 + CONV_ROWS, o2:D_MODEL] = yc.astype(BF16)
    hbuf[0:CONV_HALO, :] = hbuf[tm:tm + CONV_HALO, :]

    o_ref[0] = x + _dot(ycat[...], wout_ref[...])

    if route:
        for r in range(tm // ROUTE_T):
            rows = slice(r * ROUTE_T, (r + 1) * ROUTE_T)
            h_hi, info, padc = _route_tile(o_ref[0, rows, :], g2_ref[...], router_ref[...])
            h_ref[rows, :] = h_hi
            info_ref[rows, :] = info
            cnt_ref[r] = padc


def _const_spec(shape):
    nd = len(shape)
    return pl.BlockSpec(shape, lambda b, s: (0,) * nd, pipeline_mode=pl.Buffered(1))


def _mixer(x, g1, w_in, pool_bd, pool_scale, gm_g, gm_wcat, gm_bias, dw_w, dw_b,
           ln_g, ln_b, pw_w, pw_b, w_out, route_params=None):
    B, S, D = x.shape
    tm = MIX_TM
    tps = S // tm
    weights = (g1, w_in, pool_bd, pool_scale, gm_g, gm_wcat, gm_bias, dw_w, dw_b,
               ln_g, ln_b, pw_w, pw_b, w_out)
    out_shape = jax.ShapeDtypeStruct((B, S, D), F32)
    out_specs = pl.BlockSpec((1, tm, D), lambda b, s: (b, s, 0))
    if route_params is not None:
        weights += tuple(route_params)
        per_step = tm // ROUTE_T
        out_shape = (out_shape,
                     jax.ShapeDtypeStruct((B * S, D), BF16),
                     jax.ShapeDtypeStruct((B * S, E_LANES), F32),
                     jax.ShapeDtypeStruct((B * S // ROUTE_T, 1, E_LANES), F32))
        out_specs = (out_specs,
                     pl.BlockSpec((tm, D), lambda b, s: (b * tps + s, 0)),
                     pl.BlockSpec((tm, E_LANES), lambda b, s: (b * tps + s, 0)),
                     pl.BlockSpec((per_step, 1, E_LANES), lambda b, s: (b * tps + s, 0, 0)))
    return pl.pallas_call(
        functools.partial(_mixer_kernel, route=route_params is not None),
        out_shape=out_shape,
        grid=(B, S // tm),
        in_specs=[pl.BlockSpec((1, tm, D), lambda b, s: (b, s, 0))]
        + [_const_spec(w.shape) for w in weights],
        out_specs=out_specs,
        scratch_shapes=[
            pltpu.VMEM((POOL_HALO + tm, D_POOL), F32),
            pltpu.VMEM((POOL_HALO + tm, D_POOL), F32),
            pltpu.VMEM((POOL_HALO + tm, D_POOL), F32),
            pltpu.VMEM((POOL_HALO + tm, LANES), F32),
            pltpu.VMEM((CONV_HALO + tm, D_CONV), F32),
            pltpu.VMEM((SUBLANES, tm + CONV_HALO - SUBLANES, D_CONV), F32),
            pltpu.VMEM((CONV_WIDTH, SUBLANES, D_CONV), F32),
            pltpu.VMEM((tm, D_MODEL), BF16),
        ],
        compiler_params=pltpu.CompilerParams(
            dimension_semantics=("arbitrary", "arbitrary"),
            vmem_limit_bytes=V7X_VMEM_LIMIT_BYTES),
        name="token_mixer",
    )(x, *weights)


WEIGHT_STAGE_CHUNKS = 16
WEIGHT_STAGE_SLOTS = 4


def _load_bf16(w_hbm, w_bf, stage, sem):
    rows = stage.shape[1]

    def piece(c):
        slot = c % WEIGHT_STAGE_SLOTS
        return pltpu.make_async_copy(w_hbm.at[pl.ds(c * rows, rows)], stage.at[slot], sem.at[slot])

    for c in range(WEIGHT_STAGE_SLOTS - 1):
        piece(c).start()
    for c in range(WEIGHT_STAGE_CHUNKS):
        ahead = c + WEIGHT_STAGE_SLOTS - 1
        if ahead < WEIGHT_STAGE_CHUNKS:
            piece(ahead).start()
        piece(c).wait()
        w_bf[c * rows:(c + 1) * rows, :] = stage[c % WEIGHT_STAGE_SLOTS].astype(BF16)


def _ffn_kernel(x_ref, g_ref, wg_hbm, wu_hbm, wd_hbm, o_ref, wg, wu, wd, stage_in, stage_out, sem):
    @pl.when(pl.program_id(0) == 0)
    def _():
        _load_bf16(wg_hbm, wg, stage_in, sem)
        _load_bf16(wu_hbm, wu, stage_in, sem)
        _load_bf16(wd_hbm, wd, stage_out, sem)

    x = x_ref[...]
    h = _rms(x, g_ref[...]).astype(BF16)
    gate = _dot(h, wg[...])
    up = _dot(h, wu[...])
    act = (_silu(gate) * up).astype(BF16)
    o_ref[...] = x + _dot(act, wd[...])


def _ffn(x, g, wg, wu, wd):
    N, D = x.shape
    F = wg.shape[1]
    tm = FFN_TM
    hbm = pl.BlockSpec(memory_space=pl.ANY)
    return pl.pallas_call(
        _ffn_kernel,
        out_shape=jax.ShapeDtypeStruct((N, D), F32),
        grid=(N // tm,),
        in_specs=[pl.BlockSpec((tm, D), lambda i: (i, 0)),
                  pl.BlockSpec(g.shape, lambda i: (0, 0)), hbm, hbm, hbm],
        out_specs=pl.BlockSpec((tm, D), lambda i: (i, 0)),
        scratch_shapes=[pltpu.VMEM((D, F), BF16), pltpu.VMEM((D, F), BF16), pltpu.VMEM((F, D), BF16),
                        pltpu.VMEM((WEIGHT_STAGE_SLOTS, D // WEIGHT_STAGE_CHUNKS, F), F32),
                        pltpu.VMEM((WEIGHT_STAGE_SLOTS, F // WEIGHT_STAGE_CHUNKS, D), F32),
                        pltpu.SemaphoreType.DMA((WEIGHT_STAGE_SLOTS,))],
        compiler_params=pltpu.CompilerParams(
            dimension_semantics=("arbitrary",),
            vmem_limit_bytes=V7X_VMEM_LIMIT_BYTES),
        name="dense_swiglu",
    )(x, g, wg, wu, wd)


E_LANES = LANES
RUN_ALIGN = SUBLANES
SLOT_ROWS = 2 * ROUTE_T + RUN_ALIGN * N_EXPERTS
RUN_BITS = tuple(range((ROUTE_T // RUN_ALIGN).bit_length() - 1, -1, -1))
ZERO_PIECE_ROWS = EXPERT_TM // 2
TAIL_BITS = tuple(range((ZERO_PIECE_ROWS // RUN_ALIGN).bit_length() - 1, -1, -1))
SLOT_BITS = tuple(range((SLOT_ROWS // RUN_ALIGN).bit_length() - 1, -1, -1))


def _route_tile(x, g, r):
    T = x.shape[0]
    h = _rms(x, g)
    h_hi = h.astype(BF16)
    h_lo = (h - h_hi.astype(F32)).astype(BF16)
    r_hi = r.astype(BF16)
    r_lo = (r - r_hi.astype(F32)).astype(BF16)
    hi_both = _dot(h_hi, jnp.concatenate([r_hi, r_lo], axis=1))
    logits = hi_both[:, :E_LANES] + (_dot(h_lo, r_hi) + hi_both[:, E_LANES:])
    lane = lax.broadcasted_iota(jnp.int32, (T, E_LANES), 1)
    logits = jnp.where(lane < N_EXPERTS, logits, -jnp.inf)
    m1 = jnp.max(logits, axis=-1, keepdims=True)
    i1 = jnp.min(jnp.where(logits == m1, lane, E_LANES), axis=-1, keepdims=True)
    rest = jnp.where(lane == i1, -jnp.inf, logits)
    m2 = jnp.max(rest, axis=-1, keepdims=True)
    i2 = jnp.min(jnp.where(rest == m2, lane, E_LANES), axis=-1, keepdims=True)
    e2 = jnp.exp(m2 - m1)
    w1 = 1.0 / (1.0 + e2)
    w2 = e2 / (1.0 + e2)
    oh1 = (lane == i1).astype(F32)
    oh2 = (lane == i2).astype(F32)
    oh = oh1 + oh2
    ri = lax.broadcasted_iota(jnp.int32, (T, T), 0)
    ci = lax.broadcasted_iota(jnp.int32, (T, T), 1)
    rank = _dot((ri > ci).astype(BF16), oh.astype(BF16))
    cnt = jnp.sum(oh, axis=0, keepdims=True)
    padc = jnp.floor((cnt + (RUN_ALIGN - 1.0)) * (1.0 / RUN_ALIGN)) * RUN_ALIGN
    lane1 = lax.broadcasted_iota(jnp.int32, (1, E_LANES), 1)
    start = jnp.zeros((1, E_LANES), F32)
    for e in range(N_EXPERTS - 1):
        start = start + jnp.where(lane1 > e, padc[:, e:e + 1], 0.0)
    slot = start + rank
    q1 = jnp.sum(oh1 * slot, axis=-1, keepdims=True)
    q2 = jnp.sum(oh2 * slot, axis=-1, keepdims=True)
    info = jnp.where(lane == 0, q1, jnp.where(lane == 1, q2, jnp.where(
        lane == 2, w1, jnp.where(lane == 3, w2, 0.0))))
    return h_hi, info, padc


def _run_copies(src, dst, src0, dst0, rows, bits, sem):
    m = rows // RUN_ALIGN
    off = 0
    out = []
    for b in bits:
        size = RUN_ALIGN << b
        take = (m >> b) & 1
        s0 = 0 if src0 is None else pl.multiple_of(src0 + off, RUN_ALIGN)
        d0 = pl.multiple_of(dst0 + off, RUN_ALIGN)
        out.append((take == 1, pltpu.make_async_copy(
            src.at[pl.ds(s0, size)], dst.at[pl.ds(d0, size)], sem)))
        off = off + take * size
    return out


def _start(copies):
    for pred, cp in copies:
        @pl.when(pred)
        def _(cp=cp):
            cp.start()


def _wait(copies):
    for pred, cp in copies:
        @pl.when(pred)
        def _(cp=cp):
            cp.wait()


def _tile_runs(tile, table_refs, hbm, buf, sem, *, to_hbm):
    gdst_ref, glen_ref, gsrc_ref = table_refs
    copies = []
    for e in range(N_EXPERTS):
        j = tile * N_EXPERTS + e
        if to_hbm:
            copies += _run_copies(buf, hbm, gsrc_ref[j], gdst_ref[j], glen_ref[j], RUN_BITS, sem)
        else:
            copies += _run_copies(hbm, buf, gdst_ref[j], gsrc_ref[j], glen_ref[j], RUN_BITS, sem)
    return copies


def _tile_rows(tile, table_refs):
    _, glen_ref, gsrc_ref = table_refs
    last = tile * N_EXPERTS + N_EXPERTS - 1
    return gsrc_ref[last] + glen_ref[last]


def _wait_rows(rows, src, dst, sem):
    _wait(_run_copies(src, dst, jnp.int32(0), jnp.int32(0), rows, SLOT_BITS, sem))


def _slots(info, T):
    q1 = info[:, 0:1].astype(jnp.int32)
    q2 = info[:, 1:2].astype(jnp.int32)
    r = lax.broadcasted_iota(jnp.int32, (T, SLOT_ROWS), 1)
    return q1 == r, q2 == r


def _dispatch_kernel(gdst_ref, glen_ref, gsrc_ref, tail0_ref, tailn_ref,
                     h_ref, info_ref, xs_hbm, xbuf, zbuf, sem, *, slack_pieces):
    t = pl.program_id(0)
    n_t = pl.num_programs(0)
    T = h_ref.shape[0]
    slot = t % 2
    tables = (gdst_ref, glen_ref, gsrc_ref)
    hit1, hit2 = _slots(info_ref[...], T)
    onehot = (hit1 | hit2).astype(BF16)
    xbuf[slot] = lax.dot_general(onehot, h_ref[...], (((0,), (0,)), ((), ())),
                                 preferred_element_type=F32)
    _start(_tile_runs(t, tables, xs_hbm, xbuf.at[slot], sem.at[slot], to_hbm=True))

    @pl.when(t > 0)
    def _():
        _wait_rows(_tile_rows(t - 1, tables), xbuf.at[1 - slot], xs_hbm, sem.at[1 - slot])

    @pl.when(t == n_t - 1)
    def _():
        _wait_rows(_tile_rows(t, tables), xbuf.at[slot], xs_hbm, sem.at[slot])

    @pl.when(t == 0)
    def _():
        zbuf[...] = jnp.zeros_like(zbuf)
        tails = []
        for e in range(N_EXPERTS):
            tails += _run_copies(zbuf, xs_hbm, None, tail0_ref[e], tailn_ref[e], TAIL_BITS,
                                 sem.at[2])
        piece = zbuf.shape[0]
        for c in range(slack_pieces):
            d0 = pl.multiple_of(tail0_ref[N_EXPERTS] + c * piece, RUN_ALIGN)
            tails.append((c < tailn_ref[N_EXPERTS], pltpu.make_async_copy(
                zbuf, xs_hbm.at[pl.ds(d0, piece)], sem.at[2])))
        _start(tails)
        _wait(tails)


def _dispatch(h, info, gdst, glen, gsrc, tail0, tailn, rows_max):
    N, D = h.shape
    T = ROUTE_T
    slack_pieces = (rows_max - 2 * N) // ZERO_PIECE_ROWS
    return pl.pallas_call(
        functools.partial(_dispatch_kernel, slack_pieces=slack_pieces),
        out_shape=jax.ShapeDtypeStruct((rows_max, D), F32),
        grid_spec=pltpu.PrefetchScalarGridSpec(
            num_scalar_prefetch=5,
            grid=(N // T,),
            in_specs=[pl.BlockSpec((T, D), lambda t, *_: (t, 0)),
                      pl.BlockSpec((T, E_LANES), lambda t, *_: (t, 0))],
            out_specs=pl.BlockSpec(memory_space=pl.ANY),
            scratch_shapes=[pltpu.VMEM((2, SLOT_ROWS, D), F32),
                            pltpu.VMEM((ZERO_PIECE_ROWS, D), F32),
                            pltpu.SemaphoreType.DMA((3,))]),
        compiler_params=pltpu.CompilerParams(
            dimension_semantics=("arbitrary",),
            vmem_limit_bytes=V7X_VMEM_LIMIT_BYTES),
        name="moe_dispatch",
    )(gdst, glen, gsrc, tail0, tailn, h, info)


def _experts_kernel(te_ref, nv_ref, ns_ref, x_ref, wg_ref, wu_ref, wd_ref, o_ref,
                    xb):
    i = pl.program_id(0)
    k = pl.program_id(1)
    n_sub = ns_ref[i]

    full = x_ref.shape[0] // EXPERT_SUB

    @pl.when((n_sub > 0) & (k == 0))
    def _():
        xb[...] = x_ref[...].astype(BF16)
        o_ref[...] = jnp.zeros_like(o_ref)

    @pl.when(n_sub == full)
    def _():
        h = xb[...]
        act = (_silu(_dot(h, wg_ref[0].astype(BF16))) * _dot(h, wu_ref[0].astype(BF16))).astype(BF16)
        o_ref[...] += _dot(act, wd_ref[0].astype(BF16))

    @pl.when((n_sub > 0) & (n_sub < full))
    def _():
        def piece(s, carry):
            rows = pl.ds(pl.multiple_of(s * EXPERT_SUB, EXPERT_SUB), EXPERT_SUB)
            h = xb[rows, :]
            act = (_silu(_dot(h, wg_ref[0].astype(BF16)))
                   * _dot(h, wu_ref[0].astype(BF16))).astype(BF16)
            o_ref[rows, :] += _dot(act, wd_ref[0].astype(BF16))
            return carry

        lax.fori_loop(0, n_sub, piece, 0)

    @pl.when((n_sub == 0) & (k == 0))
    def _():
        o_ref[...] = jnp.zeros_like(o_ref)


def _experts(xs, tile_expert, n_valid, n_sub, wg, wu, wd):
    M, D = xs.shape
    tm, fk = EXPERT_TM, EXPERT_FK
    E, _, F = wg.shape
    nk = F // fk

    def row_map(i, k, te, nv, ns):
        return (jnp.maximum(jnp.minimum(i, nv[0] - 1), 0), 0)

    def kk(i, k, nv):
        return jnp.where(i < nv[0], k, nk - 1)

    return pl.pallas_call(
        _experts_kernel,
        out_shape=jax.ShapeDtypeStruct((M, D), F32),
        grid_spec=pltpu.PrefetchScalarGridSpec(
            num_scalar_prefetch=3,
            grid=(M // tm, nk),
            in_specs=[
                pl.BlockSpec((tm, D), row_map),
                pl.BlockSpec((1, D, fk), lambda i, k, te, nv, ns: (te[i], 0, kk(i, k, nv))),
                pl.BlockSpec((1, D, fk), lambda i, k, te, nv, ns: (te[i], 0, kk(i, k, nv))),
                pl.BlockSpec((1, fk, D), lambda i, k, te, nv, ns: (te[i], kk(i, k, nv), 0)),
            ],
            out_specs=pl.BlockSpec((tm, D), lambda i, k, te, nv, ns: (i, 0)),
            scratch_shapes=[pltpu.VMEM((tm, D), BF16)]),
        compiler_params=pltpu.CompilerParams(
            dimension_semantics=("arbitrary", "arbitrary"),
            vmem_limit_bytes=V7X_VMEM_LIMIT_BYTES),
        name="moe_experts",
    )(tile_expert, n_valid, n_sub, xs, wg, wu, wd)


def _combine_kernel(gdst_ref, glen_ref, gsrc_ref, x_ref, info_ref, gf_ref, ys_hbm, o_ref,
                    ybuf, sem):
    t = pl.program_id(0)
    n_t = pl.num_programs(0)
    T = x_ref.shape[0]
    slot = t % 2
    tables = (gdst_ref, glen_ref, gsrc_ref)

    def fetch(tile, into):
        ybuf[into, 2 * T:SLOT_ROWS, :] = jnp.zeros((SLOT_ROWS - 2 * T, ybuf.shape[2]), F32)
        _start(_tile_runs(tile, tables, ys_hbm, ybuf.at[into], sem.at[into], to_hbm=False))

    @pl.when(t == 0)
    def _():
        fetch(t, slot)

    @pl.when(t + 1 < n_t)
    def _():
        fetch(t + 1, 1 - slot)

    _wait_rows(_tile_rows(t, tables), ys_hbm, ybuf.at[slot], sem.at[slot])
    y = ybuf[slot].astype(BF16)
    info = info_ref[...]
    hit1, hit2 = _slots(info, T)
    weights = (jnp.where(hit1, info[:, 2:3], 0.0) + jnp.where(hit2, info[:, 3:4], 0.0)).astype(BF16)
    o_ref[...] = _rms(x_ref[...] + _dot(weights, y), gf_ref[...])


def _combine(x, info, gf, ys, gdst, glen, gsrc):
    N, D = x.shape
    T = ROUTE_T
    return pl.pallas_call(
        _combine_kernel,
        out_shape=jax.ShapeDtypeStruct((N, D), F32),
        grid_spec=pltpu.PrefetchScalarGridSpec(
            num_scalar_prefetch=3,
            grid=(N // T,),
            in_specs=[pl.BlockSpec((T, D), lambda t, *_: (t, 0)),
                      pl.BlockSpec((T, E_LANES), lambda t, *_: (t, 0)),
                      pl.BlockSpec(gf.shape, lambda t, *_: (0, 0)),
                      pl.BlockSpec(memory_space=pl.ANY)],
            out_specs=pl.BlockSpec((T, D), lambda t, *_: (t, 0)),
            scratch_shapes=[pltpu.VMEM((2, SLOT_ROWS, D), F32),
                            pltpu.SemaphoreType.DMA((2,))]),
        compiler_params=pltpu.CompilerParams(
            dimension_semantics=("arbitrary",),
            vmem_limit_bytes=V7X_VMEM_LIMIT_BYTES),
        name="moe_combine",
    )(gdst, glen, gsrc, x, info, gf, ys)


def _moe(x, h, info, cnt, wg, wu, wd, gf):
    N, D = x.shape
    n_t = N // ROUTE_T
    tm = EXPERT_TM
    glen = cnt[:, 0, :N_EXPERTS].astype(jnp.int32)
    total = jnp.sum(glen, axis=0)
    gpad = (total + tm - 1) // tm * tm
    gend = jnp.cumsum(gpad)
    goff = gend - gpad
    gdst = goff[None, :] + jnp.cumsum(glen, axis=0) - glen
    gsrc = jnp.cumsum(glen, axis=1) - glen
    rows_max = -(-(2 * N + n_t * N_EXPERTS * (RUN_ALIGN - 1) + N_EXPERTS * (tm - RUN_ALIGN)) // tm) * tm
    n_tiles = rows_max // tm
    n_valid = (gend[-1] // tm).reshape(1)
    tile_row = jnp.minimum(jnp.arange(n_tiles, dtype=jnp.int32), n_valid[0] - 1) * tm
    tile_expert = jnp.sum((tile_row[:, None] >= gend[None, :]).astype(jnp.int32), axis=1)
    tile_rows = jnp.clip((goff + total)[tile_expert] - tile_row, 0, tm)
    tile_rows = jnp.where(jnp.arange(n_tiles) < n_valid[0], tile_rows, 0)
    n_sub = (tile_rows + EXPERT_SUB - 1) // EXPERT_SUB
    flat = lambda a: a.reshape(-1).astype(jnp.int32)
    tail0 = jnp.concatenate([goff + total, gend[-1:]])
    tailn = jnp.concatenate([gpad - total, (rows_max - gend[-1:]) // ZERO_PIECE_ROWS])
    xs = _dispatch(h, info, flat(gdst), flat(glen), flat(gsrc), flat(tail0), flat(tailn), rows_max)
    ys = _experts(xs, flat(tile_expert), flat(n_valid), flat(n_sub), wg, wu, wd)
    return _combine(x, info, gf, ys, flat(gdst), flat(glen), flat(gsrc))


def kernel(x, norm1_g, w_in, pool_w, pool_scale, gm_norm_g, gm_ws, gm_b,
           conv_dw_w, conv_dw_b, conv_ln_g, conv_ln_b, conv_pw_w, conv_pw_b,
           w_out, norm2_g, ffn_wg, ffn_wu, ffn_wd,
           moe_router, moe_wg, moe_wu, moe_wd, final_g):
    B, S, D = x.shape
    depth = w_in.shape[0]
    assert depth == 2, "layer 0 is the dense SwiGLU layer, layer 1 the expert layer + final norm"
    row = lambda t: t.reshape(1, -1)
    for l in range(depth):
        pool_bd = jax.scipy.linalg.block_diag(*[pool_w[l, gi] for gi in range(len(POOL_WINDOWS))])
        gm_wcat = jnp.transpose(gm_ws[l], (1, 0, 2)).reshape(GMLP_BLOCK, GMLP_HEADS * GMLP_BLOCK)
        gm_bias = jnp.repeat(gm_b[l].T, GMLP_HEAD_DIM, axis=1)
        j = l // 2
        expert_layer = l % 2 == 1
        route_params = None
        if expert_layer:
            router_p = jnp.pad(moe_router[j], ((0, 0), (0, E_LANES - N_EXPERTS)))
            route_params = (row(norm2_g[l]), router_p)
        mixed = _mixer(x, row(norm1_g[l]), w_in[l].astype(BF16), pool_bd.astype(BF16),
                       row(pool_scale[l]), row(gm_norm_g[l]), gm_wcat, gm_bias,
                       conv_dw_w[l], row(conv_dw_b[l]), row(conv_ln_g[l]), row(conv_ln_b[l]),
                       conv_pw_w[l].astype(BF16), row(conv_pw_b[l]), w_out[l].astype(BF16),
                       route_params=route_params)
        if expert_layer:
            x, h, info, cnt = mixed
            xf = _moe(x.reshape(B * S, D), h, info, cnt, moe_wg[j], moe_wu[j], moe_wd[j],
                      row(final_g))
        else:
            xf = _ffn(mixed.reshape(B * S, D), row(norm2_g[l]), ffn_wg[j], ffn_wu[j], ffn_wd[j])
        x = xf.reshape(B, S, D)
    return x
```

```python
import functools

import jax
import jax.numpy as jnp
from jax import lax
from jax.experimental import pallas as pl
from jax.experimental.pallas import tpu as pltpu

D_MODEL = 1024
CHUNK = 64
D_POOL = 256
POOL_WINDOWS = (2, 4, 8, 16)
POOL_GROUP = 64
D_GMLP = 384
GMLP_HEADS = 4
GMLP_HEAD_DIM = 96
GMLP_BLOCK = 128
D_CONV = 384
CONV_WIDTH = 31
D_IN = D_POOL + 2 * D_GMLP + 2 * D_CONV
D_FF = 2816
N_EXPERTS = 8
D_FF_EXPERT = 3584
EPS = 1e-6

V7X_VMEM_LIMIT_BYTES = 56 * 1024 * 1024
LANES = 128
SUBLANES = 8

POOL_PAD = SUBLANES
POOL_HALO = POOL_PAD + max(POOL_WINDOWS)
CONV_HALO = 32
MIX_TM = 1024
CONV_ROWS = 256
FFN_TM = 1024
ROUTE_T = 512
EXPERT_TM = 1024
EXPERT_FK = 512
EXPERT_SUB = 256

BF16 = jnp.bfloat16
F32 = jnp.float32


def _rms(x, g):
    return x * lax.rsqrt(jnp.mean(x * x, axis=-1, keepdims=True) + EPS) * g


def _silu(x):
    return x * jax.nn.sigmoid(x)


def _dot(a, b):
    return jnp.dot(a, b, preferred_element_type=F32)


def _mixer_kernel(x_ref, g1_ref, win_ref, poolw_ref, pools_ref, gmg_ref,
                  gmws_ref, gmb_ref, dww_ref, dwb_ref, lng_ref, lnb_ref,
                  pww_ref, pwb_ref, wout_ref, *rest, route):
    if route:
        g2_ref, router_ref, o_ref, h_ref, info_ref, cnt_ref = rest[:6]
        rest = rest[6:]
    else:
        o_ref = rest[0]
        rest = rest[1:]
    abuf, s2buf, s4buf, s8buf, hbuf, hshift, wtap, ycat = rest
    tm = x_ref.shape[1]
    s = pl.program_id(1)

    @pl.when(s == 0)
    def _():
        abuf[0:POOL_HALO, :] = jnp.zeros((POOL_HALO, D_POOL), F32)
        s2buf[0:POOL_PAD, :] = jnp.zeros((POOL_PAD, D_POOL), F32)
        s4buf[0:POOL_PAD, :] = jnp.zeros((POOL_PAD, D_POOL), F32)
        s8buf[0:POOL_PAD, :] = jnp.zeros((POOL_PAD, LANES), F32)
        hbuf[0:CONV_HALO, :] = jnp.zeros((CONV_HALO, D_CONV), F32)
        for j in range(CONV_WIDTH):
            wtap[j] = jnp.broadcast_to(dww_ref[j:j + 1, :], (SUBLANES, D_CONV))

    x = x_ref[0]
    h = _rms(x, g1_ref[...])
    p = _dot(h.astype(BF16), win_ref[...])
    o1 = D_POOL
    o2 = o1 + D_GMLP
    o3 = o2 + D_GMLP
    o4 = o3 + D_CONV
    u = p[:, o1:o2]
    v = p[:, o2:o3]
    cv = p[:, o3:o4]
    cg = p[:, o4:]

    H = POOL_HALO
    n = tm + H - POOL_PAD
    abuf[H:H + tm, :] = p[:, :o1]
    s2buf[POOL_PAD:POOL_PAD + n, :] = abuf[POOL_PAD:POOL_PAD + n, :] + abuf[POOL_PAD - 1:POOL_PAD - 1 + n, :]
    s4buf[POOL_PAD:POOL_PAD + n, :] = s2buf[POOL_PAD:POOL_PAD + n, :] + s2buf[POOL_PAD - 2:POOL_PAD - 2 + n, :]
    s8buf[POOL_PAD:POOL_PAD + n, :] = (s4buf[POOL_PAD:POOL_PAD + n, LANES:]
                                       + s4buf[POOL_PAD - 4:POOL_PAD - 4 + n, LANES:])
    s16 = s8buf[H:H + tm, :] + s8buf[H - 8:H - 8 + tm, :]
    lane = lax.broadcasted_iota(jnp.int32, (1, LANES), 1)
    row = lax.broadcasted_iota(jnp.int32, (tm, 1), 0)
    tpos = (s * tm + row + 1).astype(F32)
    first = lane < POOL_GROUP
    ys = []
    for lo, (w_small, w_big), s_small, s_big in (
            (0, POOL_WINDOWS[0:2], s2buf[H:H + tm, 0:LANES], s4buf[H:H + tm, 0:LANES]),
            (LANES, POOL_WINDOWS[2:4], s8buf[H:H + tm, :], s16)):
        ssel = jnp.where(first, s_small, s_big)
        win = jnp.where(first, float(w_small), float(w_big))
        mean = ssel / jnp.minimum(tpos, win)
        ys.append(mean - abuf[H:H + tm, lo:lo + LANES])
    y = jnp.concatenate(ys, axis=1).astype(BF16)
    ya = _dot(y, poolw_ref[...]) * pools_ref[...]
    ycat[:, 0:o1] = ya.astype(BF16)
    abuf[POOL_PAD:H, :] = abuf[tm + POOL_PAD:tm + H, :]

    vn = _rms(v, gmg_ref[...])
    ri = lax.broadcasted_iota(jnp.int32, (GMLP_BLOCK, GMLP_HEADS * GMLP_BLOCK), 0)
    cj = lax.broadcasted_iota(jnp.int32, (GMLP_BLOCK, GMLP_HEADS * GMLP_BLOCK), 1)
    causal = (ri // CHUNK) >= ((cj % GMLP_BLOCK) // CHUNK)
    wcat = jnp.where(causal, gmws_ref[...], 0.0).astype(BF16)
    hid = lax.broadcasted_iota(jnp.int32, (1, D_GMLP), 1) // GMLP_HEAD_DIM
    gmb = gmb_ref[...]
    for blk in range(tm // GMLP_BLOCK):
        r0 = blk * GMLP_BLOCK
        vb = vn[r0:r0 + GMLP_BLOCK, :]
        vstack = jnp.concatenate(
            [jnp.where(hid == hh, vb, 0.0) for hh in range(GMLP_HEADS)],
            axis=0).astype(BF16)
        z = _dot(wcat, vstack) + gmb
        ycat[r0:r0 + GMLP_BLOCK, o1:o2] = (u[r0:r0 + GMLP_BLOCK, :] * z).astype(BF16)

    hbuf[CONV_HALO:CONV_HALO + tm, :] = cv * jax.nn.sigmoid(cg)
    base = CONV_HALO - (CONV_WIDTH - 1)
    dwb = dwb_ref[...]
    lng = lng_ref[...]
    lnb = lnb_ref[...]
    pwb = pwb_ref[...]
    for b in range(SUBLANES):
        n_rows = tm + SUBLANES * (len(range(b, CONV_WIDTH, SUBLANES)) - 1)
        hshift[b, 0:n_rows, :] = hbuf[base + b:base + b + n_rows, :]
    for c0 in range(0, tm, CONV_ROWS):
        acc = None
        for j in range(CONV_WIDTH):
            r0 = c0 + SUBLANES * (j // SUBLANES)
            win = hshift[j % SUBLANES, r0:r0 + CONV_ROWS, :].reshape(
                CONV_ROWS // SUBLANES, SUBLANES, D_CONV)
            term = wtap[j] * win
            acc = term if acc is None else acc + term
        acc = acc.reshape(CONV_ROWS, D_CONV) + dwb
        mu = jnp.mean(acc, axis=-1, keepdims=True)
        cen = acc - mu
        var = jnp.mean(cen * cen, axis=-1, keepdims=True)
        ln = cen * lax.rsqrt(var + EPS) * lng + lnb
        yc = _dot(_silu(ln).astype(BF16), pww_ref[...]) + pwb
        ycat[c0:c0 + CONV_ROWS, o2:D_MODEL] = yc.astype(BF16)
    hbuf[0:CONV_HALO, :] = hbuf[tm:tm + CONV_HALO, :]

    o_ref[0] = x + _dot(ycat[...], wout_ref[...])

    if route:
        for r in range(tm // ROUTE_T):
            rows = slice(r * ROUTE_T, (r + 1) * ROUTE_T)
            h_hi, info, padc = _route_tile(o_ref[0, rows, :], g2_ref[...], router_ref[...])
            h_ref[rows, :] = h_hi
            info_ref[rows, :] = info
            cnt_ref[r] = padc


def _const_spec(shape):
    nd = len(shape)
    return pl.BlockSpec(shape, lambda b, s: (0,) * nd, pipeline_mode=pl.Buffered(1))


def _mixer(x, g1, w_in, pool_bd, pool_scale, gm_g, gm_wcat, gm_bias, dw_w, dw_b,
           ln_g, ln_b, pw_w, pw_b, w_out, route_params=None):
    B, S, D = x.shape
    tm = MIX_TM
    tps = S // tm
    weights = (g1, w_in, pool_bd, pool_scale, gm_g, gm_wcat, gm_bias, dw_w, dw_b,
               ln_g, ln_b, pw_w, pw_b, w_out)
    out_shape = jax.ShapeDtypeStruct((B, S, D), F32)
    out_specs = pl.BlockSpec((1, tm, D), lambda b, s: (b, s, 0))
    if route_params is not None:
        weights += tuple(route_params)
        per_step = tm // ROUTE_T
        out_shape = (out_shape,
                     jax.ShapeDtypeStruct((B * S, D), BF16),
                     jax.ShapeDtypeStruct((B * S, E_LANES), F32),
                     jax.ShapeDtypeStruct((B * S // ROUTE_T, 1, E_LANES), F32))
        out_specs = (out_specs,
                     pl.BlockSpec((tm, D), lambda b, s: (b * tps + s, 0)),
                     pl.BlockSpec((tm, E_LANES), lambda b, s: (b * tps + s, 0)),
                     pl.BlockSpec((per_step, 1, E_LANES), lambda b, s: (b * tps + s, 0, 0)))
    return pl.pallas_call(
        functools.partial(_mixer_kernel, route=route_params is not None),
        out_shape=out_shape,
        grid=(B, S // tm),
        in_specs=[pl.BlockSpec((1, tm, D), lambda b, s: (b, s, 0))]
        + [_const_spec(w.shape) for w in weights],
        out_specs=out_specs,
        scratch_shapes=[
            pltpu.VMEM((POOL_HALO + tm, D_POOL), F32),
            pltpu.VMEM((POOL_HALO + tm, D_POOL), F32),
            pltpu.VMEM((POOL_HALO + tm, D_POOL), F32),
            pltpu.VMEM((POOL_HALO + tm, LANES), F32),
            pltpu.VMEM((CONV_HALO + tm, D_CONV), F32),
            pltpu.VMEM((SUBLANES, tm + CONV_HALO - SUBLANES, D_CONV), F32),
            pltpu.VMEM((CONV_WIDTH, SUBLANES, D_CONV), F32),
            pltpu.VMEM((tm, D_MODEL), BF16),
        ],
        compiler_params=pltpu.CompilerParams(
            dimension_semantics=("arbitrary", "arbitrary"),
            vmem_limit_bytes=V7X_VMEM_LIMIT_BYTES),
        name="token_mixer",
    )(x, *weights)


WEIGHT_STAGE_CHUNKS = 16
WEIGHT_STAGE_SLOTS = 4


def _load_bf16(w_hbm, w_bf, stage, sem):
    rows = stage.shape[1]

    def piece(c):
        slot = c % WEIGHT_STAGE_SLOTS
        return pltpu.make_async_copy(w_hbm.at[pl.ds(c * rows, rows)], stage.at[slot], sem.at[slot])

    for c in range(WEIGHT_STAGE_SLOTS - 1):
        piece(c).start()
    for c in range(WEIGHT_STAGE_CHUNKS):
        ahead = c + WEIGHT_STAGE_SLOTS - 1
        if ahead < WEIGHT_STAGE_CHUNKS:
            piece(ahead).start()
        piece(c).wait()
        w_bf[c * rows:(c + 1) * rows, :] = stage[c % WEIGHT_STAGE_SLOTS].astype(BF16)


def _ffn_kernel(x_ref, g_ref, wg_hbm, wu_hbm, wd_hbm, o_ref, wg, wu, wd, stage_in, stage_out, sem):
    @pl.when(pl.program_id(0) == 0)
    def _():
        _load_bf16(wg_hbm, wg, stage_in, sem)
        _load_bf16(wu_hbm, wu, stage_in, sem)
        _load_bf16(wd_hbm, wd, stage_out, sem)

    x = x_ref[...]
    h = _rms(x, g_ref[...]).astype(BF16)
    gate = _dot(h, wg[...])
    up = _dot(h, wu[...])
    act = (_silu(gate) * up).astype(BF16)
    o_ref[...] = x + _dot(act, wd[...])


def _ffn(x, g, wg, wu, wd):
    N, D = x.shape
    F = wg.shape[1]
    tm = FFN_TM
    hbm = pl.BlockSpec(memory_space=pl.ANY)
    return pl.pallas_call(
        _ffn_kernel,
        out_shape=jax.ShapeDtypeStruct((N, D), F32),
        grid=(N // tm,),
        in_specs=[pl.BlockSpec((tm, D), lambda i: (i, 0)),
                  pl.BlockSpec(g.shape, lambda i: (0, 0)), hbm, hbm, hbm],
        out_specs=pl.BlockSpec((tm, D), lambda i: (i, 0)),
        scratch_shapes=[pltpu.VMEM((D, F), BF16), pltpu.VMEM((D, F), BF16), pltpu.VMEM((F, D), BF16),
                        pltpu.VMEM((WEIGHT_STAGE_SLOTS, D // WEIGHT_STAGE_CHUNKS, F), F32),
                        pltpu.VMEM((WEIGHT_STAGE_SLOTS, F // WEIGHT_STAGE_CHUNKS, D), F32),
                        pltpu.SemaphoreType.DMA((WEIGHT_STAGE_SLOTS,))],
        compiler_params=pltpu.CompilerParams(
            dimension_semantics=("arbitrary",),
            vmem_limit_bytes=V7X_VMEM_LIMIT_BYTES),
        name="dense_swiglu",
    )(x, g, wg, wu, wd)


E_LANES = LANES
RUN_ALIGN = SUBLANES
SLOT_ROWS = 2 * ROUTE_T + RUN_ALIGN * N_EXPERTS
RUN_BITS = tuple(range((ROUTE_T // RUN_ALIGN).bit_length() - 1, -1, -1))
ZERO_PIECE_ROWS = EXPERT_TM // 2
TAIL_BITS = tuple(range((ZERO_PIECE_ROWS // RUN_ALIGN).bit_length() - 1, -1, -1))
SLOT_BITS = tuple(range((SLOT_ROWS // RUN_ALIGN).bit_length() - 1, -1, -1))


def _route_tile(x, g, r):
    T = x.shape[0]
    h = _rms(x, g)
    h_hi = h.astype(BF16)
    h_lo = (h - h_hi.astype(F32)).astype(BF16)
    r_hi = r.astype(BF16)
    r_lo = (r - r_hi.astype(F32)).astype(BF16)
    hi_both = _dot(h_hi, jnp.concatenate([r_hi, r_lo], axis=1))
    logits = hi_both[:, :E_LANES] + (_dot(h_lo, r_hi) + hi_both[:, E_LANES:])
    lane = lax.broadcasted_iota(jnp.int32, (T, E_LANES), 1)
    logits = jnp.where(lane < N_EXPERTS, logits, -jnp.inf)
    m1 = jnp.max(logits, axis=-1, keepdims=True)
    i1 = jnp.min(jnp.where(logits == m1, lane, E_LANES), axis=-1, keepdims=True)
    rest = jnp.where(lane == i1, -jnp.inf, logits)
    m2 = jnp.max(rest, axis=-1, keepdims=True)
    i2 = jnp.min(jnp.where(rest == m2, lane, E_LANES), axis=-1, keepdims=True)
    e2 = jnp.exp(m2 - m1)
    w1 = 1.0 / (1.0 + e2)
    w2 = e2 / (1.0 + e2)
    oh1 = (lane == i1).astype(F32)
    oh2 = (lane == i2).astype(F32)
    oh = oh1 + oh2
    ri = lax.broadcasted_iota(jnp.int32, (T, T), 0)
    ci = lax.broadcasted_iota(jnp.int32, (T, T), 1)
    rank = _dot((ri > ci).astype(BF16), oh.astype(BF16))
    cnt = jnp.sum(oh, axis=0, keepdims=True)
    padc = jnp.floor((cnt + (RUN_ALIGN - 1.0)) * (1.0 / RUN_ALIGN)) * RUN_ALIGN
    lane1 = lax.broadcasted_iota(jnp.int32, (1, E_LANES), 1)
    start = jnp.zeros((1, E_LANES), F32)
    for e in range(N_EXPERTS - 1):
        start = start + jnp.where(lane1 > e, padc[:, e:e + 1], 0.0)
    slot = start + rank
    q1 = jnp.sum(oh1 * slot, axis=-1, keepdims=True)
    q2 = jnp.sum(oh2 * slot, axis=-1, keepdims=True)
    info = jnp.where(lane == 0, q1, jnp.where(lane == 1, q2, jnp.where(
        lane == 2, w1, jnp.where(lane == 3, w2, 0.0))))
    return h_hi, info, padc


def _run_copies(src, dst, src0, dst0, rows, bits, sem):
    m = rows // RUN_ALIGN
    off = 0
    out = []
    for b in bits:
        size = RUN_ALIGN << b
        take = (m >> b) & 1
        s0 = 0 if src0 is None else pl.multiple_of(src0 + off, RUN_ALIGN)
        d0 = pl.multiple_of(dst0 + off, RUN_ALIGN)
        out.append((take == 1, pltpu.make_async_copy(
            src.at[pl.ds(s0, size)], dst.at[pl.ds(d0, size)], sem)))
        off = off + take * size
    return out


def _start(copies):
    for pred, cp in copies:
        @pl.when(pred)
        def _(cp=cp):
            cp.start()


def _wait(copies):
    for pred, cp in copies:
        @pl.when(pred)
        def _(cp=cp):
            cp.wait()


def _tile_runs(tile, table_refs, hbm, buf, sem, *, to_hbm):
    gdst_ref, glen_ref, gsrc_ref = table_refs
    copies = []
    for e in range(N_EXPERTS):
        j = tile * N_EXPERTS + e
        if to_hbm:
            copies += _run_copies(buf, hbm, gsrc_ref[j], gdst_ref[j], glen_ref[j], RUN_BITS, sem)
        else:
            copies += _run_copies(hbm, buf, gdst_ref[j], gsrc_ref[j], glen_ref[j], RUN_BITS, sem)
    return copies


def _tile_rows(tile, table_refs):
    _, glen_ref, gsrc_ref = table_refs
    last = tile * N_EXPERTS + N_EXPERTS - 1
    return gsrc_ref[last] + glen_ref[last]


def _wait_rows(rows, src, dst, sem):
    _wait(_run_copies(src, dst, jnp.int32(0), jnp.int32(0), rows, SLOT_BITS, sem))


def _slots(info, T):
    q1 = info[:, 0:1].astype(jnp.int32)
    q2 = info[:, 1:2].astype(jnp.int32)
    r = lax.broadcasted_iota(jnp.int32, (T, SLOT_ROWS), 1)
    return q1 == r, q2 == r


def _dispatch_kernel(gdst_ref, glen_ref, gsrc_ref, tail0_ref, tailn_ref,
                     h_ref, info_ref, xs_hbm, xbuf, zbuf, sem, *, slack_pieces):
    t = pl.program_id(0)
    n_t = pl.num_programs(0)
    T = h_ref.shape[0]
    slot = t % 2
    tables = (gdst_ref, glen_ref, gsrc_ref)
    hit1, hit2 = _slots(info_ref[...], T)
    onehot = (hit1 | hit2).astype(BF16)
    xbuf[slot] = lax.dot_general(onehot, h_ref[...], (((0,), (0,)), ((), ())),
                                 preferred_element_type=F32)
    _start(_tile_runs(t, tables, xs_hbm, xbuf.at[slot], sem.at[slot], to_hbm=True))

    @pl.when(t > 0)
    def _():
        _wait_rows(_tile_rows(t - 1, tables), xbuf.at[1 - slot], xs_hbm, sem.at[1 - slot])

    @pl.when(t == n_t - 1)
    def _():
        _wait_rows(_tile_rows(t, tables), xbuf.at[slot], xs_hbm, sem.at[slot])

    @pl.when(t == 0)
    def _():
        zbuf[...] = jnp.zeros_like(zbuf)
        tails = []
        for e in range(N_EXPERTS):
            tails += _run_copies(zbuf, xs_hbm, None, tail0_ref[e], tailn_ref[e], TAIL_BITS,
                                 sem.at[2])
        piece = zbuf.shape[0]
        for c in range(slack_pieces):
            d0 = pl.multiple_of(tail0_ref[N_EXPERTS] + c * piece, RUN_ALIGN)
            tails.append((c < tailn_ref[N_EXPERTS], pltpu.make_async_copy(
                zbuf, xs_hbm.at[pl.ds(d0, piece)], sem.at[2])))
        _start(tails)
        _wait(tails)


def _dispatch(h, info, gdst, glen, gsrc, tail0, tailn, rows_max):
    N, D = h.shape
    T = ROUTE_T
    slack_pieces = (rows_max - 2 * N) // ZERO_PIECE_ROWS
    return pl.pallas_call(
        functools.partial(_dispatch_kernel, slack_pieces=slack_pieces),
        out_shape=jax.ShapeDtypeStruct((rows_max, D), F32),
        grid_spec=pltpu.PrefetchScalarGridSpec(
            num_scalar_prefetch=5,
            grid=(N // T,),
            in_specs=[pl.BlockSpec((T, D), lambda t, *_: (t, 0)),
                      pl.BlockSpec((T, E_LANES), lambda t, *_: (t, 0))],
            out_specs=pl.BlockSpec(memory_space=pl.ANY),
            scratch_shapes=[pltpu.VMEM((2, SLOT_ROWS, D), F32),
                            pltpu.VMEM((ZERO_PIECE_ROWS, D), F32),
                            pltpu.SemaphoreType.DMA((3,))]),
        compiler_params=pltpu.CompilerParams(
            dimension_semantics=("arbitrary",),
            vmem_limit_bytes=V7X_VMEM_LIMIT_BYTES),
        name="moe_dispatch",
    )(gdst, glen, gsrc, tail0, tailn, h, info)


def _experts_kernel(te_ref, nv_ref, ns_ref, x_ref, wg_ref, wu_ref, wd_ref, o_ref,
                    xb):
    i = pl.program_id(0)
    k = pl.program_id(1)
    n_sub = ns_ref[i]

    full = x_ref.shape[0] // EXPERT_SUB

    @pl.when((n_sub > 0) & (k == 0))
    def _():
        xb[...] = x_ref[...].astype(BF16)

    def full_tile():
        h = xb[...]
        act = (_silu(_dot(h, wg_ref[0].astype(BF16))) * _dot(h, wu_ref[0].astype(BF16))).astype(BF16)
        return _dot(act, wd_ref[0].astype(BF16))

    @pl.when((n_sub == full) & (k == 0))
    def _():
        o_ref[...] = full_tile()

    @pl.when((n_sub == full) & (k > 0))
    def _():
        o_ref[...] += full_tile()

    @pl.when((n_sub > 0) & (n_sub < full) & (k == 0))
    def _():
        o_ref[...] = jnp.zeros_like(o_ref)

    @pl.when((n_sub > 0) & (n_sub < full))
    def _():
        def piece(s, carry):
            rows = pl.ds(pl.multiple_of(s * EXPERT_SUB, EXPERT_SUB), EXPERT_SUB)
            h = xb[rows, :]
            act = (_silu(_dot(h, wg_ref[0].astype(BF16)))
                   * _dot(h, wu_ref[0].astype(BF16))).astype(BF16)
            o_ref[rows, :] += _dot(act, wd_ref[0].astype(BF16))
            return carry

        lax.fori_loop(0, n_sub, piece, 0)

    @pl.when((n_sub == 0) & (k == 0))
    def _():
        o_ref[...] = jnp.zeros_like(o_ref)


def _experts(xs, tile_expert, n_valid, n_sub, wg, wu, wd):
    M, D = xs.shape
    tm, fk = EXPERT_TM, EXPERT_FK
    E, _, F = wg.shape
    nk = F // fk

    def row_map(i, k, te, nv, ns):
        return (jnp.maximum(jnp.minimum(i, nv[0] - 1), 0), 0)

    def kk(i, k, nv):
        return jnp.where(i < nv[0], k, nk - 1)

    return pl.pallas_call(
        _experts_kernel,
        out_shape=jax.ShapeDtypeStruct((M, D), F32),
        grid_spec=pltpu.PrefetchScalarGridSpec(
            num_scalar_prefetch=3,
            grid=(M // tm, nk),
            in_specs=[
                pl.BlockSpec((tm, D), row_map),
                pl.BlockSpec((1, D, fk), lambda i, k, te, nv, ns: (te[i], 0, kk(i, k, nv))),
                pl.BlockSpec((1, D, fk), lambda i, k, te, nv, ns: (te[i], 0, kk(i, k, nv))),
                pl.BlockSpec((1, fk, D), lambda i, k, te, nv, ns: (te[i], kk(i, k, nv), 0)),
            ],
            out_specs=pl.BlockSpec((tm, D), lambda i, k, te, nv, ns: (i, 0)),
            scratch_shapes=[pltpu.VMEM((tm, D), BF16)]),
        compiler_params=pltpu.CompilerParams(
            dimension_semantics=("arbitrary", "arbitrary"),
            vmem_limit_bytes=V7X_VMEM_LIMIT_BYTES),
        name="moe_experts",
    )(tile_expert, n_valid, n_sub, xs, wg, wu, wd)


def _combine_kernel(gdst_ref, glen_ref, gsrc_ref, x_ref, info_ref, gf_ref, ys_hbm, o_ref,
                    ybuf, sem):
    t = pl.program_id(0)
    n_t = pl.num_programs(0)
    T = x_ref.shape[0]
    slot = t % 2
    tables = (gdst_ref, glen_ref, gsrc_ref)

    def fetch(tile, into):
        ybuf[into, 2 * T:SLOT_ROWS, :] = jnp.zeros((SLOT_ROWS - 2 * T, ybuf.shape[2]), F32)
        _start(_tile_runs(tile, tables, ys_hbm, ybuf.at[into], sem.at[into], to_hbm=False))

    @pl.when(t == 0)
    def _():
        fetch(t, slot)

    @pl.when(t + 1 < n_t)
    def _():
        fetch(t + 1, 1 - slot)

    _wait_rows(_tile_rows(t, tables), ys_hbm, ybuf.at[slot], sem.at[slot])
    y = ybuf[slot].astype(BF16)
    info = info_ref[...]
    hit1, hit2 = _slots(info, T)
    weights = (jnp.where(hit1, info[:, 2:3], 0.0) + jnp.where(hit2, info[:, 3:4], 0.0)).astype(BF16)
    o_ref[...] = _rms(x_ref[...] + _dot(weights, y), gf_ref[...])


def _combine(x, info, gf, ys, gdst, glen, gsrc):
    N, D = x.shape
    T = ROUTE_T
    return pl.pallas_call(
        _combine_kernel,
        out_shape=jax.ShapeDtypeStruct((N, D), F32),
        grid_spec=pltpu.PrefetchScalarGridSpec(
            num_scalar_prefetch=3,
            grid=(N // T,),
            in_specs=[pl.BlockSpec((T, D), lambda t, *_: (t, 0)),
                      pl.BlockSpec((T, E_LANES), lambda t, *_: (t, 0)),
                      pl.BlockSpec(gf.shape, lambda t, *_: (0, 0)),
                      pl.BlockSpec(memory_space=pl.ANY)],
            out_specs=pl.BlockSpec((T, D), lambda t, *_: (t, 0)),
            scratch_shapes=[pltpu.VMEM((2, SLOT_ROWS, D), F32),
                            pltpu.SemaphoreType.DMA((2,))]),
        compiler_params=pltpu.CompilerParams(
            dimension_semantics=("arbitrary",),
            vmem_limit_bytes=V7X_VMEM_LIMIT_BYTES),
        name="moe_combine",
    )(gdst, glen, gsrc, x, info, gf, ys)


def _moe(x, h, info, cnt, wg, wu, wd, gf):
    N, D = x.shape
    n_t = N // ROUTE_T
    tm = EXPERT_TM
    glen = cnt[:, 0, :N_EXPERTS].astype(jnp.int32)
    total = jnp.sum(glen, axis=0)
    gpad = (total + tm - 1) // tm * tm
    gend = jnp.cumsum(gpad)
    goff = gend - gpad
    gdst = goff[None, :] + jnp.cumsum(glen, axis=0) - glen
    gsrc = jnp.cumsum(glen, axis=1) - glen
    rows_max = -(-(2 * N + n_t * N_EXPERTS * (RUN_ALIGN - 1) + N_EXPERTS * (tm - RUN_ALIGN)) // tm) * tm
    n_tiles = rows_max // tm
    n_valid = (gend[-1] // tm).reshape(1)
    tile_row = jnp.minimum(jnp.arange(n_tiles, dtype=jnp.int32), n_valid[0] - 1) * tm
    tile_expert = jnp.sum((tile_row[:, None] >= gend[None, :]).astype(jnp.int32), axis=1)
    tile_rows = jnp.clip((goff + total)[tile_expert] - tile_row, 0, tm)
    tile_rows = jnp.where(jnp.arange(n_tiles) < n_valid[0], tile_rows, 0)
    n_sub = (tile_rows + EXPERT_SUB - 1) // EXPERT_SUB
    flat = lambda a: a.reshape(-1).astype(jnp.int32)
    tail0 = jnp.concatenate([goff + total, gend[-1:]])
    tailn = jnp.concatenate([gpad - total, (rows_max - gend[-1:]) // ZERO_PIECE_ROWS])
    xs = _dispatch(h, info, flat(gdst), flat(glen), flat(gsrc), flat(tail0), flat(tailn), rows_max)
    ys = _experts(xs, flat(tile_expert), flat(n_valid), flat(n_sub), wg, wu, wd)
    return _combine(x, info, gf, ys, flat(gdst), flat(glen), flat(gsrc))


def kernel(x, norm1_g, w_in, pool_w, pool_scale, gm_norm_g, gm_ws, gm_b,
           conv_dw_w, conv_dw_b, conv_ln_g, conv_ln_b, conv_pw_w, conv_pw_b,
           w_out, norm2_g, ffn_wg, ffn_wu, ffn_wd,
           moe_router, moe_wg, moe_wu, moe_wd, final_g):
    B, S, D = x.shape
    depth = w_in.shape[0]
    assert depth == 2, "layer 0 is the dense SwiGLU layer, layer 1 the expert layer + final norm"
    row = lambda t: t.reshape(1, -1)
    for l in range(depth):
        pool_bd = jax.scipy.linalg.block_diag(*[pool_w[l, gi] for gi in range(len(POOL_WINDOWS))])
        gm_wcat = jnp.transpose(gm_ws[l], (1, 0, 2)).reshape(GMLP_BLOCK, GMLP_HEADS * GMLP_BLOCK)
        gm_bias = jnp.repeat(gm_b[l].T, GMLP_HEAD_DIM, axis=1)
        j = l // 2
        expert_layer = l % 2 == 1
        route_params = None
        if expert_layer:
            router_p = jnp.pad(moe_router[j], ((0, 0), (0, E_LANES - N_EXPERTS)))
            route_params = (row(norm2_g[l]), router_p)
        mixed = _mixer(x, row(norm1_g[l]), w_in[l].astype(BF16), pool_bd.astype(BF16),
                       row(pool_scale[l]), row(gm_norm_g[l]), gm_wcat, gm_bias,
                       conv_dw_w[l], row(conv_dw_b[l]), row(conv_ln_g[l]), row(conv_ln_b[l]),
                       conv_pw_w[l].astype(BF16), row(conv_pw_b[l]), w_out[l].astype(BF16),
                       route_params=route_params)
        if expert_layer:
            x, h, info, cnt = mixed
            xf = _moe(x.reshape(B * S, D), h, info, cnt, moe_wg[j], moe_wu[j], moe_wd[j],
                      row(final_g))
        else:
            xf = _ffn(mixed.reshape(B * S, D), row(norm2_g[l]), ffn_wg[j], ffn_wu[j], ffn_wd[j])
        x = xf.reshape(B, S, D)
    return x
```

```python
import functools

import jax
import jax.numpy as jnp
from jax import lax
from jax.experimental import pallas as pl
from jax.experimental.pallas import tpu as pltpu

D_MODEL = 1024
CHUNK = 64
D_POOL = 256
POOL_WINDOWS = (2, 4, 8, 16)
POOL_GROUP = 64
D_GMLP = 384
GMLP_HEADS = 4
GMLP_HEAD_DIM = 96
GMLP_BLOCK = 128
D_CONV = 384
CONV_WIDTH = 31
D_IN = D_POOL + 2 * D_GMLP + 2 * D_CONV
D_FF = 2816
N_EXPERTS = 8
D_FF_EXPERT = 3584
EPS = 1e-6

V7X_VMEM_LIMIT_BYTES = 56 * 1024 * 1024
LANES = 128
SUBLANES = 8

POOL_PAD = SUBLANES
POOL_HALO = POOL_PAD + max(POOL_WINDOWS)
CONV_HALO = 32
MIX_TM = 1024
CONV_ROWS = 256
FFN_TM = 1024
ROUTE_T = 512
EXPERT_TM = 1024
EXPERT_FK = 512
EXPERT_SUB = 256
EXPERT_WEIGHT_SLOTS = 4

BF16 = jnp.bfloat16
F32 = jnp.float32


def _rms(x, g):
    return x * lax.rsqrt(jnp.mean(x * x, axis=-1, keepdims=True) + EPS) * g


def _silu(x):
    return x * jax.nn.sigmoid(x)


def _dot(a, b):
    return jnp.dot(a, b, preferred_element_type=F32)


def _mixer_kernel(x_ref, g1_ref, win_ref, poolw_ref, pools_ref, gmg_ref,
                  gmws_ref, gmb_ref, dww_ref, dwb_ref, lng_ref, lnb_ref,
                  pww_ref, pwb_ref, wout_ref, *rest, route):
    if route:
        g2_ref, router_ref, o_ref, h_ref, info_ref, cnt_ref = rest[:6]
        rest = rest[6:]
    else:
        o_ref = rest[0]
        rest = rest[1:]
    abuf, s2buf, s4buf, s8buf, hbuf, hshift, wtap, ycat = rest
    tm = x_ref.shape[1]
    s = pl.program_id(1)

    @pl.when(s == 0)
    def _():
        abuf[0:POOL_HALO, :] = jnp.zeros((POOL_HALO, D_POOL), F32)
        s2buf[0:POOL_PAD, :] = jnp.zeros((POOL_PAD, D_POOL), F32)
        s4buf[0:POOL_PAD, :] = jnp.zeros((POOL_PAD, D_POOL), F32)
        s8buf[0:POOL_PAD, :] = jnp.zeros((POOL_PAD, LANES), F32)
        hbuf[0:CONV_HALO, :] = jnp.zeros((CONV_HALO, D_CONV), F32)
        for j in range(CONV_WIDTH):
            wtap[j] = jnp.broadcast_to(dww_ref[j:j + 1, :], (SUBLANES, D_CONV))

    x = x_ref[0]
    h = _rms(x, g1_ref[...])
    p = _dot(h.astype(BF16), win_ref[...])
    o1 = D_POOL
    o2 = o1 + D_GMLP
    o3 = o2 + D_GMLP
    o4 = o3 + D_CONV
    u = p[:, o1:o2]
    v = p[:, o2:o3]
    cv = p[:, o3:o4]
    cg = p[:, o4:]

    H = POOL_HALO
    n = tm + H - POOL_PAD
    abuf[H:H + tm, :] = p[:, :o1]
    s2buf[POOL_PAD:POOL_PAD + n, :] = abuf[POOL_PAD:POOL_PAD + n, :] + abuf[POOL_PAD - 1:POOL_PAD - 1 + n, :]
    s4buf[POOL_PAD:POOL_PAD + n, :] = s2buf[POOL_PAD:POOL_PAD + n, :] + s2buf[POOL_PAD - 2:POOL_PAD - 2 + n, :]
    s8buf[POOL_PAD:POOL_PAD + n, :] = (s4buf[POOL_PAD:POOL_PAD + n, LANES:]
                                       + s4buf[POOL_PAD - 4:POOL_PAD - 4 + n, LANES:])
    s16 = s8buf[H:H + tm, :] + s8buf[H - 8:H - 8 + tm, :]
    lane = lax.broadcasted_iota(jnp.int32, (1, LANES), 1)
    row = lax.broadcasted_iota(jnp.int32, (tm, 1), 0)
    tpos = (s * tm + row + 1).astype(F32)
    first = lane < POOL_GROUP
    ys = []
    for lo, (w_small, w_big), s_small, s_big in (
            (0, POOL_WINDOWS[0:2], s2buf[H:H + tm, 0:LANES], s4buf[H:H + tm, 0:LANES]),
            (LANES, POOL_WINDOWS[2:4], s8buf[H:H + tm, :], s16)):
        ssel = jnp.where(first, s_small, s_big)
        win = jnp.where(first, float(w_small), float(w_big))
        mean = ssel / jnp.minimum(tpos, win)
        ys.append(mean - abuf[H:H + tm, lo:lo + LANES])
    y = jnp.concatenate(ys, axis=1).astype(BF16)
    ya = _dot(y, poolw_ref[...]) * pools_ref[...]
    ycat[:, 0:o1] = ya.astype(BF16)
    abuf[POOL_PAD:H, :] = abuf[tm + POOL_PAD:tm + H, :]

    vn = _rms(v, gmg_ref[...])
    ri = lax.broadcasted_iota(jnp.int32, (GMLP_BLOCK, GMLP_HEADS * GMLP_BLOCK), 0)
    cj = lax.broadcasted_iota(jnp.int32, (GMLP_BLOCK, GMLP_HEADS * GMLP_BLOCK), 1)
    causal = (ri // CHUNK) >= ((cj % GMLP_BLOCK) // CHUNK)
    wcat = jnp.where(causal, gmws_ref[...], 0.0).astype(BF16)
    hid = lax.broadcasted_iota(jnp.int32, (1, D_GMLP), 1) // GMLP_HEAD_DIM
    gmb = gmb_ref[...]
    for blk in range(tm // GMLP_BLOCK):
        r0 = blk * GMLP_BLOCK
        vb = vn[r0:r0 + GMLP_BLOCK, :]
        vstack = jnp.concatenate(
            [jnp.where(hid == hh, vb, 0.0) for hh in range(GMLP_HEADS)],
            axis=0).astype(BF16)
        z = _dot(wcat, vstack) + gmb
        ycat[r0:r0 + GMLP_BLOCK, o1:o2] = (u[r0:r0 + GMLP_BLOCK, :] * z).astype(BF16)

    hbuf[CONV_HALO:CONV_HALO + tm, :] = cv * jax.nn.sigmoid(cg)
    base = CONV_HALO - (CONV_WIDTH - 1)
    dwb = dwb_ref[...]
    lng = lng_ref[...]
    lnb = lnb_ref[...]
    pwb = pwb_ref[...]
    for b in range(SUBLANES):
        n_rows = tm + SUBLANES * (len(range(b, CONV_WIDTH, SUBLANES)) - 1)
        hshift[b, 0:n_rows, :] = hbuf[base + b:base + b + n_rows, :]
    for c0 in range(0, tm, CONV_ROWS):
        acc = None
        for j in range(CONV_WIDTH):
            r0 = c0 + SUBLANES * (j // SUBLANES)
            win = hshift[j % SUBLANES, r0:r0 + CONV_ROWS, :].reshape(
                CONV_ROWS // SUBLANES, SUBLANES, D_CONV)
            term = wtap[j] * win
            acc = term if acc is None else acc + term
        acc = acc.reshape(CONV_ROWS, D_CONV) + dwb
        mu = jnp.mean(acc, axis=-1, keepdims=True)
        cen = acc - mu
        var = jnp.mean(cen * cen, axis=-1, keepdims=True)
        ln = cen * lax.rsqrt(var + EPS) * lng + lnb
        yc = _dot(_silu(ln).astype(BF16), pww_ref[...]) + pwb
        ycat[c0:c0 + CONV_ROWS, o2:D_MODEL] = yc.astype(BF16)
    hbuf[0:CONV_HALO, :] = hbuf[tm:tm + CONV_HALO, :]

    o_ref[0] = x + _dot(ycat[...], wout_ref[...])

    if route:
        for r in range(tm // ROUTE_T):
            rows = slice(r * ROUTE_T, (r + 1) * ROUTE_T)
            h_hi, info, padc = _route_tile(o_ref[0, rows, :], g2_ref[...], router_ref[...])
            h_ref[rows, :] = h_hi
            info_ref[rows, :] = info
            cnt_ref[r] = padc


def _const_spec(shape):
    nd = len(shape)
    return pl.BlockSpec(shape, lambda b, s: (0,) * nd, pipeline_mode=pl.Buffered(1))


def _mixer(x, g1, w_in, pool_bd, pool_scale, gm_g, gm_wcat, gm_bias, dw_w, dw_b,
           ln_g, ln_b, pw_w, pw_b, w_out, route_params=None):
    B, S, D = x.shape
    tm = MIX_TM
    tps = S // tm
    weights = (g1, w_in, pool_bd, pool_scale, gm_g, gm_wcat, gm_bias, dw_w, dw_b,
               ln_g, ln_b, pw_w, pw_b, w_out)
    out_shape = jax.ShapeDtypeStruct((B, S, D), F32)
    out_specs = pl.BlockSpec((1, tm, D), lambda b, s: (b, s, 0))
    if route_params is not None:
        weights += tuple(route_params)
        per_step = tm // ROUTE_T
        out_shape = (out_shape,
                     jax.ShapeDtypeStruct((B * S, D), BF16),
                     jax.ShapeDtypeStruct((B * S, E_LANES), F32),
                     jax.ShapeDtypeStruct((B * S // ROUTE_T, 1, E_LANES), F32))
        out_specs = (out_specs,
                     pl.BlockSpec((tm, D), lambda b, s: (b * tps + s, 0)),
                     pl.BlockSpec((tm, E_LANES), lambda b, s: (b * tps + s, 0)),
                     pl.BlockSpec((per_step, 1, E_LANES), lambda b, s: (b * tps + s, 0, 0)))
    return pl.pallas_call(
        functools.partial(_mixer_kernel, route=route_params is not None),
        out_shape=out_shape,
        grid=(B, S // tm),
        in_specs=[pl.BlockSpec((1, tm, D), lambda b, s: (b, s, 0))]
        + [_const_spec(w.shape) for w in weights],
        out_specs=out_specs,
        scratch_shapes=[
            pltpu.VMEM((POOL_HALO + tm, D_POOL), F32),
            pltpu.VMEM((POOL_HALO + tm, D_POOL), F32),
            pltpu.VMEM((POOL_HALO + tm, D_POOL), F32),
            pltpu.VMEM((POOL_HALO + tm, LANES), F32),
            pltpu.VMEM((CONV_HALO + tm, D_CONV), F32),
            pltpu.VMEM((SUBLANES, tm + CONV_HALO - SUBLANES, D_CONV), F32),
            pltpu.VMEM((CONV_WIDTH, SUBLANES, D_CONV), F32),
            pltpu.VMEM((tm, D_MODEL), BF16),
        ],
        compiler_params=pltpu.CompilerParams(
            dimension_semantics=("arbitrary", "arbitrary"),
            vmem_limit_bytes=V7X_VMEM_LIMIT_BYTES),
        name="token_mixer",
    )(x, *weights)


WEIGHT_STAGE_CHUNKS = 16
WEIGHT_STAGE_SLOTS = 4


def _load_bf16(w_hbm, w_bf, stage, sem):
    rows = stage.shape[1]

    def piece(c):
        slot = c % WEIGHT_STAGE_SLOTS
        return pltpu.make_async_copy(w_hbm.at[pl.ds(c * rows, rows)], stage.at[slot], sem.at[slot])

    for c in range(WEIGHT_STAGE_SLOTS - 1):
        piece(c).start()
    for c in range(WEIGHT_STAGE_CHUNKS):
        ahead = c + WEIGHT_STAGE_SLOTS - 1
        if ahead < WEIGHT_STAGE_CHUNKS:
            piece(ahead).start()
        piece(c).wait()
        w_bf[c * rows:(c + 1) * rows, :] = stage[c % WEIGHT_STAGE_SLOTS].astype(BF16)


def _ffn_kernel(x_ref, g_ref, wg_hbm, wu_hbm, wd_hbm, o_ref, wg, wu, wd, stage_in, stage_out, sem):
    @pl.when(pl.program_id(0) == 0)
    def _():
        _load_bf16(wg_hbm, wg, stage_in, sem)
        _load_bf16(wu_hbm, wu, stage_in, sem)
        _load_bf16(wd_hbm, wd, stage_out, sem)

    x = x_ref[...]
    h = _rms(x, g_ref[...]).astype(BF16)
    gate = _dot(h, wg[...])
    up = _dot(h, wu[...])
    act = (_silu(gate) * up).astype(BF16)
    o_ref[...] = x + _dot(act, wd[...])


def _ffn(x, g, wg, wu, wd):
    N, D = x.shape
    F = wg.shape[1]
    tm = FFN_TM
    hbm = pl.BlockSpec(memory_space=pl.ANY)
    return pl.pallas_call(
        _ffn_kernel,
        out_shape=jax.ShapeDtypeStruct((N, D), F32),
        grid=(N // tm,),
        in_specs=[pl.BlockSpec((tm, D), lambda i: (i, 0)),
                  pl.BlockSpec(g.shape, lambda i: (0, 0)), hbm, hbm, hbm],
        out_specs=pl.BlockSpec((tm, D), lambda i: (i, 0)),
        scratch_shapes=[pltpu.VMEM((D, F), BF16), pltpu.VMEM((D, F), BF16), pltpu.VMEM((F, D), BF16),
                        pltpu.VMEM((WEIGHT_STAGE_SLOTS, D // WEIGHT_STAGE_CHUNKS, F), F32),
                        pltpu.VMEM((WEIGHT_STAGE_SLOTS, F // WEIGHT_STAGE_CHUNKS, D), F32),
                        pltpu.SemaphoreType.DMA((WEIGHT_STAGE_SLOTS,))],
        compiler_params=pltpu.CompilerParams(
            dimension_semantics=("arbitrary",),
            vmem_limit_bytes=V7X_VMEM_LIMIT_BYTES),
        name="dense_swiglu",
    )(x, g, wg, wu, wd)


E_LANES = LANES
RUN_ALIGN = SUBLANES
SLOT_ROWS = 2 * ROUTE_T + RUN_ALIGN * N_EXPERTS
RUN_BITS = tuple(range((ROUTE_T // RUN_ALIGN).bit_length() - 1, -1, -1))
ZERO_PIECE_ROWS = EXPERT_TM // 2
TAIL_BITS = tuple(range((ZERO_PIECE_ROWS // RUN_ALIGN).bit_length() - 1, -1, -1))
SLOT_BITS = tuple(range((SLOT_ROWS // RUN_ALIGN).bit_length() - 1, -1, -1))


def _route_tile(x, g, r):
    T = x.shape[0]
    h = _rms(x, g)
    h_hi = h.astype(BF16)
    h_lo = (h - h_hi.astype(F32)).astype(BF16)
    r_hi = r.astype(BF16)
    r_lo = (r - r_hi.astype(F32)).astype(BF16)
    hi_both = _dot(h_hi, jnp.concatenate([r_hi, r_lo], axis=1))
    logits = hi_both[:, :E_LANES] + (_dot(h_lo, r_hi) + hi_both[:, E_LANES:])
    lane = lax.broadcasted_iota(jnp.int32, (T, E_LANES), 1)
    logits = jnp.where(lane < N_EXPERTS, logits, -jnp.inf)
    m1 = jnp.max(logits, axis=-1, keepdims=True)
    i1 = jnp.min(jnp.where(logits == m1, lane, E_LANES), axis=-1, keepdims=True)
    rest = jnp.where(lane == i1, -jnp.inf, logits)
    m2 = jnp.max(rest, axis=-1, keepdims=True)
    i2 = jnp.min(jnp.where(rest == m2, lane, E_LANES), axis=-1, keepdims=True)
    e2 = jnp.exp(m2 - m1)
    w1 = 1.0 / (1.0 + e2)
    w2 = e2 / (1.0 + e2)
    oh1 = (lane == i1).astype(F32)
    oh2 = (lane == i2).astype(F32)
    oh = oh1 + oh2
    ri = lax.broadcasted_iota(jnp.int32, (T, T), 0)
    ci = lax.broadcasted_iota(jnp.int32, (T, T), 1)
    rank = _dot((ri > ci).astype(BF16), oh.astype(BF16))
    cnt = jnp.sum(oh, axis=0, keepdims=True)
    padc = jnp.floor((cnt + (RUN_ALIGN - 1.0)) * (1.0 / RUN_ALIGN)) * RUN_ALIGN
    lane1 = lax.broadcasted_iota(jnp.int32, (1, E_LANES), 1)
    start = jnp.zeros((1, E_LANES), F32)
    for e in range(N_EXPERTS - 1):
        start = start + jnp.where(lane1 > e, padc[:, e:e + 1], 0.0)
    slot = start + rank
    q1 = jnp.sum(oh1 * slot, axis=-1, keepdims=True)
    q2 = jnp.sum(oh2 * slot, axis=-1, keepdims=True)
    info = jnp.where(lane == 0, q1, jnp.where(lane == 1, q2, jnp.where(
        lane == 2, w1, jnp.where(lane == 3, w2, 0.0))))
    return h_hi, info, padc


def _run_copies(src, dst, src0, dst0, rows, bits, sem):
    m = rows // RUN_ALIGN
    off = 0
    out = []
    for b in bits:
        size = RUN_ALIGN << b
        take = (m >> b) & 1
        s0 = 0 if src0 is None else pl.multiple_of(src0 + off, RUN_ALIGN)
        d0 = pl.multiple_of(dst0 + off, RUN_ALIGN)
        out.append((take == 1, pltpu.make_async_copy(
            src.at[pl.ds(s0, size)], dst.at[pl.ds(d0, size)], sem)))
        off = off + take * size
    return out


def _start(copies):
    for pred, cp in copies:
        @pl.when(pred)
        def _(cp=cp):
            cp.start()


def _wait(copies):
    for pred, cp in copies:
        @pl.when(pred)
        def _(cp=cp):
            cp.wait()


def _tile_runs(tile, table_refs, hbm, buf, sem, *, to_hbm):
    gdst_ref, glen_ref, gsrc_ref = table_refs
    copies = []
    for e in range(N_EXPERTS):
        j = tile * N_EXPERTS + e
        if to_hbm:
            copies += _run_copies(buf, hbm, gsrc_ref[j], gdst_ref[j], glen_ref[j], RUN_BITS, sem)
        else:
            copies += _run_copies(hbm, buf, gdst_ref[j], gsrc_ref[j], glen_ref[j], RUN_BITS, sem)
    return copies


def _tile_rows(tile, table_refs):
    _, glen_ref, gsrc_ref = table_refs
    last = tile * N_EXPERTS + N_EXPERTS - 1
    return gsrc_ref[last] + glen_ref[last]


def _wait_rows(rows, src, dst, sem):
    _wait(_run_copies(src, dst, jnp.int32(0), jnp.int32(0), rows, SLOT_BITS, sem))


def _slots(info, T):
    q1 = info[:, 0:1].astype(jnp.int32)
    q2 = info[:, 1:2].astype(jnp.int32)
    r = lax.broadcasted_iota(jnp.int32, (T, SLOT_ROWS), 1)
    return q1 == r, q2 == r


def _dispatch_kernel(gdst_ref, glen_ref, gsrc_ref, tail0_ref, tailn_ref,
                     h_ref, info_ref, xs_hbm, xbuf, zbuf, sem, *, slack_pieces):
    t = pl.program_id(0)
    n_t = pl.num_programs(0)
    T = h_ref.shape[0]
    slot = t % 2
    tables = (gdst_ref, glen_ref, gsrc_ref)
    hit1, hit2 = _slots(info_ref[...], T)
    onehot = (hit1 | hit2).astype(BF16)
    xbuf[slot] = lax.dot_general(onehot, h_ref[...], (((0,), (0,)), ((), ())),
                                 preferred_element_type=F32)
    _start(_tile_runs(t, tables, xs_hbm, xbuf.at[slot], sem.at[slot], to_hbm=True))

    @pl.when(t > 0)
    def _():
        _wait_rows(_tile_rows(t - 1, tables), xbuf.at[1 - slot], xs_hbm, sem.at[1 - slot])

    @pl.when(t == n_t - 1)
    def _():
        _wait_rows(_tile_rows(t, tables), xbuf.at[slot], xs_hbm, sem.at[slot])

    @pl.when(t == 0)
    def _():
        zbuf[...] = jnp.zeros_like(zbuf)
        tails = []
        for e in range(N_EXPERTS):
            tails += _run_copies(zbuf, xs_hbm, None, tail0_ref[e], tailn_ref[e], TAIL_BITS,
                                 sem.at[2])
        piece = zbuf.shape[0]
        for c in range(slack_pieces):
            d0 = pl.multiple_of(tail0_ref[N_EXPERTS] + c * piece, RUN_ALIGN)
            tails.append((c < tailn_ref[N_EXPERTS], pltpu.make_async_copy(
                zbuf, xs_hbm.at[pl.ds(d0, piece)], sem.at[2])))
        _start(tails)
        _wait(tails)


def _dispatch(h, info, gdst, glen, gsrc, tail0, tailn, rows_max):
    N, D = h.shape
    T = ROUTE_T
    slack_pieces = (rows_max - 2 * N) // ZERO_PIECE_ROWS
    return pl.pallas_call(
        functools.partial(_dispatch_kernel, slack_pieces=slack_pieces),
        out_shape=jax.ShapeDtypeStruct((rows_max, D), F32),
        grid_spec=pltpu.PrefetchScalarGridSpec(
            num_scalar_prefetch=5,
            grid=(N // T,),
            in_specs=[pl.BlockSpec((T, D), lambda t, *_: (t, 0)),
                      pl.BlockSpec((T, E_LANES), lambda t, *_: (t, 0))],
            out_specs=pl.BlockSpec(memory_space=pl.ANY),
            scratch_shapes=[pltpu.VMEM((2, SLOT_ROWS, D), F32),
                            pltpu.VMEM((ZERO_PIECE_ROWS, D), F32),
                            pltpu.SemaphoreType.DMA((3,))]),
        compiler_params=pltpu.CompilerParams(
            dimension_semantics=("arbitrary",),
            vmem_limit_bytes=V7X_VMEM_LIMIT_BYTES),
        name="moe_dispatch",
    )(gdst, glen, gsrc, tail0, tailn, h, info)


def _experts_kernel(te_ref, nv_ref, ns_ref, x_ref, wg_hbm, wu_hbm, wd_hbm, o_ref,
                    xb, wgs, wus, wds, sem):
    i = pl.program_id(0)
    k = pl.program_id(1)
    nk = pl.num_programs(1)
    fk = wgs.shape[2]
    n_sub = ns_ref[i]
    step = i * nk + k
    n_steps = nv_ref[0] * nk
    slot = step % EXPERT_WEIGHT_SLOTS

    def chunk_copies(s):
        tile = s // nk
        cols = pl.ds(pl.multiple_of((s - tile * nk) * fk, fk), fk)
        e = te_ref[tile]
        sl = s % EXPERT_WEIGHT_SLOTS
        return (pltpu.make_async_copy(wg_hbm.at[e, :, cols], wgs.at[sl], sem.at[0, sl]),
                pltpu.make_async_copy(wu_hbm.at[e, :, cols], wus.at[sl], sem.at[1, sl]),
                pltpu.make_async_copy(wd_hbm.at[e, cols, :], wds.at[sl], sem.at[2, sl]))

    @pl.when(step == 0)
    def _():
        for s in range(EXPERT_WEIGHT_SLOTS - 1):
            for cp in chunk_copies(s):
                cp.start()

    @pl.when(step + EXPERT_WEIGHT_SLOTS - 1 < n_steps)
    def _():
        for cp in chunk_copies(step + EXPERT_WEIGHT_SLOTS - 1):
            cp.start()

    @pl.when(step < n_steps)
    def _():
        for cp in chunk_copies(step):
            cp.wait()

    wg_ref = wgs.at[slot]
    wu_ref = wus.at[slot]
    wd_ref = wds.at[slot]
    full = x_ref.shape[0] // EXPERT_SUB

    @pl.when((n_sub > 0) & (k == 0))
    def _():
        xb[...] = x_ref[...].astype(BF16)

    def full_tile():
        h = xb[...]
        act = (_silu(_dot(h, wg_ref[...].astype(BF16))) * _dot(h, wu_ref[...].astype(BF16))).astype(BF16)
        return _dot(act, wd_ref[...].astype(BF16))

    @pl.when((n_sub == full) & (k == 0))
    def _():
        o_ref[...] = full_tile()

    @pl.when((n_sub == full) & (k > 0))
    def _():
        o_ref[...] += full_tile()

    @pl.when((n_sub > 0) & (n_sub < full) & (k == 0))
    def _():
        o_ref[...] = jnp.zeros_like(o_ref)

    @pl.when((n_sub > 0) & (n_sub < full))
    def _():
        def piece(s, carry):
            rows = pl.ds(pl.multiple_of(s * EXPERT_SUB, EXPERT_SUB), EXPERT_SUB)
            h = xb[rows, :]
            act = (_silu(_dot(h, wg_ref[...].astype(BF16)))
                   * _dot(h, wu_ref[...].astype(BF16))).astype(BF16)
            o_ref[rows, :] += _dot(act, wd_ref[...].astype(BF16))
            return carry

        lax.fori_loop(0, n_sub, piece, 0)

    @pl.when((n_sub == 0) & (k == 0))
    def _():
        o_ref[...] = jnp.zeros_like(o_ref)


def _experts(xs, tile_expert, n_valid, n_sub, wg, wu, wd):
    M, D = xs.shape
    tm, fk = EXPERT_TM, EXPERT_FK
    E, _, F = wg.shape
    nk = F // fk

    def row_map(i, k, te, nv, ns):
        return (jnp.maximum(jnp.minimum(i, nv[0] - 1), 0), 0)

    hbm = pl.BlockSpec(memory_space=pl.ANY)
    return pl.pallas_call(
        _experts_kernel,
        out_shape=jax.ShapeDtypeStruct((M, D), F32),
        grid_spec=pltpu.PrefetchScalarGridSpec(
            num_scalar_prefetch=3,
            grid=(M // tm, nk),
            in_specs=[pl.BlockSpec((tm, D), row_map), hbm, hbm, hbm],
            out_specs=pl.BlockSpec((tm, D), lambda i, k, te, nv, ns: (i, 0)),
            scratch_shapes=[pltpu.VMEM((tm, D), BF16),
                            pltpu.VMEM((EXPERT_WEIGHT_SLOTS, D, fk), F32),
                            pltpu.VMEM((EXPERT_WEIGHT_SLOTS, D, fk), F32),
                            pltpu.VMEM((EXPERT_WEIGHT_SLOTS, fk, D), F32),
                            pltpu.SemaphoreType.DMA((3, EXPERT_WEIGHT_SLOTS))]),
        compiler_params=pltpu.CompilerParams(
            dimension_semantics=("arbitrary", "arbitrary"),
            vmem_limit_bytes=V7X_VMEM_LIMIT_BYTES),
        name="moe_experts",
    )(tile_expert, n_valid, n_sub, xs, wg, wu, wd)


def _combine_kernel(gdst_ref, glen_ref, gsrc_ref, x_ref, info_ref, gf_ref, ys_hbm, o_ref,
                    ybuf, sem):
    t = pl.program_id(0)
    n_t = pl.num_programs(0)
    T = x_ref.shape[0]
    slot = t % 2
    tables = (gdst_ref, glen_ref, gsrc_ref)

    def fetch(tile, into):
        ybuf[into, 2 * T:SLOT_ROWS, :] = jnp.zeros((SLOT_ROWS - 2 * T, ybuf.shape[2]), F32)
        _start(_tile_runs(tile, tables, ys_hbm, ybuf.at[into], sem.at[into], to_hbm=False))

    @pl.when(t == 0)
    def _():
        fetch(t, slot)

    @pl.when(t + 1 < n_t)
    def _():
        fetch(t + 1, 1 - slot)

    _wait_rows(_tile_rows(t, tables), ys_hbm, ybuf.at[slot], sem.at[slot])
    y = ybuf[slot].astype(BF16)
    info = info_ref[...]
    hit1, hit2 = _slots(info, T)
    weights = (jnp.where(hit1, info[:, 2:3], 0.0) + jnp.where(hit2, info[:, 3:4], 0.0)).astype(BF16)
    o_ref[...] = _rms(x_ref[...] + _dot(weights, y), gf_ref[...])


def _combine(x, info, gf, ys, gdst, glen, gsrc):
    N, D = x.shape
    T = ROUTE_T
    return pl.pallas_call(
        _combine_kernel,
        out_shape=jax.ShapeDtypeStruct((N, D), F32),
        grid_spec=pltpu.PrefetchScalarGridSpec(
            num_scalar_prefetch=3,
            grid=(N // T,),
            in_specs=[pl.BlockSpec((T, D), lambda t, *_: (t, 0)),
                      pl.BlockSpec((T, E_LANES), lambda t, *_: (t, 0)),
                      pl.BlockSpec(gf.shape, lambda t, *_: (0, 0)),
                      pl.BlockSpec(memory_space=pl.ANY)],
            out_specs=pl.BlockSpec((T, D), lambda t, *_: (t, 0)),
            scratch_shapes=[pltpu.VMEM((2, SLOT_ROWS, D), F32),
                            pltpu.SemaphoreType.DMA((2,))]),
        compiler_params=pltpu.CompilerParams(
            dimension_semantics=("arbitrary",),
            vmem_limit_bytes=V7X_VMEM_LIMIT_BYTES),
        name="moe_combine",
    )(gdst, glen, gsrc, x, info, gf, ys)


def _moe(x, h, info, cnt, wg, wu, wd, gf):
    N, D = x.shape
    n_t = N // ROUTE_T
    tm = EXPERT_TM
    glen = cnt[:, 0, :N_EXPERTS].astype(jnp.int32)
    total = jnp.sum(glen, axis=0)
    gpad = (total + tm - 1) // tm * tm
    gend = jnp.cumsum(gpad)
    goff = gend - gpad
    gdst = goff[None, :] + jnp.cumsum(glen, axis=0) - glen
    gsrc = jnp.cumsum(glen, axis=1) - glen
    rows_max = -(-(2 * N + n_t * N_EXPERTS * (RUN_ALIGN - 1) + N_EXPERTS * (tm - RUN_ALIGN)) // tm) * tm
    n_tiles = rows_max // tm
    n_valid = (gend[-1] // tm).reshape(1)
    tile_row = jnp.minimum(jnp.arange(n_tiles, dtype=jnp.int32), n_valid[0] - 1) * tm
    tile_expert = jnp.sum((tile_row[:, None] >= gend[None, :]).astype(jnp.int32), axis=1)
    tile_rows = jnp.clip((goff + total)[tile_expert] - tile_row, 0, tm)
    tile_rows = jnp.where(jnp.arange(n_tiles) < n_valid[0], tile_rows, 0)
    n_sub = (tile_rows + EXPERT_SUB - 1) // EXPERT_SUB
    flat = lambda a: a.reshape(-1).astype(jnp.int32)
    tail0 = jnp.concatenate([goff + total, gend[-1:]])
    tailn = jnp.concatenate([gpad - total, (rows_max - gend[-1:]) // ZERO_PIECE_ROWS])
    xs = _dispatch(h, info, flat(gdst), flat(glen), flat(gsrc), flat(tail0), flat(tailn), rows_max)
    ys = _experts(xs, flat(tile_expert), flat(n_valid), flat(n_sub), wg, wu, wd)
    return _combine(x, info, gf, ys, flat(gdst), flat(glen), flat(gsrc))


def kernel(x, norm1_g, w_in, pool_w, pool_scale, gm_norm_g, gm_ws, gm_b,
           conv_dw_w, conv_dw_b, conv_ln_g, conv_ln_b, conv_pw_w, conv_pw_b,
           w_out, norm2_g, ffn_wg, ffn_wu, ffn_wd,
           moe_router, moe_wg, moe_wu, moe_wd, final_g):
    B, S, D = x.shape
    depth = w_in.shape[0]
    assert depth == 2, "layer 0 is the dense SwiGLU layer, layer 1 the expert layer + final norm"
    row = lambda t: t.reshape(1, -1)
    for l in range(depth):
        pool_bd = jax.scipy.linalg.block_diag(*[pool_w[l, gi] for gi in range(len(POOL_WINDOWS))])
        gm_wcat = jnp.transpose(gm_ws[l], (1, 0, 2)).reshape(GMLP_BLOCK, GMLP_HEADS * GMLP_BLOCK)
        gm_bias = jnp.repeat(gm_b[l].T, GMLP_HEAD_DIM, axis=1)
        j = l // 2
        expert_layer = l % 2 == 1
        route_params = None
        if expert_layer:
            router_p = jnp.pad(moe_router[j], ((0, 0), (0, E_LANES - N_EXPERTS)))
            route_params = (row(norm2_g[l]), router_p)
        mixed = _mixer(x, row(norm1_g[l]), w_in[l].astype(BF16), pool_bd.astype(BF16),
                       row(pool_scale[l]), row(gm_norm_g[l]), gm_wcat, gm_bias,
                       conv_dw_w[l], row(conv_dw_b[l]), row(conv_ln_g[l]), row(conv_ln_b[l]),
                       conv_pw_w[l].astype(BF16), row(conv_pw_b[l]), w_out[l].astype(BF16),
                       route_params=route_params)
        if expert_layer:
            x, h, info, cnt = mixed
            xf = _moe(x.reshape(B * S, D), h, info, cnt, moe_wg[j], moe_wu[j], moe_wd[j],
                      row(final_g))
        else:
            xf = _ffn(mixed.reshape(B * S, D), row(norm2_g[l]), ffn_wg[j], ffn_wu[j], ffn_wd[j])
        x = xf.reshape(B, S, D)
    return x
```

```python
import functools

import jax
import jax.numpy as jnp
from jax import lax
from jax.experimental import pallas as pl
from jax.experimental.pallas import tpu as pltpu

D_MODEL = 1024
CHUNK = 64
D_POOL = 256
POOL_WINDOWS = (2, 4, 8, 16)
POOL_GROUP = 64
D_GMLP = 384
GMLP_HEADS = 4
GMLP_HEAD_DIM = 96
GMLP_BLOCK = 128
D_CONV = 384
CONV_WIDTH = 31
D_IN = D_POOL + 2 * D_GMLP + 2 * D_CONV
D_FF = 2816
N_EXPERTS = 8
D_FF_EXPERT = 3584
EPS = 1e-6

V7X_VMEM_LIMIT_BYTES = 56 * 1024 * 1024
LANES = 128
SUBLANES = 8

POOL_PAD = SUBLANES
POOL_HALO = POOL_PAD + max(POOL_WINDOWS)
CONV_HALO = 32
MIX_TM = 1024
CONV_ROWS = 256
FFN_TM = 1024
ROUTE_T = 512
EXPERT_TM = 1024
EXPERT_FK = 512
EXPERT_SUB = 256
EXPERT_WEIGHT_SLOTS = 5

BF16 = jnp.bfloat16
F32 = jnp.float32


def _rms(x, g):
    return x * lax.rsqrt(jnp.mean(x * x, axis=-1, keepdims=True) + EPS) * g


def _silu(x):
    return x * jax.nn.sigmoid(x)


def _dot(a, b):
    return jnp.dot(a, b, preferred_element_type=F32)


def _mixer_kernel(x_ref, g1_ref, win_ref, poolw_ref, pools_ref, gmg_ref,
                  gmws_ref, gmb_ref, dww_ref, dwb_ref, lng_ref, lnb_ref,
                  pww_ref, pwb_ref, wout_ref, *rest, route):
    if route:
        g2_ref, router_ref, o_ref, h_ref, info_ref, cnt_ref = rest[:6]
        rest = rest[6:]
    else:
        o_ref = rest[0]
        rest = rest[1:]
    abuf, s2buf, s4buf, s8buf, hbuf, hshift, wtap, ycat = rest
    tm = x_ref.shape[1]
    s = pl.program_id(1)

    @pl.when(s == 0)
    def _():
        abuf[0:POOL_HALO, :] = jnp.zeros((POOL_HALO, D_POOL), F32)
        s2buf[0:POOL_PAD, :] = jnp.zeros((POOL_PAD, D_POOL), F32)
        s4buf[0:POOL_PAD, :] = jnp.zeros((POOL_PAD, D_POOL), F32)
        s8buf[0:POOL_PAD, :] = jnp.zeros((POOL_PAD, LANES), F32)
        hbuf[0:CONV_HALO, :] = jnp.zeros((CONV_HALO, D_CONV), F32)
        for j in range(CONV_WIDTH):
            wtap[j] = jnp.broadcast_to(dww_ref[j:j + 1, :], (SUBLANES, D_CONV))

    x = x_ref[0]
    h = _rms(x, g1_ref[...])
    p = _dot(h.astype(BF16), win_ref[...])
    o1 = D_POOL
    o2 = o1 + D_GMLP
    o3 = o2 + D_GMLP
    o4 = o3 + D_CONV
    u = p[:, o1:o2]
    v = p[:, o2:o3]
    cv = p[:, o3:o4]
    cg = p[:, o4:]

    H = POOL_HALO
    n = tm + H - POOL_PAD
    abuf[H:H + tm, :] = p[:, :o1]
    s2buf[POOL_PAD:POOL_PAD + n, :] = abuf[POOL_PAD:POOL_PAD + n, :] + abuf[POOL_PAD - 1:POOL_PAD - 1 + n, :]
    s4buf[POOL_PAD:POOL_PAD + n, :] = s2buf[POOL_PAD:POOL_PAD + n, :] + s2buf[POOL_PAD - 2:POOL_PAD - 2 + n, :]
    s8buf[POOL_PAD:POOL_PAD + n, :] = (s4buf[POOL_PAD:POOL_PAD + n, LANES:]
                                       + s4buf[POOL_PAD - 4:POOL_PAD - 4 + n, LANES:])
    s16 = s8buf[H:H + tm, :] + s8buf[H - 8:H - 8 + tm, :]
    lane = lax.broadcasted_iota(jnp.int32, (1, LANES), 1)
    row = lax.broadcasted_iota(jnp.int32, (tm, 1), 0)
    tpos = (s * tm + row + 1).astype(F32)
    first = lane < POOL_GROUP
    ys = []
    for lo, (w_small, w_big), s_small, s_big in (
            (0, POOL_WINDOWS[0:2], s2buf[H:H + tm, 0:LANES], s4buf[H:H + tm, 0:LANES]),
            (LANES, POOL_WINDOWS[2:4], s8buf[H:H + tm, :], s16)):
        ssel = jnp.where(first, s_small, s_big)
        win = jnp.where(first, float(w_small), float(w_big))
        mean = ssel / jnp.minimum(tpos, win)
        ys.append(mean - abuf[H:H + tm, lo:lo + LANES])
    y = jnp.concatenate(ys, axis=1).astype(BF16)
    ya = _dot(y, poolw_ref[...]) * pools_ref[...]
    ycat[:, 0:o1] = ya.astype(BF16)
    abuf[POOL_PAD:H, :] = abuf[tm + POOL_PAD:tm + H, :]

    vn = _rms(v, gmg_ref[...])
    ri = lax.broadcasted_iota(jnp.int32, (GMLP_BLOCK, GMLP_HEADS * GMLP_BLOCK), 0)
    cj = lax.broadcasted_iota(jnp.int32, (GMLP_BLOCK, GMLP_HEADS * GMLP_BLOCK), 1)
    causal = (ri // CHUNK) >= ((cj % GMLP_BLOCK) // CHUNK)
    wcat = jnp.where(causal, gmws_ref[...], 0.0).astype(BF16)
    hid = lax.broadcasted_iota(jnp.int32, (1, D_GMLP), 1) // GMLP_HEAD_DIM
    gmb = gmb_ref[...]
    for blk in range(tm // GMLP_BLOCK):
        r0 = blk * GMLP_BLOCK
        vb = vn[r0:r0 + GMLP_BLOCK, :]
        vstack = jnp.concatenate(
            [jnp.where(hid == hh, vb, 0.0) for hh in range(GMLP_HEADS)],
            axis=0).astype(BF16)
        z = _dot(wcat, vstack) + gmb
        ycat[r0:r0 + GMLP_BLOCK, o1:o2] = (u[r0:r0 + GMLP_BLOCK, :] * z).astype(BF16)

    hbuf[CONV_HALO:CONV_HALO + tm, :] = cv * jax.nn.sigmoid(cg)
    base = CONV_HALO - (CONV_WIDTH - 1)
    dwb = dwb_ref[...]
    lng = lng_ref[...]
    lnb = lnb_ref[...]
    pwb = pwb_ref[...]
    for b in range(SUBLANES):
        n_rows = tm + SUBLANES * (len(range(b, CONV_WIDTH, SUBLANES)) - 1)
        hshift[b, 0:n_rows, :] = hbuf[base + b:base + b + n_rows, :]
    for c0 in range(0, tm, CONV_ROWS):
        acc = None
        for j in range(CONV_WIDTH):
            r0 = c0 + SUBLANES * (j // SUBLANES)
            win = hshift[j % SUBLANES, r0:r0 + CONV_ROWS, :].reshape(
                CONV_ROWS // SUBLANES, SUBLANES, D_CONV)
            term = wtap[j] * win
            acc = term if acc is None else acc + term
        acc = acc.reshape(CONV_ROWS, D_CONV) + dwb
        mu = jnp.mean(acc, axis=-1, keepdims=True)
        cen = acc - mu
        var = jnp.mean(cen * cen, axis=-1, keepdims=True)
        ln = cen * lax.rsqrt(var + EPS) * lng + lnb
        yc = _dot(_silu(ln).astype(BF16), pww_ref[...]) + pwb
        ycat[c0:c0 + CONV_ROWS, o2:D_MODEL] = yc.astype(BF16)
    hbuf[0:CONV_HALO, :] = hbuf[tm:tm + CONV_HALO, :]

    o_ref[0] = x + _dot(ycat[...], wout_ref[...])

    if route:
        for r in range(tm // ROUTE_T):
            rows = slice(r * ROUTE_T, (r + 1) * ROUTE_T)
            h_hi, info, padc = _route_tile(o_ref[0, rows, :], g2_ref[...], router_ref[...])
            h_ref[rows, :] = h_hi
            info_ref[rows, :] = info
            cnt_ref[r] = padc


def _const_spec(shape):
    nd = len(shape)
    return pl.BlockSpec(shape, lambda b, s: (0,) * nd, pipeline_mode=pl.Buffered(1))


def _mixer(x, g1, w_in, pool_bd, pool_scale, gm_g, gm_wcat, gm_bias, dw_w, dw_b,
           ln_g, ln_b, pw_w, pw_b, w_out, route_params=None):
    B, S, D = x.shape
    tm = MIX_TM
    tps = S // tm
    weights = (g1, w_in, pool_bd, pool_scale, gm_g, gm_wcat, gm_bias, dw_w, dw_b,
               ln_g, ln_b, pw_w, pw_b, w_out)
    out_shape = jax.ShapeDtypeStruct((B, S, D), F32)
    out_specs = pl.BlockSpec((1, tm, D), lambda b, s: (b, s, 0))
    if route_params is not None:
        weights += tuple(route_params)
        per_step = tm // ROUTE_T
        out_shape = (out_shape,
                     jax.ShapeDtypeStruct((B * S, D), BF16),
                     jax.ShapeDtypeStruct((B * S, E_LANES), F32),
                     jax.ShapeDtypeStruct((B * S // ROUTE_T, 1, E_LANES), F32))
        out_specs = (out_specs,
                     pl.BlockSpec((tm, D), lambda b, s: (b * tps + s, 0)),
                     pl.BlockSpec((tm, E_LANES), lambda b, s: (b * tps + s, 0)),
                     pl.BlockSpec((per_step, 1, E_LANES), lambda b, s: (b * tps + s, 0, 0)))
    return pl.pallas_call(
        functools.partial(_mixer_kernel, route=route_params is not None),
        out_shape=out_shape,
        grid=(B, S // tm),
        in_specs=[pl.BlockSpec((1, tm, D), lambda b, s: (b, s, 0))]
        + [_const_spec(w.shape) for w in weights],
        out_specs=out_specs,
        scratch_shapes=[
            pltpu.VMEM((POOL_HALO + tm, D_POOL), F32),
            pltpu.VMEM((POOL_HALO + tm, D_POOL), F32),
            pltpu.VMEM((POOL_HALO + tm, D_POOL), F32),
            pltpu.VMEM((POOL_HALO + tm, LANES), F32),
            pltpu.VMEM((CONV_HALO + tm, D_CONV), F32),
            pltpu.VMEM((SUBLANES, tm + CONV_HALO - SUBLANES, D_CONV), F32),
            pltpu.VMEM((CONV_WIDTH, SUBLANES, D_CONV), F32),
            pltpu.VMEM((tm, D_MODEL), BF16),
        ],
        compiler_params=pltpu.CompilerParams(
            dimension_semantics=("arbitrary", "arbitrary"),
            vmem_limit_bytes=V7X_VMEM_LIMIT_BYTES),
        name="token_mixer",
    )(x, *weights)


WEIGHT_STAGE_CHUNKS = 16
WEIGHT_STAGE_SLOTS = 4


def _load_bf16(w_hbm, w_bf, stage, sem):
    rows = stage.shape[1]

    def piece(c):
        slot = c % WEIGHT_STAGE_SLOTS
        return pltpu.make_async_copy(w_hbm.at[pl.ds(c * rows, rows)], stage.at[slot], sem.at[slot])

    for c in range(WEIGHT_STAGE_SLOTS - 1):
        piece(c).start()
    for c in range(WEIGHT_STAGE_CHUNKS):
        ahead = c + WEIGHT_STAGE_SLOTS - 1
        if ahead < WEIGHT_STAGE_CHUNKS:
            piece(ahead).start()
        piece(c).wait()
        w_bf[c * rows:(c + 1) * rows, :] = stage[c % WEIGHT_STAGE_SLOTS].astype(BF16)


def _ffn_kernel(x_ref, g_ref, wg_hbm, wu_hbm, wd_hbm, o_ref, wg, wu, wd, stage_in, stage_out, sem):
    @pl.when(pl.program_id(0) == 0)
    def _():
        _load_bf16(wg_hbm, wg, stage_in, sem)
        _load_bf16(wu_hbm, wu, stage_in, sem)
        _load_bf16(wd_hbm, wd, stage_out, sem)

    x = x_ref[...]
    h = _rms(x, g_ref[...]).astype(BF16)
    gate = _dot(h, wg[...])
    up = _dot(h, wu[...])
    act = (_silu(gate) * up).astype(BF16)
    o_ref[...] = x + _dot(act, wd[...])


def _ffn(x, g, wg, wu, wd):
    N, D = x.shape
    F = wg.shape[1]
    tm = FFN_TM
    hbm = pl.BlockSpec(memory_space=pl.ANY)
    return pl.pallas_call(
        _ffn_kernel,
        out_shape=jax.ShapeDtypeStruct((N, D), F32),
        grid=(N // tm,),
        in_specs=[pl.BlockSpec((tm, D), lambda i: (i, 0)),
                  pl.BlockSpec(g.shape, lambda i: (0, 0)), hbm, hbm, hbm],
        out_specs=pl.BlockSpec((tm, D), lambda i: (i, 0)),
        scratch_shapes=[pltpu.VMEM((D, F), BF16), pltpu.VMEM((D, F), BF16), pltpu.VMEM((F, D), BF16),
                        pltpu.VMEM((WEIGHT_STAGE_SLOTS, D // WEIGHT_STAGE_CHUNKS, F), F32),
                        pltpu.VMEM((WEIGHT_STAGE_SLOTS, F // WEIGHT_STAGE_CHUNKS, D), F32),
                        pltpu.SemaphoreType.DMA((WEIGHT_STAGE_SLOTS,))],
        compiler_params=pltpu.CompilerParams(
            dimension_semantics=("arbitrary",),
            vmem_limit_bytes=V7X_VMEM_LIMIT_BYTES),
        name="dense_swiglu",
    )(x, g, wg, wu, wd)


E_LANES = LANES
RUN_ALIGN = SUBLANES
SLOT_ROWS = 2 * ROUTE_T + RUN_ALIGN * N_EXPERTS
RUN_BITS = tuple(range((ROUTE_T // RUN_ALIGN).bit_length() - 1, -1, -1))
ZERO_PIECE_ROWS = EXPERT_TM // 2
TAIL_BITS = tuple(range((ZERO_PIECE_ROWS // RUN_ALIGN).bit_length() - 1, -1, -1))
SLOT_BITS = tuple(range((SLOT_ROWS // RUN_ALIGN).bit_length() - 1, -1, -1))


def _route_tile(x, g, r):
    T = x.shape[0]
    h = _rms(x, g)
    h_hi = h.astype(BF16)
    h_lo = (h - h_hi.astype(F32)).astype(BF16)
    r_hi = r.astype(BF16)
    r_lo = (r - r_hi.astype(F32)).astype(BF16)
    hi_both = _dot(h_hi, jnp.concatenate([r_hi, r_lo], axis=1))
    logits = hi_both[:, :E_LANES] + (_dot(h_lo, r_hi) + hi_both[:, E_LANES:])
    lane = lax.broadcasted_iota(jnp.int32, (T, E_LANES), 1)
    logits = jnp.where(lane < N_EXPERTS, logits, -jnp.inf)
    m1 = jnp.max(logits, axis=-1, keepdims=True)
    i1 = jnp.min(jnp.where(logits == m1, lane, E_LANES), axis=-1, keepdims=True)
    rest = jnp.where(lane == i1, -jnp.inf, logits)
    m2 = jnp.max(rest, axis=-1, keepdims=True)
    i2 = jnp.min(jnp.where(rest == m2, lane, E_LANES), axis=-1, keepdims=True)
    e2 = jnp.exp(m2 - m1)
    w1 = 1.0 / (1.0 + e2)
    w2 = e2 / (1.0 + e2)
    oh1 = (lane == i1).astype(F32)
    oh2 = (lane == i2).astype(F32)
    oh = oh1 + oh2
    ri = lax.broadcasted_iota(jnp.int32, (T, T), 0)
    ci = lax.broadcasted_iota(jnp.int32, (T, T), 1)
    rank = _dot((ri > ci).astype(BF16), oh.astype(BF16))
    cnt = jnp.sum(oh, axis=0, keepdims=True)
    padc = jnp.floor((cnt + (RUN_ALIGN - 1.0)) * (1.0 / RUN_ALIGN)) * RUN_ALIGN
    lane1 = lax.broadcasted_iota(jnp.int32, (1, E_LANES), 1)
    start = jnp.zeros((1, E_LANES), F32)
    for e in range(N_EXPERTS - 1):
        start = start + jnp.where(lane1 > e, padc[:, e:e + 1], 0.0)
    slot = start + rank
    q1 = jnp.sum(oh1 * slot, axis=-1, keepdims=True)
    q2 = jnp.sum(oh2 * slot, axis=-1, keepdims=True)
    info = jnp.where(lane == 0, q1, jnp.where(lane == 1, q2, jnp.where(
        lane == 2, w1, jnp.where(lane == 3, w2, 0.0))))
    return h_hi, info, padc


def _run_copies(src, dst, src0, dst0, rows, bits, sem):
    m = rows // RUN_ALIGN
    off = 0
    out = []
    for b in bits:
        size = RUN_ALIGN << b
        take = (m >> b) & 1
        s0 = 0 if src0 is None else pl.multiple_of(src0 + off, RUN_ALIGN)
        d0 = pl.multiple_of(dst0 + off, RUN_ALIGN)
        out.append((take == 1, pltpu.make_async_copy(
            src.at[pl.ds(s0, size)], dst.at[pl.ds(d0, size)], sem)))
        off = off + take * size
    return out


def _start(copies):
    for pred, cp in copies:
        @pl.when(pred)
        def _(cp=cp):
            cp.start()


def _wait(copies):
    for pred, cp in copies:
        @pl.when(pred)
        def _(cp=cp):
            cp.wait()


def _tile_runs(tile, table_refs, hbm, buf, sem, *, to_hbm):
    gdst_ref, glen_ref, gsrc_ref = table_refs
    copies = []
    for e in range(N_EXPERTS):
        j = tile * N_EXPERTS + e
        if to_hbm:
            copies += _run_copies(buf, hbm, gsrc_ref[j], gdst_ref[j], glen_ref[j], RUN_BITS, sem)
        else:
            copies += _run_copies(hbm, buf, gdst_ref[j], gsrc_ref[j], glen_ref[j], RUN_BITS, sem)
    return copies


def _tile_rows(tile, table_refs):
    _, glen_ref, gsrc_ref = table_refs
    last = tile * N_EXPERTS + N_EXPERTS - 1
    return gsrc_ref[last] + glen_ref[last]


def _wait_rows(rows, src, dst, sem):
    _wait(_run_copies(src, dst, jnp.int32(0), jnp.int32(0), rows, SLOT_BITS, sem))


def _slots(info, T):
    q1 = info[:, 0:1].astype(jnp.int32)
    q2 = info[:, 1:2].astype(jnp.int32)
    r = lax.broadcasted_iota(jnp.int32, (T, SLOT_ROWS), 1)
    return q1 == r, q2 == r


def _dispatch_kernel(gdst_ref, glen_ref, gsrc_ref, tail0_ref, tailn_ref,
                     h_ref, info_ref, xs_hbm, xbuf, zbuf, sem, *, slack_pieces):
    t = pl.program_id(0)
    n_t = pl.num_programs(0)
    T = h_ref.shape[0]
    slot = t % 2
    tables = (gdst_ref, glen_ref, gsrc_ref)
    hit1, hit2 = _slots(info_ref[...], T)
    onehot = (hit1 | hit2).astype(BF16)
    xbuf[slot] = lax.dot_general(onehot, h_ref[...], (((0,), (0,)), ((), ())),
                                 preferred_element_type=F32)
    _start(_tile_runs(t, tables, xs_hbm, xbuf.at[slot], sem.at[slot], to_hbm=True))

    @pl.when(t > 0)
    def _():
        _wait_rows(_tile_rows(t - 1, tables), xbuf.at[1 - slot], xs_hbm, sem.at[1 - slot])

    @pl.when(t == n_t - 1)
    def _():
        _wait_rows(_tile_rows(t, tables), xbuf.at[slot], xs_hbm, sem.at[slot])

    @pl.when(t == 0)
    def _():
        zbuf[...] = jnp.zeros_like(zbuf)
        tails = []
        for e in range(N_EXPERTS):
            tails += _run_copies(zbuf, xs_hbm, None, tail0_ref[e], tailn_ref[e], TAIL_BITS,
                                 sem.at[2])
        piece = zbuf.shape[0]
        for c in range(slack_pieces):
            d0 = pl.multiple_of(tail0_ref[N_EXPERTS] + c * piece, RUN_ALIGN)
            tails.append((c < tailn_ref[N_EXPERTS], pltpu.make_async_copy(
                zbuf, xs_hbm.at[pl.ds(d0, piece)], sem.at[2])))
        _start(tails)
        _wait(tails)


def _dispatch(h, info, gdst, glen, gsrc, tail0, tailn, rows_max):
    N, D = h.shape
    T = ROUTE_T
    slack_pieces = (rows_max - 2 * N) // ZERO_PIECE_ROWS
    return pl.pallas_call(
        functools.partial(_dispatch_kernel, slack_pieces=slack_pieces),
        out_shape=jax.ShapeDtypeStruct((rows_max, D), F32),
        grid_spec=pltpu.PrefetchScalarGridSpec(
            num_scalar_prefetch=5,
            grid=(N // T,),
            in_specs=[pl.BlockSpec((T, D), lambda t, *_: (t, 0)),
                      pl.BlockSpec((T, E_LANES), lambda t, *_: (t, 0))],
            out_specs=pl.BlockSpec(memory_space=pl.ANY),
            scratch_shapes=[pltpu.VMEM((2, SLOT_ROWS, D), F32),
                            pltpu.VMEM((ZERO_PIECE_ROWS, D), F32),
                            pltpu.SemaphoreType.DMA((3,))]),
        compiler_params=pltpu.CompilerParams(
            dimension_semantics=("arbitrary",),
            vmem_limit_bytes=V7X_VMEM_LIMIT_BYTES),
        name="moe_dispatch",
    )(gdst, glen, gsrc, tail0, tailn, h, info)


def _experts_kernel(te_ref, nv_ref, ns_ref, x_ref, wg_hbm, wu_hbm, wd_hbm, o_ref,
                    xb, wgs, wus, wds, sem):
    i = pl.program_id(0)
    k = pl.program_id(1)
    nk = pl.num_programs(1)
    fk = wgs.shape[2]
    n_sub = ns_ref[i]
    step = i * nk + k
    n_steps = nv_ref[0] * nk
    slot = step % EXPERT_WEIGHT_SLOTS

    def chunk_copies(s):
        tile = s // nk
        cols = pl.ds(pl.multiple_of((s - tile * nk) * fk, fk), fk)
        e = te_ref[tile]
        sl = s % EXPERT_WEIGHT_SLOTS
        return (pltpu.make_async_copy(wg_hbm.at[e, :, cols], wgs.at[sl], sem.at[0, sl]),
                pltpu.make_async_copy(wu_hbm.at[e, :, cols], wus.at[sl], sem.at[1, sl]),
                pltpu.make_async_copy(wd_hbm.at[e, cols, :], wds.at[sl], sem.at[2, sl]))

    @pl.when(step == 0)
    def _():
        for s in range(EXPERT_WEIGHT_SLOTS - 1):
            for cp in chunk_copies(s):
                cp.start()

    @pl.when(step + EXPERT_WEIGHT_SLOTS - 1 < n_steps)
    def _():
        for cp in chunk_copies(step + EXPERT_WEIGHT_SLOTS - 1):
            cp.start()

    @pl.when(step < n_steps)
    def _():
        for cp in chunk_copies(step):
            cp.wait()

    wg_ref = wgs.at[slot]
    wu_ref = wus.at[slot]
    wd_ref = wds.at[slot]
    full = x_ref.shape[0] // EXPERT_SUB

    @pl.when((n_sub > 0) & (k == 0))
    def _():
        xb[...] = x_ref[...].astype(BF16)

    def full_tile():
        h = xb[...]
        act = (_silu(_dot(h, wg_ref[...].astype(BF16))) * _dot(h, wu_ref[...].astype(BF16))).astype(BF16)
        return _dot(act, wd_ref[...].astype(BF16))

    @pl.when((n_sub == full) & (k == 0))
    def _():
        o_ref[...] = full_tile()

    @pl.when((n_sub == full) & (k > 0))
    def _():
        o_ref[...] += full_tile()

    @pl.when((n_sub > 0) & (n_sub < full) & (k == 0))
    def _():
        o_ref[...] = jnp.zeros_like(o_ref)

    @pl.when((n_sub > 0) & (n_sub < full))
    def _():
        def piece(s, carry):
            rows = pl.ds(pl.multiple_of(s * EXPERT_SUB, EXPERT_SUB), EXPERT_SUB)
            h = xb[rows, :]
            act = (_silu(_dot(h, wg_ref[...].astype(BF16)))
                   * _dot(h, wu_ref[...].astype(BF16))).astype(BF16)
            o_ref[rows, :] += _dot(act, wd_ref[...].astype(BF16))
            return carry

        lax.fori_loop(0, n_sub, piece, 0)

    @pl.when((n_sub == 0) & (k == 0))
    def _():
        o_ref[...] = jnp.zeros_like(o_ref)


def _experts(xs, tile_expert, n_valid, n_sub, wg, wu, wd):
    M, D = xs.shape
    tm, fk = EXPERT_TM, EXPERT_FK
    E, _, F = wg.shape
    nk = F // fk

    def row_map(i, k, te, nv, ns):
        return (jnp.maximum(jnp.minimum(i, nv[0] - 1), 0), 0)

    hbm = pl.BlockSpec(memory_space=pl.ANY)
    return pl.pallas_call(
        _experts_kernel,
        out_shape=jax.ShapeDtypeStruct((M, D), F32),
        grid_spec=pltpu.PrefetchScalarGridSpec(
            num_scalar_prefetch=3,
            grid=(M // tm, nk),
            in_specs=[pl.BlockSpec((tm, D), row_map), hbm, hbm, hbm],
            out_specs=pl.BlockSpec((tm, D), lambda i, k, te, nv, ns: (i, 0)),
            scratch_shapes=[pltpu.VMEM((tm, D), BF16),
                            pltpu.VMEM((EXPERT_WEIGHT_SLOTS, D, fk), F32),
                            pltpu.VMEM((EXPERT_WEIGHT_SLOTS, D, fk), F32),
                            pltpu.VMEM((EXPERT_WEIGHT_SLOTS, fk, D), F32),
                            pltpu.SemaphoreType.DMA((3, EXPERT_WEIGHT_SLOTS))]),
        compiler_params=pltpu.CompilerParams(
            dimension_semantics=("arbitrary", "arbitrary"),
            vmem_limit_bytes=V7X_VMEM_LIMIT_BYTES),
        name="moe_experts",
    )(tile_expert, n_valid, n_sub, xs, wg, wu, wd)


def _combine_kernel(gdst_ref, glen_ref, gsrc_ref, x_ref, info_ref, gf_ref, ys_hbm, o_ref,
                    ybuf, sem):
    t = pl.program_id(0)
    n_t = pl.num_programs(0)
    T = x_ref.shape[0]
    slot = t % 2
    tables = (gdst_ref, glen_ref, gsrc_ref)

    def fetch(tile, into):
        ybuf[into, 2 * T:SLOT_ROWS, :] = jnp.zeros((SLOT_ROWS - 2 * T, ybuf.shape[2]), F32)
        _start(_tile_runs(tile, tables, ys_hbm, ybuf.at[into], sem.at[into], to_hbm=False))

    @pl.when(t == 0)
    def _():
        fetch(t, slot)

    @pl.when(t + 1 < n_t)
    def _():
        fetch(t + 1, 1 - slot)

    _wait_rows(_tile_rows(t, tables), ys_hbm, ybuf.at[slot], sem.at[slot])
    y = ybuf[slot].astype(BF16)
    info = info_ref[...]
    hit1, hit2 = _slots(info, T)
    weights = (jnp.where(hit1, info[:, 2:3], 0.0) + jnp.where(hit2, info[:, 3:4], 0.0)).astype(BF16)
    o_ref[...] = _rms(x_ref[...] + _dot(weights, y), gf_ref[...])


def _combine(x, info, gf, ys, gdst, glen, gsrc):
    N, D = x.shape
    T = ROUTE_T
    return pl.pallas_call(
        _combine_kernel,
        out_shape=jax.ShapeDtypeStruct((N, D), F32),
        grid_spec=pltpu.PrefetchScalarGridSpec(
            num_scalar_prefetch=3,
            grid=(N // T,),
            in_specs=[pl.BlockSpec((T, D), lambda t, *_: (t, 0)),
                      pl.BlockSpec((T, E_LANES), lambda t, *_: (t, 0)),
                      pl.BlockSpec(gf.shape, lambda t, *_: (0, 0)),
                      pl.BlockSpec(memory_space=pl.ANY)],
            out_specs=pl.BlockSpec((T, D), lambda t, *_: (t, 0)),
            scratch_shapes=[pltpu.VMEM((2, SLOT_ROWS, D), F32),
                            pltpu.SemaphoreType.DMA((2,))]),
        compiler_params=pltpu.CompilerParams(
            dimension_semantics=("arbitrary",),
            vmem_limit_bytes=V7X_VMEM_LIMIT_BYTES),
        name="moe_combine",
    )(gdst, glen, gsrc, x, info, gf, ys)


def _moe(x, h, info, cnt, wg, wu, wd, gf):
    N, D = x.shape
    n_t = N // ROUTE_T
    tm = EXPERT_TM
    glen = cnt[:, 0, :N_EXPERTS].astype(jnp.int32)
    total = jnp.sum(glen, axis=0)
    gpad = (total + tm - 1) // tm * tm
    gend = jnp.cumsum(gpad)
    goff = gend - gpad
    gdst = goff[None, :] + jnp.cumsum(glen, axis=0) - glen
    gsrc = jnp.cumsum(glen, axis=1) - glen
    rows_max = -(-(2 * N + n_t * N_EXPERTS * (RUN_ALIGN - 1) + N_EXPERTS * (tm - RUN_ALIGN)) // tm) * tm
    n_tiles = rows_max // tm
    n_valid = (gend[-1] // tm).reshape(1)
    tile_row = jnp.minimum(jnp.arange(n_tiles, dtype=jnp.int32), n_valid[0] - 1) * tm
    tile_expert = jnp.sum((tile_row[:, None] >= gend[None, :]).astype(jnp.int32), axis=1)
    tile_rows = jnp.clip((goff + total)[tile_expert] - tile_row, 0, tm)
    tile_rows = jnp.where(jnp.arange(n_tiles) < n_valid[0], tile_rows, 0)
    n_sub = (tile_rows + EXPERT_SUB - 1) // EXPERT_SUB
    flat = lambda a: a.reshape(-1).astype(jnp.int32)
    tail0 = jnp.concatenate([goff + total, gend[-1:]])
    tailn = jnp.concatenate([gpad - total, (rows_max - gend[-1:]) // ZERO_PIECE_ROWS])
    xs = _dispatch(h, info, flat(gdst), flat(glen), flat(gsrc), flat(tail0), flat(tailn), rows_max)
    ys = _experts(xs, flat(tile_expert), flat(n_valid), flat(n_sub), wg, wu, wd)
    return _combine(x, info, gf, ys, flat(gdst), flat(glen), flat(gsrc))


def kernel(x, norm1_g, w_in, pool_w, pool_scale, gm_norm_g, gm_ws, gm_b,
           conv_dw_w, conv_dw_b, conv_ln_g, conv_ln_b, conv_pw_w, conv_pw_b,
           w_out, norm2_g, ffn_wg, ffn_wu, ffn_wd,
           moe_router, moe_wg, moe_wu, moe_wd, final_g):
    B, S, D = x.shape
    depth = w_in.shape[0]
    assert depth == 2, "layer 0 is the dense SwiGLU layer, layer 1 the expert layer + final norm"
    row = lambda t: t.reshape(1, -1)
    for l in range(depth):
        pool_bd = jax.scipy.linalg.block_diag(*[pool_w[l, gi] for gi in range(len(POOL_WINDOWS))])
        gm_wcat = jnp.transpose(gm_ws[l], (1, 0, 2)).reshape(GMLP_BLOCK, GMLP_HEADS * GMLP_BLOCK)
        gm_bias = jnp.repeat(gm_b[l].T, GMLP_HEAD_DIM, axis=1)
        j = l // 2
        expert_layer = l % 2 == 1
        route_params = None
        if expert_layer:
            router_p = jnp.pad(moe_router[j], ((0, 0), (0, E_LANES - N_EXPERTS)))
            route_params = (row(norm2_g[l]), router_p)
        mixed = _mixer(x, row(norm1_g[l]), w_in[l].astype(BF16), pool_bd.astype(BF16),
                       row(pool_scale[l]), row(gm_norm_g[l]), gm_wcat, gm_bias,
                       conv_dw_w[l], row(conv_dw_b[l]), row(conv_ln_g[l]), row(conv_ln_b[l]),
                       conv_pw_w[l].astype(BF16), row(conv_pw_b[l]), w_out[l].astype(BF16),
                       route_params=route_params)
        if expert_layer:
            x, h, info, cnt = mixed
            xf = _moe(x.reshape(B * S, D), h, info, cnt, moe_wg[j], moe_wu[j], moe_wd[j],
                      row(final_g))
        else:
            xf = _ffn(mixed.reshape(B * S, D), row(norm2_g[l]), ffn_wg[j], ffn_wu[j], ffn_wd[j])
        x = xf.reshape(B, S, D)
    return x
```

```python
import functools

import jax
import jax.numpy as jnp
from jax import lax
from jax.experimental import pallas as pl
from jax.experimental.pallas import tpu as pltpu

D_MODEL = 1024
CHUNK = 64
D_POOL = 256
POOL_WINDOWS = (2, 4, 8, 16)
POOL_GROUP = 64
D_GMLP = 384
GMLP_HEADS = 4
GMLP_HEAD_DIM = 96
GMLP_BLOCK = 128
D_CONV = 384
CONV_WIDTH = 31
D_IN = D_POOL + 2 * D_GMLP + 2 * D_CONV
D_FF = 2816
N_EXPERTS = 8
D_FF_EXPERT = 3584
EPS = 1e-6

V7X_VMEM_LIMIT_BYTES = 56 * 1024 * 1024
LANES = 128
SUBLANES = 8

POOL_PAD = SUBLANES
POOL_HALO = POOL_PAD + max(POOL_WINDOWS)
CONV_HALO = 32
MIX_TM = 1024
CONV_ROWS = 256
FFN_TM = 1024
ROUTE_T = 512
EXPERT_TM = 1024
EXPERT_FK = 512
EXPERT_SUB = 256
EXPERT_WEIGHT_SLOTS = 4
WEIGHT_DMA_PRIORITY = 1

BF16 = jnp.bfloat16
F32 = jnp.float32


def _rms(x, g):
    return x * lax.rsqrt(jnp.mean(x * x, axis=-1, keepdims=True) + EPS) * g


def _silu(x):
    return x * jax.nn.sigmoid(x)


def _dot(a, b):
    return jnp.dot(a, b, preferred_element_type=F32)


def _mixer_kernel(x_ref, g1_ref, win_ref, poolw_ref, pools_ref, gmg_ref,
                  gmws_ref, gmb_ref, dww_ref, dwb_ref, lng_ref, lnb_ref,
                  pww_ref, pwb_ref, wout_ref, *rest, route):
    if route:
        g2_ref, router_ref, o_ref, h_ref, info_ref, cnt_ref = rest[:6]
        rest = rest[6:]
    else:
        o_ref = rest[0]
        rest = rest[1:]
    abuf, s2buf, s4buf, s8buf, hbuf, hshift, wtap, ycat = rest
    tm = x_ref.shape[1]
    s = pl.program_id(1)

    @pl.when(s == 0)
    def _():
        abuf[0:POOL_HALO, :] = jnp.zeros((POOL_HALO, D_POOL), F32)
        s2buf[0:POOL_PAD, :] = jnp.zeros((POOL_PAD, D_POOL), F32)
        s4buf[0:POOL_PAD, :] = jnp.zeros((POOL_PAD, D_POOL), F32)
        s8buf[0:POOL_PAD, :] = jnp.zeros((POOL_PAD, LANES), F32)
        hbuf[0:CONV_HALO, :] = jnp.zeros((CONV_HALO, D_CONV), F32)
        for j in range(CONV_WIDTH):
            wtap[j] = jnp.broadcast_to(dww_ref[j:j + 1, :], (SUBLANES, D_CONV))

    x = x_ref[0]
    h = _rms(x, g1_ref[...])
    p = _dot(h.astype(BF16), win_ref[...])
    o1 = D_POOL
    o2 = o1 + D_GMLP
    o3 = o2 + D_GMLP
    o4 = o3 + D_CONV
    u = p[:, o1:o2]
    v = p[:, o2:o3]
    cv = p[:, o3:o4]
    cg = p[:, o4:]

    H = POOL_HALO
    n = tm + H - POOL_PAD
    abuf[H:H + tm, :] = p[:, :o1]
    s2buf[POOL_PAD:POOL_PAD + n, :] = abuf[POOL_PAD:POOL_PAD + n, :] + abuf[POOL_PAD - 1:POOL_PAD - 1 + n, :]
    s4buf[POOL_PAD:POOL_PAD + n, :] = s2buf[POOL_PAD:POOL_PAD + n, :] + s2buf[POOL_PAD - 2:POOL_PAD - 2 + n, :]
    s8buf[POOL_PAD:POOL_PAD + n, :] = (s4buf[POOL_PAD:POOL_PAD + n, LANES:]
                                       + s4buf[POOL_PAD - 4:POOL_PAD - 4 + n, LANES:])
    s16 = s8buf[H:H + tm, :] + s8buf[H - 8:H - 8 + tm, :]
    lane = lax.broadcasted_iota(jnp.int32, (1, LANES), 1)
    row = lax.broadcasted_iota(jnp.int32, (tm, 1), 0)
    tpos = (s * tm + row + 1).astype(F32)
    first = lane < POOL_GROUP
    ys = []
    for lo, (w_small, w_big), s_small, s_big in (
            (0, POOL_WINDOWS[0:2], s2buf[H:H + tm, 0:LANES], s4buf[H:H + tm, 0:LANES]),
            (LANES, POOL_WINDOWS[2:4], s8buf[H:H + tm, :], s16)):
        ssel = jnp.where(first, s_small, s_big)
        win = jnp.where(first, float(w_small), float(w_big))
        mean = ssel / jnp.minimum(tpos, win)
        ys.append(mean - abuf[H:H + tm, lo:lo + LANES])
    y = jnp.concatenate(ys, axis=1).astype(BF16)
    ya = _dot(y, poolw_ref[...]) * pools_ref[...]
    ycat[:, 0:o1] = ya.astype(BF16)
    abuf[POOL_PAD:H, :] = abuf[tm + POOL_PAD:tm + H, :]

    vn = _rms(v, gmg_ref[...])
    ri = lax.broadcasted_iota(jnp.int32, (GMLP_BLOCK, GMLP_HEADS * GMLP_BLOCK), 0)
    cj = lax.broadcasted_iota(jnp.int32, (GMLP_BLOCK, GMLP_HEADS * GMLP_BLOCK), 1)
    causal = (ri // CHUNK) >= ((cj % GMLP_BLOCK) // CHUNK)
    wcat = jnp.where(causal, gmws_ref[...], 0.0).astype(BF16)
    hid = lax.broadcasted_iota(jnp.int32, (1, D_GMLP), 1) // GMLP_HEAD_DIM
    gmb = gmb_ref[...]
    for blk in range(tm // GMLP_BLOCK):
        r0 = blk * GMLP_BLOCK
        vb = vn[r0:r0 + GMLP_BLOCK, :]
        vstack = jnp.concatenate(
            [jnp.where(hid == hh, vb, 0.0) for hh in range(GMLP_HEADS)],
            axis=0).astype(BF16)
        z = _dot(wcat, vstack) + gmb
        ycat[r0:r0 + GMLP_BLOCK, o1:o2] = (u[r0:r0 + GMLP_BLOCK, :] * z).astype(BF16)

    hbuf[CONV_HALO:CONV_HALO + tm, :] = cv * jax.nn.sigmoid(cg)
    base = CONV_HALO - (CONV_WIDTH - 1)
    dwb = dwb_ref[...]
    lng = lng_ref[...]
    lnb = lnb_ref[...]
    pwb = pwb_ref[...]
    for b in range(SUBLANES):
        n_rows = tm + SUBLANES * (len(range(b, CONV_WIDTH, SUBLANES)) - 1)
        hshift[b, 0:n_rows, :] = hbuf[base + b:base + b + n_rows, :]
    for c0 in range(0, tm, CONV_ROWS):
        acc = None
        for j in range(CONV_WIDTH):
            r0 = c0 + SUBLANES * (j // SUBLANES)
            win = hshift[j % SUBLANES, r0:r0 + CONV_ROWS, :].reshape(
                CONV_ROWS // SUBLANES, SUBLANES, D_CONV)
            term = wtap[j] * win
            acc = term if acc is None else acc + term
        acc = acc.reshape(CONV_ROWS, D_CONV) + dwb
        mu = jnp.mean(acc, axis=-1, keepdims=True)
        cen = acc - mu
        var = jnp.mean(cen * cen, axis=-1, keepdims=True)
        ln = cen * lax.rsqrt(var + EPS) * lng + lnb
        yc = _dot(_silu(ln).astype(BF16), pww_ref[...]) + pwb
        ycat[c0:c0 + CONV_ROWS, o2:D_MODEL] = yc.astype(BF16)
    hbuf[0:CONV_HALO, :] = hbuf[tm:tm + CONV_HALO, :]

    o_ref[0] = x + _dot(ycat[...], wout_ref[...])

    if route:
        for r in range(tm // ROUTE_T):
            rows = slice(r * ROUTE_T, (r + 1) * ROUTE_T)
            h_hi, info, padc = _route_tile(o_ref[0, rows, :], g2_ref[...], router_ref[...])
            h_ref[rows, :] = h_hi
            info_ref[rows, :] = info
            cnt_ref[r] = padc


def _const_spec(shape):
    nd = len(shape)
    return pl.BlockSpec(shape, lambda b, s: (0,) * nd, pipeline_mode=pl.Buffered(1))


def _mixer(x, g1, w_in, pool_bd, pool_scale, gm_g, gm_wcat, gm_bias, dw_w, dw_b,
           ln_g, ln_b, pw_w, pw_b, w_out, route_params=None):
    B, S, D = x.shape
    tm = MIX_TM
    tps = S // tm
    weights = (g1, w_in, pool_bd, pool_scale, gm_g, gm_wcat, gm_bias, dw_w, dw_b,
               ln_g, ln_b, pw_w, pw_b, w_out)
    out_shape = jax.ShapeDtypeStruct((B, S, D), F32)
    out_specs = pl.BlockSpec((1, tm, D), lambda b, s: (b, s, 0))
    if route_params is not None:
        weights += tuple(route_params)
        per_step = tm // ROUTE_T
        out_shape = (out_shape,
                     jax.ShapeDtypeStruct((B * S, D), BF16),
                     jax.ShapeDtypeStruct((B * S, E_LANES), F32),
                     jax.ShapeDtypeStruct((B * S // ROUTE_T, 1, E_LANES), F32))
        out_specs = (out_specs,
                     pl.BlockSpec((tm, D), lambda b, s: (b * tps + s, 0)),
                     pl.BlockSpec((tm, E_LANES), lambda b, s: (b * tps + s, 0)),
                     pl.BlockSpec((per_step, 1, E_LANES), lambda b, s: (b * tps + s, 0, 0)))
    return pl.pallas_call(
        functools.partial(_mixer_kernel, route=route_params is not None),
        out_shape=out_shape,
        grid=(B, S // tm),
        in_specs=[pl.BlockSpec((1, tm, D), lambda b, s: (b, s, 0))]
        + [_const_spec(w.shape) for w in weights],
        out_specs=out_specs,
        scratch_shapes=[
            pltpu.VMEM((POOL_HALO + tm, D_POOL), F32),
            pltpu.VMEM((POOL_HALO + tm, D_POOL), F32),
            pltpu.VMEM((POOL_HALO + tm, D_POOL), F32),
            pltpu.VMEM((POOL_HALO + tm, LANES), F32),
            pltpu.VMEM((CONV_HALO + tm, D_CONV), F32),
            pltpu.VMEM((SUBLANES, tm + CONV_HALO - SUBLANES, D_CONV), F32),
            pltpu.VMEM((CONV_WIDTH, SUBLANES, D_CONV), F32),
            pltpu.VMEM((tm, D_MODEL), BF16),
        ],
        compiler_params=pltpu.CompilerParams(
            dimension_semantics=("arbitrary", "arbitrary"),
            vmem_limit_bytes=V7X_VMEM_LIMIT_BYTES),
        name="token_mixer",
    )(x, *weights)


WEIGHT_STAGE_CHUNKS = 16
WEIGHT_STAGE_SLOTS = 4


def _load_bf16(w_hbm, w_bf, stage, sem):
    rows = stage.shape[1]

    def piece(c):
        slot = c % WEIGHT_STAGE_SLOTS
        return pltpu.make_async_copy(w_hbm.at[pl.ds(c * rows, rows)], stage.at[slot], sem.at[slot])

    for c in range(WEIGHT_STAGE_SLOTS - 1):
        piece(c).start()
    for c in range(WEIGHT_STAGE_CHUNKS):
        ahead = c + WEIGHT_STAGE_SLOTS - 1
        if ahead < WEIGHT_STAGE_CHUNKS:
            piece(ahead).start()
        piece(c).wait()
        w_bf[c * rows:(c + 1) * rows, :] = stage[c % WEIGHT_STAGE_SLOTS].astype(BF16)


def _ffn_kernel(x_ref, g_ref, wg_hbm, wu_hbm, wd_hbm, o_ref, wg, wu, wd, stage_in, stage_out, sem):
    @pl.when(pl.program_id(0) == 0)
    def _():
        _load_bf16(wg_hbm, wg, stage_in, sem)
        _load_bf16(wu_hbm, wu, stage_in, sem)
        _load_bf16(wd_hbm, wd, stage_out, sem)

    x = x_ref[...]
    h = _rms(x, g_ref[...]).astype(BF16)
    gate = _dot(h, wg[...])
    up = _dot(h, wu[...])
    act = (_silu(gate) * up).astype(BF16)
    o_ref[...] = x + _dot(act, wd[...])


def _ffn(x, g, wg, wu, wd):
    N, D = x.shape
    F = wg.shape[1]
    tm = FFN_TM
    hbm = pl.BlockSpec(memory_space=pl.ANY)
    return pl.pallas_call(
        _ffn_kernel,
        out_shape=jax.ShapeDtypeStruct((N, D), F32),
        grid=(N // tm,),
        in_specs=[pl.BlockSpec((tm, D), lambda i: (i, 0)),
                  pl.BlockSpec(g.shape, lambda i: (0, 0)), hbm, hbm, hbm],
        out_specs=pl.BlockSpec((tm, D), lambda i: (i, 0)),
        scratch_shapes=[pltpu.VMEM((D, F), BF16), pltpu.VMEM((D, F), BF16), pltpu.VMEM((F, D), BF16),
                        pltpu.VMEM((WEIGHT_STAGE_SLOTS, D // WEIGHT_STAGE_CHUNKS, F), F32),
                        pltpu.VMEM((WEIGHT_STAGE_SLOTS, F // WEIGHT_STAGE_CHUNKS, D), F32),
                        pltpu.SemaphoreType.DMA((WEIGHT_STAGE_SLOTS,))],
        compiler_params=pltpu.CompilerParams(
            dimension_semantics=("arbitrary",),
            vmem_limit_bytes=V7X_VMEM_LIMIT_BYTES),
        name="dense_swiglu",
    )(x, g, wg, wu, wd)


E_LANES = LANES
RUN_ALIGN = SUBLANES
SLOT_ROWS = 2 * ROUTE_T + RUN_ALIGN * N_EXPERTS
RUN_BITS = tuple(range((ROUTE_T // RUN_ALIGN).bit_length() - 1, -1, -1))
ZERO_PIECE_ROWS = EXPERT_TM // 2
TAIL_BITS = tuple(range((ZERO_PIECE_ROWS // RUN_ALIGN).bit_length() - 1, -1, -1))
SLOT_BITS = tuple(range((SLOT_ROWS // RUN_ALIGN).bit_length() - 1, -1, -1))


def _route_tile(x, g, r):
    T = x.shape[0]
    h = _rms(x, g)
    h_hi = h.astype(BF16)
    h_lo = (h - h_hi.astype(F32)).astype(BF16)
    r_hi = r.astype(BF16)
    r_lo = (r - r_hi.astype(F32)).astype(BF16)
    hi_both = _dot(h_hi, jnp.concatenate([r_hi, r_lo], axis=1))
    logits = hi_both[:, :E_LANES] + (_dot(h_lo, r_hi) + hi_both[:, E_LANES:])
    lane = lax.broadcasted_iota(jnp.int32, (T, E_LANES), 1)
    logits = jnp.where(lane < N_EXPERTS, logits, -jnp.inf)
    m1 = jnp.max(logits, axis=-1, keepdims=True)
    i1 = jnp.min(jnp.where(logits == m1, lane, E_LANES), axis=-1, keepdims=True)
    rest = jnp.where(lane == i1, -jnp.inf, logits)
    m2 = jnp.max(rest, axis=-1, keepdims=True)
    i2 = jnp.min(jnp.where(rest == m2, lane, E_LANES), axis=-1, keepdims=True)
    e2 = jnp.exp(m2 - m1)
    w1 = 1.0 / (1.0 + e2)
    w2 = e2 / (1.0 + e2)
    oh1 = (lane == i1).astype(F32)
    oh2 = (lane == i2).astype(F32)
    oh = oh1 + oh2
    ri = lax.broadcasted_iota(jnp.int32, (T, T), 0)
    ci = lax.broadcasted_iota(jnp.int32, (T, T), 1)
    rank = _dot((ri > ci).astype(BF16), oh.astype(BF16))
    cnt = jnp.sum(oh, axis=0, keepdims=True)
    padc = jnp.floor((cnt + (RUN_ALIGN - 1.0)) * (1.0 / RUN_ALIGN)) * RUN_ALIGN
    lane1 = lax.broadcasted_iota(jnp.int32, (1, E_LANES), 1)
    start = jnp.zeros((1, E_LANES), F32)
    for e in range(N_EXPERTS - 1):
        start = start + jnp.where(lane1 > e, padc[:, e:e + 1], 0.0)
    slot = start + rank
    q1 = jnp.sum(oh1 * slot, axis=-1, keepdims=True)
    q2 = jnp.sum(oh2 * slot, axis=-1, keepdims=True)
    info = jnp.where(lane == 0, q1, jnp.where(lane == 1, q2, jnp.where(
        lane == 2, w1, jnp.where(lane == 3, w2, 0.0))))
    return h_hi, info, padc


def _run_copies(src, dst, src0, dst0, rows, bits, sem):
    m = rows // RUN_ALIGN
    off = 0
    out = []
    for b in bits:
        size = RUN_ALIGN << b
        take = (m >> b) & 1
        s0 = 0 if src0 is None else pl.multiple_of(src0 + off, RUN_ALIGN)
        d0 = pl.multiple_of(dst0 + off, RUN_ALIGN)
        out.append((take == 1, pltpu.make_async_copy(
            src.at[pl.ds(s0, size)], dst.at[pl.ds(d0, size)], sem)))
        off = off + take * size
    return out


def _start(copies):
    for pred, cp in copies:
        @pl.when(pred)
        def _(cp=cp):
            cp.start()


def _wait(copies):
    for pred, cp in copies:
        @pl.when(pred)
        def _(cp=cp):
            cp.wait()


def _tile_runs(tile, table_refs, hbm, buf, sem, *, to_hbm):
    gdst_ref, glen_ref, gsrc_ref = table_refs
    copies = []
    for e in range(N_EXPERTS):
        j = tile * N_EXPERTS + e
        if to_hbm:
            copies += _run_copies(buf, hbm, gsrc_ref[j], gdst_ref[j], glen_ref[j], RUN_BITS, sem)
        else:
            copies += _run_copies(hbm, buf, gdst_ref[j], gsrc_ref[j], glen_ref[j], RUN_BITS, sem)
    return copies


def _tile_rows(tile, table_refs):
    _, glen_ref, gsrc_ref = table_refs
    last = tile * N_EXPERTS + N_EXPERTS - 1
    return gsrc_ref[last] + glen_ref[last]


def _wait_rows(rows, src, dst, sem):
    _wait(_run_copies(src, dst, jnp.int32(0), jnp.int32(0), rows, SLOT_BITS, sem))


def _slots(info, T):
    q1 = info[:, 0:1].astype(jnp.int32)
    q2 = info[:, 1:2].astype(jnp.int32)
    r = lax.broadcasted_iota(jnp.int32, (T, SLOT_ROWS), 1)
    return q1 == r, q2 == r


def _dispatch_kernel(gdst_ref, glen_ref, gsrc_ref, tail0_ref, tailn_ref,
                     h_ref, info_ref, xs_hbm, xbuf, zbuf, sem, *, slack_pieces):
    t = pl.program_id(0)
    n_t = pl.num_programs(0)
    T = h_ref.shape[0]
    slot = t % 2
    tables = (gdst_ref, glen_ref, gsrc_ref)
    hit1, hit2 = _slots(info_ref[...], T)
    onehot = (hit1 | hit2).astype(BF16)
    xbuf[slot] = lax.dot_general(onehot, h_ref[...], (((0,), (0,)), ((), ())),
                                 preferred_element_type=F32)
    _start(_tile_runs(t, tables, xs_hbm, xbuf.at[slot], sem.at[slot], to_hbm=True))

    @pl.when(t > 0)
    def _():
        _wait_rows(_tile_rows(t - 1, tables), xbuf.at[1 - slot], xs_hbm, sem.at[1 - slot])

    @pl.when(t == n_t - 1)
    def _():
        _wait_rows(_tile_rows(t, tables), xbuf.at[slot], xs_hbm, sem.at[slot])

    @pl.when(t == 0)
    def _():
        zbuf[...] = jnp.zeros_like(zbuf)
        tails = []
        for e in range(N_EXPERTS):
            tails += _run_copies(zbuf, xs_hbm, None, tail0_ref[e], tailn_ref[e], TAIL_BITS,
                                 sem.at[2])
        piece = zbuf.shape[0]
        for c in range(slack_pieces):
            d0 = pl.multiple_of(tail0_ref[N_EXPERTS] + c * piece, RUN_ALIGN)
            tails.append((c < tailn_ref[N_EXPERTS], pltpu.make_async_copy(
                zbuf, xs_hbm.at[pl.ds(d0, piece)], sem.at[2])))
        _start(tails)
        _wait(tails)


def _dispatch(h, info, gdst, glen, gsrc, tail0, tailn, rows_max):
    N, D = h.shape
    T = ROUTE_T
    slack_pieces = (rows_max - 2 * N) // ZERO_PIECE_ROWS
    return pl.pallas_call(
        functools.partial(_dispatch_kernel, slack_pieces=slack_pieces),
        out_shape=jax.ShapeDtypeStruct((rows_max, D), F32),
        grid_spec=pltpu.PrefetchScalarGridSpec(
            num_scalar_prefetch=5,
            grid=(N // T,),
            in_specs=[pl.BlockSpec((T, D), lambda t, *_: (t, 0)),
                      pl.BlockSpec((T, E_LANES), lambda t, *_: (t, 0))],
            out_specs=pl.BlockSpec(memory_space=pl.ANY),
            scratch_shapes=[pltpu.VMEM((2, SLOT_ROWS, D), F32),
                            pltpu.VMEM((ZERO_PIECE_ROWS, D), F32),
                            pltpu.SemaphoreType.DMA((3,))]),
        compiler_params=pltpu.CompilerParams(
            dimension_semantics=("arbitrary",),
            vmem_limit_bytes=V7X_VMEM_LIMIT_BYTES),
        name="moe_dispatch",
    )(gdst, glen, gsrc, tail0, tailn, h, info)


def _experts_kernel(te_ref, nv_ref, ns_ref, x_ref, wg_hbm, wu_hbm, wd_hbm, o_ref,
                    xb, wgs, wus, wds, sem):
    i = pl.program_id(0)
    k = pl.program_id(1)
    nk = pl.num_programs(1)
    fk = wgs.shape[2]
    n_sub = ns_ref[i]
    step = i * nk + k
    n_steps = nv_ref[0] * nk
    slot = step % EXPERT_WEIGHT_SLOTS

    def chunk_copies(s):
        tile = s // nk
        cols = pl.ds(pl.multiple_of((s - tile * nk) * fk, fk), fk)
        e = te_ref[tile]
        sl = s % EXPERT_WEIGHT_SLOTS
        return (pltpu.make_async_copy(wg_hbm.at[e, :, cols], wgs.at[sl], sem.at[0, sl]),
                pltpu.make_async_copy(wu_hbm.at[e, :, cols], wus.at[sl], sem.at[1, sl]),
                pltpu.make_async_copy(wd_hbm.at[e, cols, :], wds.at[sl], sem.at[2, sl]))

    @pl.when(step == 0)
    def _():
        for s in range(EXPERT_WEIGHT_SLOTS - 1):
            for cp in chunk_copies(s):
                cp.start(priority=WEIGHT_DMA_PRIORITY)

    @pl.when(step + EXPERT_WEIGHT_SLOTS - 1 < n_steps)
    def _():
        for cp in chunk_copies(step + EXPERT_WEIGHT_SLOTS - 1):
            cp.start(priority=WEIGHT_DMA_PRIORITY)

    @pl.when(step < n_steps)
    def _():
        for cp in chunk_copies(step):
            cp.wait()

    wg_ref = wgs.at[slot]
    wu_ref = wus.at[slot]
    wd_ref = wds.at[slot]
    full = x_ref.shape[0] // EXPERT_SUB

    @pl.when((n_sub > 0) & (k == 0))
    def _():
        xb[...] = x_ref[...].astype(BF16)

    def full_tile():
        h = xb[...]
        act = (_silu(_dot(h, wg_ref[...].astype(BF16))) * _dot(h, wu_ref[...].astype(BF16))).astype(BF16)
        return _dot(act, wd_ref[...].astype(BF16))

    @pl.when((n_sub == full) & (k == 0))
    def _():
        o_ref[...] = full_tile()

    @pl.when((n_sub == full) & (k > 0))
    def _():
        o_ref[...] += full_tile()

    @pl.when((n_sub > 0) & (n_sub < full) & (k == 0))
    def _():
        o_ref[...] = jnp.zeros_like(o_ref)

    @pl.when((n_sub > 0) & (n_sub < full))
    def _():
        def piece(s, carry):
            rows = pl.ds(pl.multiple_of(s * EXPERT_SUB, EXPERT_SUB), EXPERT_SUB)
            h = xb[rows, :]
            act = (_silu(_dot(h, wg_ref[...].astype(BF16)))
                   * _dot(h, wu_ref[...].astype(BF16))).astype(BF16)
            o_ref[rows, :] += _dot(act, wd_ref[...].astype(BF16))
            return carry

        lax.fori_loop(0, n_sub, piece, 0)

    @pl.when((n_sub == 0) & (k == 0))
    def _():
        o_ref[...] = jnp.zeros_like(o_ref)


def _experts(xs, tile_expert, n_valid, n_sub, wg, wu, wd):
    M, D = xs.shape
    tm, fk = EXPERT_TM, EXPERT_FK
    E, _, F = wg.shape
    nk = F // fk

    def row_map(i, k, te, nv, ns):
        return (jnp.maximum(jnp.minimum(i, nv[0] - 1), 0), 0)

    hbm = pl.BlockSpec(memory_space=pl.ANY)
    return pl.pallas_call(
        _experts_kernel,
        out_shape=jax.ShapeDtypeStruct((M, D), F32),
        grid_spec=pltpu.PrefetchScalarGridSpec(
            num_scalar_prefetch=3,
            grid=(M // tm, nk),
            in_specs=[pl.BlockSpec((tm, D), row_map), hbm, hbm, hbm],
            out_specs=pl.BlockSpec((tm, D), lambda i, k, te, nv, ns: (i, 0)),
            scratch_shapes=[pltpu.VMEM((tm, D), BF16),
                            pltpu.VMEM((EXPERT_WEIGHT_SLOTS, D, fk), F32),
                            pltpu.VMEM((EXPERT_WEIGHT_SLOTS, D, fk), F32),
                            pltpu.VMEM((EXPERT_WEIGHT_SLOTS, fk, D), F32),
                            pltpu.SemaphoreType.DMA((3, EXPERT_WEIGHT_SLOTS))]),
        compiler_params=pltpu.CompilerParams(
            dimension_semantics=("arbitrary", "arbitrary"),
            vmem_limit_bytes=V7X_VMEM_LIMIT_BYTES),
        name="moe_experts",
    )(tile_expert, n_valid, n_sub, xs, wg, wu, wd)


def _combine_kernel(gdst_ref, glen_ref, gsrc_ref, x_ref, info_ref, gf_ref, ys_hbm, o_ref,
                    ybuf, sem):
    t = pl.program_id(0)
    n_t = pl.num_programs(0)
    T = x_ref.shape[0]
    slot = t % 2
    tables = (gdst_ref, glen_ref, gsrc_ref)

    def fetch(tile, into):
        ybuf[into, 2 * T:SLOT_ROWS, :] = jnp.zeros((SLOT_ROWS - 2 * T, ybuf.shape[2]), F32)
        _start(_tile_runs(tile, tables, ys_hbm, ybuf.at[into], sem.at[into], to_hbm=False))

    @pl.when(t == 0)
    def _():
        fetch(t, slot)

    @pl.when(t + 1 < n_t)
    def _():
        fetch(t + 1, 1 - slot)

    _wait_rows(_tile_rows(t, tables), ys_hbm, ybuf.at[slot], sem.at[slot])
    y = ybuf[slot].astype(BF16)
    info = info_ref[...]
    hit1, hit2 = _slots(info, T)
    weights = (jnp.where(hit1, info[:, 2:3], 0.0) + jnp.where(hit2, info[:, 3:4], 0.0)).astype(BF16)
    o_ref[...] = _rms(x_ref[...] + _dot(weights, y), gf_ref[...])


def _combine(x, info, gf, ys, gdst, glen, gsrc):
    N, D = x.shape
    T = ROUTE_T
    return pl.pallas_call(
        _combine_kernel,
        out_shape=jax.ShapeDtypeStruct((N, D), F32),
        grid_spec=pltpu.PrefetchScalarGridSpec(
            num_scalar_prefetch=3,
            grid=(N // T,),
            in_specs=[pl.BlockSpec((T, D), lambda t, *_: (t, 0)),
                      pl.BlockSpec((T, E_LANES), lambda t, *_: (t, 0)),
                      pl.BlockSpec(gf.shape, lambda t, *_: (0, 0)),
                      pl.BlockSpec(memory_space=pl.ANY)],
            out_specs=pl.BlockSpec((T, D), lambda t, *_: (t, 0)),
            scratch_shapes=[pltpu.VMEM((2, SLOT_ROWS, D), F32),
                            pltpu.SemaphoreType.DMA((2,))]),
        compiler_params=pltpu.CompilerParams(
            dimension_semantics=("arbitrary",),
            vmem_limit_bytes=V7X_VMEM_LIMIT_BYTES),
        name="moe_combine",
    )(gdst, glen, gsrc, x, info, gf, ys)


def _moe(x, h, info, cnt, wg, wu, wd, gf):
    N, D = x.shape
    n_t = N // ROUTE_T
    tm = EXPERT_TM
    glen = cnt[:, 0, :N_EXPERTS].astype(jnp.int32)
    total = jnp.sum(glen, axis=0)
    gpad = (total + tm - 1) // tm * tm
    gend = jnp.cumsum(gpad)
    goff = gend - gpad
    gdst = goff[None, :] + jnp.cumsum(glen, axis=0) - glen
    gsrc = jnp.cumsum(glen, axis=1) - glen
    rows_max = -(-(2 * N + n_t * N_EXPERTS * (RUN_ALIGN - 1) + N_EXPERTS * (tm - RUN_ALIGN)) // tm) * tm
    n_tiles = rows_max // tm
    n_valid = (gend[-1] // tm).reshape(1)
    tile_row = jnp.minimum(jnp.arange(n_tiles, dtype=jnp.int32), n_valid[0] - 1) * tm
    tile_expert = jnp.sum((tile_row[:, None] >= gend[None, :]).astype(jnp.int32), axis=1)
    tile_rows = jnp.clip((goff + total)[tile_expert] - tile_row, 0, tm)
    tile_rows = jnp.where(jnp.arange(n_tiles) < n_valid[0], tile_rows, 0)
    n_sub = (tile_rows + EXPERT_SUB - 1) // EXPERT_SUB
    flat = lambda a: a.reshape(-1).astype(jnp.int32)
    tail0 = jnp.concatenate([goff + total, gend[-1:]])
    tailn = jnp.concatenate([gpad - total, (rows_max - gend[-1:]) // ZERO_PIECE_ROWS])
    xs = _dispatch(h, info, flat(gdst), flat(glen), flat(gsrc), flat(tail0), flat(tailn), rows_max)
    ys = _experts(xs, flat(tile_expert), flat(n_valid), flat(n_sub), wg, wu, wd)
    return _combine(x, info, gf, ys, flat(gdst), flat(glen), flat(gsrc))


def kernel(x, norm1_g, w_in, pool_w, pool_scale, gm_norm_g, gm_ws, gm_b,
           conv_dw_w, conv_dw_b, conv_ln_g, conv_ln_b, conv_pw_w, conv_pw_b,
           w_out, norm2_g, ffn_wg, ffn_wu, ffn_wd,
           moe_router, moe_wg, moe_wu, moe_wd, final_g):
    B, S, D = x.shape
    depth = w_in.shape[0]
    assert depth == 2, "layer 0 is the dense SwiGLU layer, layer 1 the expert layer + final norm"
    row = lambda t: t.reshape(1, -1)
    for l in range(depth):
        pool_bd = jax.scipy.linalg.block_diag(*[pool_w[l, gi] for gi in range(len(POOL_WINDOWS))])
        gm_wcat = jnp.transpose(gm_ws[l], (1, 0, 2)).reshape(GMLP_BLOCK, GMLP_HEADS * GMLP_BLOCK)
        gm_bias = jnp.repeat(gm_b[l].T, GMLP_HEAD_DIM, axis=1)
        j = l // 2
        expert_layer = l % 2 == 1
        route_params = None
        if expert_layer:
            router_p = jnp.pad(moe_router[j], ((0, 0), (0, E_LANES - N_EXPERTS)))
            route_params = (row(norm2_g[l]), router_p)
        mixed = _mixer(x, row(norm1_g[l]), w_in[l].astype(BF16), pool_bd.astype(BF16),
                       row(pool_scale[l]), row(gm_norm_g[l]), gm_wcat, gm_bias,
                       conv_dw_w[l], row(conv_dw_b[l]), row(conv_ln_g[l]), row(conv_ln_b[l]),
                       conv_pw_w[l].astype(BF16), row(conv_pw_b[l]), w_out[l].astype(BF16),
                       route_params=route_params)
        if expert_layer:
            x, h, info, cnt = mixed
            xf = _moe(x.reshape(B * S, D), h, info, cnt, moe_wg[j], moe_wu[j], moe_wd[j],
                      row(final_g))
        else:
            xf = _ffn(mixed.reshape(B * S, D), row(norm2_g[l]), ffn_wg[j], ffn_wu[j], ffn_wd[j])
        x = xf.reshape(B, S, D)
    return x
```

```python
import functools

import jax
import jax.numpy as jnp
from jax import lax
from jax.experimental import pallas as pl
from jax.experimental.pallas import tpu as pltpu

D_MODEL = 1024
CHUNK = 64
D_POOL = 256
POOL_WINDOWS = (2, 4, 8, 16)
POOL_GROUP = 64
D_GMLP = 384
GMLP_HEADS = 4
GMLP_HEAD_DIM = 96
GMLP_BLOCK = 128
D_CONV = 384
CONV_WIDTH = 31
D_IN = D_POOL + 2 * D_GMLP + 2 * D_CONV
D_FF = 2816
N_EXPERTS = 8
D_FF_EXPERT = 3584
EPS = 1e-6

V7X_VMEM_LIMIT_BYTES = 56 * 1024 * 1024
LANES = 128
SUBLANES = 8

POOL_PAD = SUBLANES
POOL_HALO = POOL_PAD + max(POOL_WINDOWS)
CONV_HALO = 32
MIX_TM = 1024
CONV_ROWS = 256
FFN_TM = 1024
ROUTE_T = 512
EXPERT_TM = 1024
EXPERT_FK = 512
EXPERT_SUB = 256
EXPERT_WEIGHT_SLOTS = 4
WEIGHT_DMA_PRIORITY = 1
RUN_DMA_PRIORITY = 1

BF16 = jnp.bfloat16
F32 = jnp.float32


def _rms(x, g):
    return x * lax.rsqrt(jnp.mean(x * x, axis=-1, keepdims=True) + EPS) * g


def _silu(x):
    return x * jax.nn.sigmoid(x)


def _dot(a, b):
    return jnp.dot(a, b, preferred_element_type=F32)


def _mixer_kernel(x_ref, g1_ref, win_ref, poolw_ref, pools_ref, gmg_ref,
                  gmws_ref, gmb_ref, dww_ref, dwb_ref, lng_ref, lnb_ref,
                  pww_ref, pwb_ref, wout_ref, *rest, route):
    if route:
        g2_ref, router_ref, o_ref, h_ref, info_ref, cnt_ref = rest[:6]
        rest = rest[6:]
    else:
        o_ref = rest[0]
        rest = rest[1:]
    abuf, s2buf, s4buf, s8buf, hbuf, hshift, wtap, ycat = rest
    tm = x_ref.shape[1]
    s = pl.program_id(1)

    @pl.when(s == 0)
    def _():
        abuf[0:POOL_HALO, :] = jnp.zeros((POOL_HALO, D_POOL), F32)
        s2buf[0:POOL_PAD, :] = jnp.zeros((POOL_PAD, D_POOL), F32)
        s4buf[0:POOL_PAD, :] = jnp.zeros((POOL_PAD, D_POOL), F32)
        s8buf[0:POOL_PAD, :] = jnp.zeros((POOL_PAD, LANES), F32)
        hbuf[0:CONV_HALO, :] = jnp.zeros((CONV_HALO, D_CONV), F32)
        for j in range(CONV_WIDTH):
            wtap[j] = jnp.broadcast_to(dww_ref[j:j + 1, :], (SUBLANES, D_CONV))

    x = x_ref[0]
    h = _rms(x, g1_ref[...])
    p = _dot(h.astype(BF16), win_ref[...])
    o1 = D_POOL
    o2 = o1 + D_GMLP
    o3 = o2 + D_GMLP
    o4 = o3 + D_CONV
    u = p[:, o1:o2]
    v = p[:, o2:o3]
    cv = p[:, o3:o4]
    cg = p[:, o4:]

    H = POOL_HALO
    n = tm + H - POOL_PAD
    abuf[H:H + tm, :] = p[:, :o1]
    s2buf[POOL_PAD:POOL_PAD + n, :] = abuf[POOL_PAD:POOL_PAD + n, :] + abuf[POOL_PAD - 1:POOL_PAD - 1 + n, :]
    s4buf[POOL_PAD:POOL_PAD + n, :] = s2buf[POOL_PAD:POOL_PAD + n, :] + s2buf[POOL_PAD - 2:POOL_PAD - 2 + n, :]
    s8buf[POOL_PAD:POOL_PAD + n, :] = (s4buf[POOL_PAD:POOL_PAD + n, LANES:]
                                       + s4buf[POOL_PAD - 4:POOL_PAD - 4 + n, LANES:])
    s16 = s8buf[H:H + tm, :] + s8buf[H - 8:H - 8 + tm, :]
    lane = lax.broadcasted_iota(jnp.int32, (1, LANES), 1)
    row = lax.broadcasted_iota(jnp.int32, (tm, 1), 0)
    tpos = (s * tm + row + 1).astype(F32)
    first = lane < POOL_GROUP
    ys = []
    for lo, (w_small, w_big), s_small, s_big in (
            (0, POOL_WINDOWS[0:2], s2buf[H:H + tm, 0:LANES], s4buf[H:H + tm, 0:LANES]),
            (LANES, POOL_WINDOWS[2:4], s8buf[H:H + tm, :], s16)):
        ssel = jnp.where(first, s_small, s_big)
        win = jnp.where(first, float(w_small), float(w_big))
        mean = ssel / jnp.minimum(tpos, win)
        ys.append(mean - abuf[H:H + tm, lo:lo + LANES])
    y = jnp.concatenate(ys, axis=1).astype(BF16)
    ya = _dot(y, poolw_ref[...]) * pools_ref[...]
    ycat[:, 0:o1] = ya.astype(BF16)
    abuf[POOL_PAD:H, :] = abuf[tm + POOL_PAD:tm + H, :]

    vn = _rms(v, gmg_ref[...])
    ri = lax.broadcasted_iota(jnp.int32, (GMLP_BLOCK, GMLP_HEADS * GMLP_BLOCK), 0)
    cj = lax.broadcasted_iota(jnp.int32, (GMLP_BLOCK, GMLP_HEADS * GMLP_BLOCK), 1)
    causal = (ri // CHUNK) >= ((cj % GMLP_BLOCK) // CHUNK)
    wcat = jnp.where(causal, gmws_ref[...], 0.0).astype(BF16)
    hid = lax.broadcasted_iota(jnp.int32, (1, D_GMLP), 1) // GMLP_HEAD_DIM
    gmb = gmb_ref[...]
    for blk in range(tm // GMLP_BLOCK):
        r0 = blk * GMLP_BLOCK
        vb = vn[r0:r0 + GMLP_BLOCK, :]
        vstack = jnp.concatenate(
            [jnp.where(hid == hh, vb, 0.0) for hh in range(GMLP_HEADS)],
            axis=0).astype(BF16)
        z = _dot(wcat, vstack) + gmb
        ycat[r0:r0 + GMLP_BLOCK, o1:o2] = (u[r0:r0 + GMLP_BLOCK, :] * z).astype(BF16)

    hbuf[CONV_HALO:CONV_HALO + tm, :] = cv * jax.nn.sigmoid(cg)
    base = CONV_HALO - (CONV_WIDTH - 1)
    dwb = dwb_ref[...]
    lng = lng_ref[...]
    lnb = lnb_ref[...]
    pwb = pwb_ref[...]
    for b in range(SUBLANES):
        n_rows = tm + SUBLANES * (len(range(b, CONV_WIDTH, SUBLANES)) - 1)
        hshift[b, 0:n_rows, :] = hbuf[base + b:base + b + n_rows, :]
    for c0 in range(0, tm, CONV_ROWS):
        acc = None
        for j in range(CONV_WIDTH):
            r0 = c0 + SUBLANES * (j // SUBLANES)
            win = hshift[j % SUBLANES, r0:r0 + CONV_ROWS, :].reshape(
                CONV_ROWS // SUBLANES, SUBLANES, D_CONV)
            term = wtap[j] * win
            acc = term if acc is None else acc + term
        acc = acc.reshape(CONV_ROWS, D_CONV) + dwb
        mu = jnp.mean(acc, axis=-1, keepdims=True)
        cen = acc - mu
        var = jnp.mean(cen * cen, axis=-1, keepdims=True)
        ln = cen * lax.rsqrt(var + EPS) * lng + lnb
        yc = _dot(_silu(ln).astype(BF16), pww_ref[...]) + pwb
        ycat[c0:c0 + CONV_ROWS, o2:D_MODEL] = yc.astype(BF16)
    hbuf[0:CONV_HALO, :] = hbuf[tm:tm + CONV_HALO, :]

    o_ref[0] = x + _dot(ycat[...], wout_ref[...])

    if route:
        for r in range(tm // ROUTE_T):
            rows = slice(r * ROUTE_T, (r + 1) * ROUTE_T)
            h_hi, info, padc = _route_tile(o_ref[0, rows, :], g2_ref[...], router_ref[...])
            h_ref[rows, :] = h_hi
            info_ref[rows, :] = info
            cnt_ref[r] = padc


def _const_spec(shape):
    nd = len(shape)
    return pl.BlockSpec(shape, lambda b, s: (0,) * nd, pipeline_mode=pl.Buffered(1))


def _mixer(x, g1, w_in, pool_bd, pool_scale, gm_g, gm_wcat, gm_bias, dw_w, dw_b,
           ln_g, ln_b, pw_w, pw_b, w_out, route_params=None):
    B, S, D = x.shape
    tm = MIX_TM
    tps = S // tm
    weights = (g1, w_in, pool_bd, pool_scale, gm_g, gm_wcat, gm_bias, dw_w, dw_b,
               ln_g, ln_b, pw_w, pw_b, w_out)
    out_shape = jax.ShapeDtypeStruct((B, S, D), F32)
    out_specs = pl.BlockSpec((1, tm, D), lambda b, s: (b, s, 0))
    if route_params is not None:
        weights += tuple(route_params)
        per_step = tm // ROUTE_T
        out_shape = (out_shape,
                     jax.ShapeDtypeStruct((B * S, D), BF16),
                     jax.ShapeDtypeStruct((B * S, E_LANES), F32),
                     jax.ShapeDtypeStruct((B * S // ROUTE_T, 1, E_LANES), F32))
        out_specs = (out_specs,
                     pl.BlockSpec((tm, D), lambda b, s: (b * tps + s, 0)),
                     pl.BlockSpec((tm, E_LANES), lambda b, s: (b * tps + s, 0)),
                     pl.BlockSpec((per_step, 1, E_LANES), lambda b, s: (b * tps + s, 0, 0)))
    return pl.pallas_call(
        functools.partial(_mixer_kernel, route=route_params is not None),
        out_shape=out_shape,
        grid=(B, S // tm),
        in_specs=[pl.BlockSpec((1, tm, D), lambda b, s: (b, s, 0))]
        + [_const_spec(w.shape) for w in weights],
        out_specs=out_specs,
        scratch_shapes=[
            pltpu.VMEM((POOL_HALO + tm, D_POOL), F32),
            pltpu.VMEM((POOL_HALO + tm, D_POOL), F32),
            pltpu.VMEM((POOL_HALO + tm, D_POOL), F32),
            pltpu.VMEM((POOL_HALO + tm, LANES), F32),
            pltpu.VMEM((CONV_HALO + tm, D_CONV), F32),
            pltpu.VMEM((SUBLANES, tm + CONV_HALO - SUBLANES, D_CONV), F32),
            pltpu.VMEM((CONV_WIDTH, SUBLANES, D_CONV), F32),
            pltpu.VMEM((tm, D_MODEL), BF16),
        ],
        compiler_params=pltpu.CompilerParams(
            dimension_semantics=("arbitrary", "arbitrary"),
            vmem_limit_bytes=V7X_VMEM_LIMIT_BYTES),
        name="token_mixer",
    )(x, *weights)


WEIGHT_STAGE_CHUNKS = 16
WEIGHT_STAGE_SLOTS = 4


def _load_bf16(w_hbm, w_bf, stage, sem):
    rows = stage.shape[1]

    def piece(c):
        slot = c % WEIGHT_STAGE_SLOTS
        return pltpu.make_async_copy(w_hbm.at[pl.ds(c * rows, rows)], stage.at[slot], sem.at[slot])

    for c in range(WEIGHT_STAGE_SLOTS - 1):
        piece(c).start()
    for c in range(WEIGHT_STAGE_CHUNKS):
        ahead = c + WEIGHT_STAGE_SLOTS - 1
        if ahead < WEIGHT_STAGE_CHUNKS:
            piece(ahead).start()
        piece(c).wait()
        w_bf[c * rows:(c + 1) * rows, :] = stage[c % WEIGHT_STAGE_SLOTS].astype(BF16)


def _ffn_kernel(x_ref, g_ref, wg_hbm, wu_hbm, wd_hbm, o_ref, wg, wu, wd, stage_in, stage_out, sem):
    @pl.when(pl.program_id(0) == 0)
    def _():
        _load_bf16(wg_hbm, wg, stage_in, sem)
        _load_bf16(wu_hbm, wu, stage_in, sem)
        _load_bf16(wd_hbm, wd, stage_out, sem)

    x = x_ref[...]
    h = _rms(x, g_ref[...]).astype(BF16)
    gate = _dot(h, wg[...])
    up = _dot(h, wu[...])
    act = (_silu(gate) * up).astype(BF16)
    o_ref[...] = x + _dot(act, wd[...])


def _ffn(x, g, wg, wu, wd):
    N, D = x.shape
    F = wg.shape[1]
    tm = FFN_TM
    hbm = pl.BlockSpec(memory_space=pl.ANY)
    return pl.pallas_call(
        _ffn_kernel,
        out_shape=jax.ShapeDtypeStruct((N, D), F32),
        grid=(N // tm,),
        in_specs=[pl.BlockSpec((tm, D), lambda i: (i, 0)),
                  pl.BlockSpec(g.shape, lambda i: (0, 0)), hbm, hbm, hbm],
        out_specs=pl.BlockSpec((tm, D), lambda i: (i, 0)),
        scratch_shapes=[pltpu.VMEM((D, F), BF16), pltpu.VMEM((D, F), BF16), pltpu.VMEM((F, D), BF16),
                        pltpu.VMEM((WEIGHT_STAGE_SLOTS, D // WEIGHT_STAGE_CHUNKS, F), F32),
                        pltpu.VMEM((WEIGHT_STAGE_SLOTS, F // WEIGHT_STAGE_CHUNKS, D), F32),
                        pltpu.SemaphoreType.DMA((WEIGHT_STAGE_SLOTS,))],
        compiler_params=pltpu.CompilerParams(
            dimension_semantics=("arbitrary",),
            vmem_limit_bytes=V7X_VMEM_LIMIT_BYTES),
        name="dense_swiglu",
    )(x, g, wg, wu, wd)


E_LANES = LANES
RUN_ALIGN = SUBLANES
SLOT_ROWS = 2 * ROUTE_T + RUN_ALIGN * N_EXPERTS
RUN_BITS = tuple(range((ROUTE_T // RUN_ALIGN).bit_length() - 1, -1, -1))
ZERO_PIECE_ROWS = EXPERT_TM // 2
TAIL_BITS = tuple(range((ZERO_PIECE_ROWS // RUN_ALIGN).bit_length() - 1, -1, -1))
SLOT_BITS = tuple(range((SLOT_ROWS // RUN_ALIGN).bit_length() - 1, -1, -1))


def _route_tile(x, g, r):
    T = x.shape[0]
    h = _rms(x, g)
    h_hi = h.astype(BF16)
    h_lo = (h - h_hi.astype(F32)).astype(BF16)
    r_hi = r.astype(BF16)
    r_lo = (r - r_hi.astype(F32)).astype(BF16)
    hi_both = _dot(h_hi, jnp.concatenate([r_hi, r_lo], axis=1))
    logits = hi_both[:, :E_LANES] + (_dot(h_lo, r_hi) + hi_both[:, E_LANES:])
    lane = lax.broadcasted_iota(jnp.int32, (T, E_LANES), 1)
    logits = jnp.where(lane < N_EXPERTS, logits, -jnp.inf)
    m1 = jnp.max(logits, axis=-1, keepdims=True)
    i1 = jnp.min(jnp.where(logits == m1, lane, E_LANES), axis=-1, keepdims=True)
    rest = jnp.where(lane == i1, -jnp.inf, logits)
    m2 = jnp.max(rest, axis=-1, keepdims=True)
    i2 = jnp.min(jnp.where(rest == m2, lane, E_LANES), axis=-1, keepdims=True)
    e2 = jnp.exp(m2 - m1)
    w1 = 1.0 / (1.0 + e2)
    w2 = e2 / (1.0 + e2)
    oh1 = (lane == i1).astype(F32)
    oh2 = (lane == i2).astype(F32)
    oh = oh1 + oh2
    ri = lax.broadcasted_iota(jnp.int32, (T, T), 0)
    ci = lax.broadcasted_iota(jnp.int32, (T, T), 1)
    rank = _dot((ri > ci).astype(BF16), oh.astype(BF16))
    cnt = jnp.sum(oh, axis=0, keepdims=True)
    padc = jnp.floor((cnt + (RUN_ALIGN - 1.0)) * (1.0 / RUN_ALIGN)) * RUN_ALIGN
    lane1 = lax.broadcasted_iota(jnp.int32, (1, E_LANES), 1)
    start = jnp.zeros((1, E_LANES), F32)
    for e in range(N_EXPERTS - 1):
        start = start + jnp.where(lane1 > e, padc[:, e:e + 1], 0.0)
    slot = start + rank
    q1 = jnp.sum(oh1 * slot, axis=-1, keepdims=True)
    q2 = jnp.sum(oh2 * slot, axis=-1, keepdims=True)
    info = jnp.where(lane == 0, q1, jnp.where(lane == 1, q2, jnp.where(
        lane == 2, w1, jnp.where(lane == 3, w2, 0.0))))
    return h_hi, info, padc


def _run_copies(src, dst, src0, dst0, rows, bits, sem):
    m = rows // RUN_ALIGN
    off = 0
    out = []
    for b in bits:
        size = RUN_ALIGN << b
        take = (m >> b) & 1
        s0 = 0 if src0 is None else pl.multiple_of(src0 + off, RUN_ALIGN)
        d0 = pl.multiple_of(dst0 + off, RUN_ALIGN)
        out.append((take == 1, pltpu.make_async_copy(
            src.at[pl.ds(s0, size)], dst.at[pl.ds(d0, size)], sem)))
        off = off + take * size
    return out


def _start(copies):
    for pred, cp in copies:
        @pl.when(pred)
        def _(cp=cp):
            cp.start(priority=RUN_DMA_PRIORITY)


def _wait(copies):
    for pred, cp in copies:
        @pl.when(pred)
        def _(cp=cp):
            cp.wait()


def _tile_runs(tile, table_refs, hbm, buf, sem, *, to_hbm):
    gdst_ref, glen_ref, gsrc_ref = table_refs
    copies = []
    for e in range(N_EXPERTS):
        j = tile * N_EXPERTS + e
        if to_hbm:
            copies += _run_copies(buf, hbm, gsrc_ref[j], gdst_ref[j], glen_ref[j], RUN_BITS, sem)
        else:
            copies += _run_copies(hbm, buf, gdst_ref[j], gsrc_ref[j], glen_ref[j], RUN_BITS, sem)
    return copies


def _tile_rows(tile, table_refs):
    _, glen_ref, gsrc_ref = table_refs
    last = tile * N_EXPERTS + N_EXPERTS - 1
    return gsrc_ref[last] + glen_ref[last]


def _wait_rows(rows, src, dst, sem):
    _wait(_run_copies(src, dst, jnp.int32(0), jnp.int32(0), rows, SLOT_BITS, sem))


def _slots(info, T):
    q1 = info[:, 0:1].astype(jnp.int32)
    q2 = info[:, 1:2].astype(jnp.int32)
    r = lax.broadcasted_iota(jnp.int32, (T, SLOT_ROWS), 1)
    return q1 == r, q2 == r


def _dispatch_kernel(gdst_ref, glen_ref, gsrc_ref, tail0_ref, tailn_ref,
                     h_ref, info_ref, xs_hbm, xbuf, zbuf, sem, *, slack_pieces):
    t = pl.program_id(0)
    n_t = pl.num_programs(0)
    T = h_ref.shape[0]
    slot = t % 2
    tables = (gdst_ref, glen_ref, gsrc_ref)
    hit1, hit2 = _slots(info_ref[...], T)
    onehot = (hit1 | hit2).astype(BF16)
    xbuf[slot] = lax.dot_general(onehot, h_ref[...], (((0,), (0,)), ((), ())),
                                 preferred_element_type=F32)
    _start(_tile_runs(t, tables, xs_hbm, xbuf.at[slot], sem.at[slot], to_hbm=True))

    @pl.when(t > 0)
    def _():
        _wait_rows(_tile_rows(t - 1, tables), xbuf.at[1 - slot], xs_hbm, sem.at[1 - slot])

    @pl.when(t == n_t - 1)
    def _():
        _wait_rows(_tile_rows(t, tables), xbuf.at[slot], xs_hbm, sem.at[slot])

    @pl.when(t == 0)
    def _():
        zbuf[...] = jnp.zeros_like(zbuf)
        tails = []
        for e in range(N_EXPERTS):
            tails += _run_copies(zbuf, xs_hbm, None, tail0_ref[e], tailn_ref[e], TAIL_BITS,
                                 sem.at[2])
        piece = zbuf.shape[0]
        for c in range(slack_pieces):
            d0 = pl.multiple_of(tail0_ref[N_EXPERTS] + c * piece, RUN_ALIGN)
            tails.append((c < tailn_ref[N_EXPERTS], pltpu.make_async_copy(
                zbuf, xs_hbm.at[pl.ds(d0, piece)], sem.at[2])))
        _start(tails)
        _wait(tails)


def _dispatch(h, info, gdst, glen, gsrc, tail0, tailn, rows_max):
    N, D = h.shape
    T = ROUTE_T
    slack_pieces = (rows_max - 2 * N) // ZERO_PIECE_ROWS
    return pl.pallas_call(
        functools.partial(_dispatch_kernel, slack_pieces=slack_pieces),
        out_shape=jax.ShapeDtypeStruct((rows_max, D), F32),
        grid_spec=pltpu.PrefetchScalarGridSpec(
            num_scalar_prefetch=5,
            grid=(N // T,),
            in_specs=[pl.BlockSpec((T, D), lambda t, *_: (t, 0)),
                      pl.BlockSpec((T, E_LANES), lambda t, *_: (t, 0))],
            out_specs=pl.BlockSpec(memory_space=pl.ANY),
            scratch_shapes=[pltpu.VMEM((2, SLOT_ROWS, D), F32),
                            pltpu.VMEM((ZERO_PIECE_ROWS, D), F32),
                            pltpu.SemaphoreType.DMA((3,))]),
        compiler_params=pltpu.CompilerParams(
            dimension_semantics=("arbitrary",),
            vmem_limit_bytes=V7X_VMEM_LIMIT_BYTES),
        name="moe_dispatch",
    )(gdst, glen, gsrc, tail0, tailn, h, info)


def _experts_kernel(te_ref, nv_ref, ns_ref, x_ref, wg_hbm, wu_hbm, wd_hbm, o_ref,
                    xb, wgs, wus, wds, sem):
    i = pl.program_id(0)
    k = pl.program_id(1)
    nk = pl.num_programs(1)
    fk = wgs.shape[2]
    n_sub = ns_ref[i]
    step = i * nk + k
    n_steps = nv_ref[0] * nk
    slot = step % EXPERT_WEIGHT_SLOTS

    def chunk_copies(s):
        tile = s // nk
        cols = pl.ds(pl.multiple_of((s - tile * nk) * fk, fk), fk)
        e = te_ref[tile]
        sl = s % EXPERT_WEIGHT_SLOTS
        return (pltpu.make_async_copy(wg_hbm.at[e, :, cols], wgs.at[sl], sem.at[0, sl]),
                pltpu.make_async_copy(wu_hbm.at[e, :, cols], wus.at[sl], sem.at[1, sl]),
                pltpu.make_async_copy(wd_hbm.at[e, cols, :], wds.at[sl], sem.at[2, sl]))

    @pl.when(step == 0)
    def _():
        for s in range(EXPERT_WEIGHT_SLOTS - 1):
            for cp in chunk_copies(s):
                cp.start(priority=WEIGHT_DMA_PRIORITY)

    @pl.when(step + EXPERT_WEIGHT_SLOTS - 1 < n_steps)
    def _():
        for cp in chunk_copies(step + EXPERT_WEIGHT_SLOTS - 1):
            cp.start(priority=WEIGHT_DMA_PRIORITY)

    @pl.when(step < n_steps)
    def _():
        for cp in chunk_copies(step):
            cp.wait()

    wg_ref = wgs.at[slot]
    wu_ref = wus.at[slot]
    wd_ref = wds.at[slot]
    full = x_ref.shape[0] // EXPERT_SUB

    @pl.when((n_sub > 0) & (k == 0))
    def _():
        xb[...] = x_ref[...].astype(BF16)

    def full_tile():
        h = xb[...]
        act = (_silu(_dot(h, wg_ref[...].astype(BF16))) * _dot(h, wu_ref[...].astype(BF16))).astype(BF16)
        return _dot(act, wd_ref[...].astype(BF16))

    @pl.when((n_sub == full) & (k == 0))
    def _():
        o_ref[...] = full_tile()

    @pl.when((n_sub == full) & (k > 0))
    def _():
        o_ref[...] += full_tile()

    @pl.when((n_sub > 0) & (n_sub < full) & (k == 0))
    def _():
        o_ref[...] = jnp.zeros_like(o_ref)

    @pl.when((n_sub > 0) & (n_sub < full))
    def _():
        def piece(s, carry):
            rows = pl.ds(pl.multiple_of(s * EXPERT_SUB, EXPERT_SUB), EXPERT_SUB)
            h = xb[rows, :]
            act = (_silu(_dot(h, wg_ref[...].astype(BF16)))
                   * _dot(h, wu_ref[...].astype(BF16))).astype(BF16)
            o_ref[rows, :] += _dot(act, wd_ref[...].astype(BF16))
            return carry

        lax.fori_loop(0, n_sub, piece, 0)

    @pl.when((n_sub == 0) & (k == 0))
    def _():
        o_ref[...] = jnp.zeros_like(o_ref)


def _experts(xs, tile_expert, n_valid, n_sub, wg, wu, wd):
    M, D = xs.shape
    tm, fk = EXPERT_TM, EXPERT_FK
    E, _, F = wg.shape
    nk = F // fk

    def row_map(i, k, te, nv, ns):
        return (jnp.maximum(jnp.minimum(i, nv[0] - 1), 0), 0)

    hbm = pl.BlockSpec(memory_space=pl.ANY)
    return pl.pallas_call(
        _experts_kernel,
        out_shape=jax.ShapeDtypeStruct((M, D), F32),
        grid_spec=pltpu.PrefetchScalarGridSpec(
            num_scalar_prefetch=3,
            grid=(M // tm, nk),
            in_specs=[pl.BlockSpec((tm, D), row_map), hbm, hbm, hbm],
            out_specs=pl.BlockSpec((tm, D), lambda i, k, te, nv, ns: (i, 0)),
            scratch_shapes=[pltpu.VMEM((tm, D), BF16),
                            pltpu.VMEM((EXPERT_WEIGHT_SLOTS, D, fk), F32),
                            pltpu.VMEM((EXPERT_WEIGHT_SLOTS, D, fk), F32),
                            pltpu.VMEM((EXPERT_WEIGHT_SLOTS, fk, D), F32),
                            pltpu.SemaphoreType.DMA((3, EXPERT_WEIGHT_SLOTS))]),
        compiler_params=pltpu.CompilerParams(
            dimension_semantics=("arbitrary", "arbitrary"),
            vmem_limit_bytes=V7X_VMEM_LIMIT_BYTES),
        name="moe_experts",
    )(tile_expert, n_valid, n_sub, xs, wg, wu, wd)


def _combine_kernel(gdst_ref, glen_ref, gsrc_ref, x_ref, info_ref, gf_ref, ys_hbm, o_ref,
                    ybuf, sem):
    t = pl.program_id(0)
    n_t = pl.num_programs(0)
    T = x_ref.shape[0]
    slot = t % 2
    tables = (gdst_ref, glen_ref, gsrc_ref)

    def fetch(tile, into):
        ybuf[into, 2 * T:SLOT_ROWS, :] = jnp.zeros((SLOT_ROWS - 2 * T, ybuf.shape[2]), F32)
        _start(_tile_runs(tile, tables, ys_hbm, ybuf.at[into], sem.at[into], to_hbm=False))

    @pl.when(t == 0)
    def _():
        fetch(t, slot)

    @pl.when(t + 1 < n_t)
    def _():
        fetch(t + 1, 1 - slot)

    _wait_rows(_tile_rows(t, tables), ys_hbm, ybuf.at[slot], sem.at[slot])
    y = ybuf[slot].astype(BF16)
    info = info_ref[...]
    hit1, hit2 = _slots(info, T)
    weights = (jnp.where(hit1, info[:, 2:3], 0.0) + jnp.where(hit2, info[:, 3:4], 0.0)).astype(BF16)
    o_ref[...] = _rms(x_ref[...] + _dot(weights, y), gf_ref[...])


def _combine(x, info, gf, ys, gdst, glen, gsrc):
    N, D = x.shape
    T = ROUTE_T
    return pl.pallas_call(
        _combine_kernel,
        out_shape=jax.ShapeDtypeStruct((N, D), F32),
        grid_spec=pltpu.PrefetchScalarGridSpec(
            num_scalar_prefetch=3,
            grid=(N // T,),
            in_specs=[pl.BlockSpec((T, D), lambda t, *_: (t, 0)),
                      pl.BlockSpec((T, E_LANES), lambda t, *_: (t, 0)),
                      pl.BlockSpec(gf.shape, lambda t, *_: (0, 0)),
                      pl.BlockSpec(memory_space=pl.ANY)],
            out_specs=pl.BlockSpec((T, D), lambda t, *_: (t, 0)),
            scratch_shapes=[pltpu.VMEM((2, SLOT_ROWS, D), F32),
                            pltpu.SemaphoreType.DMA((2,))]),
        compiler_params=pltpu.CompilerParams(
            dimension_semantics=("arbitrary",),
            vmem_limit_bytes=V7X_VMEM_LIMIT_BYTES),
        name="moe_combine",
    )(gdst, glen, gsrc, x, info, gf, ys)


def _moe(x, h, info, cnt, wg, wu, wd, gf):
    N, D = x.shape
    n_t = N // ROUTE_T
    tm = EXPERT_TM
    glen = cnt[:, 0, :N_EXPERTS].astype(jnp.int32)
    total = jnp.sum(glen, axis=0)
    gpad = (total + tm - 1) // tm * tm
    gend = jnp.cumsum(gpad)
    goff = gend - gpad
    gdst = goff[None, :] + jnp.cumsum(glen, axis=0) - glen
    gsrc = jnp.cumsum(glen, axis=1) - glen
    rows_max = -(-(2 * N + n_t * N_EXPERTS * (RUN_ALIGN - 1) + N_EXPERTS * (tm - RUN_ALIGN)) // tm) * tm
    n_tiles = rows_max // tm
    n_valid = (gend[-1] // tm).reshape(1)
    tile_row = jnp.minimum(jnp.arange(n_tiles, dtype=jnp.int32), n_valid[0] - 1) * tm
    tile_expert = jnp.sum((tile_row[:, None] >= gend[None, :]).astype(jnp.int32), axis=1)
    tile_rows = jnp.clip((goff + total)[tile_expert] - tile_row, 0, tm)
    tile_rows = jnp.where(jnp.arange(n_tiles) < n_valid[0], tile_rows, 0)
    n_sub = (tile_rows + EXPERT_SUB - 1) // EXPERT_SUB
    flat = lambda a: a.reshape(-1).astype(jnp.int32)
    tail0 = jnp.concatenate([goff + total, gend[-1:]])
    tailn = jnp.concatenate([gpad - total, (rows_max - gend[-1:]) // ZERO_PIECE_ROWS])
    xs = _dispatch(h, info, flat(gdst), flat(glen), flat(gsrc), flat(tail0), flat(tailn), rows_max)
    ys = _experts(xs, flat(tile_expert), flat(n_valid), flat(n_sub), wg, wu, wd)
    return _combine(x, info, gf, ys, flat(gdst), flat(glen), flat(gsrc))


def kernel(x, norm1_g, w_in, pool_w, pool_scale, gm_norm_g, gm_ws, gm_b,
           conv_dw_w, conv_dw_b, conv_ln_g, conv_ln_b, conv_pw_w, conv_pw_b,
           w_out, norm2_g, ffn_wg, ffn_wu, ffn_wd,
           moe_router, moe_wg, moe_wu, moe_wd, final_g):
    B, S, D = x.shape
    depth = w_in.shape[0]
    assert depth == 2, "layer 0 is the dense SwiGLU layer, layer 1 the expert layer + final norm"
    row = lambda t: t.reshape(1, -1)
    for l in range(depth):
        pool_bd = jax.scipy.linalg.block_diag(*[pool_w[l, gi] for gi in range(len(POOL_WINDOWS))])
        gm_wcat = jnp.transpose(gm_ws[l], (1, 0, 2)).reshape(GMLP_BLOCK, GMLP_HEADS * GMLP_BLOCK)
        gm_bias = jnp.repeat(gm_b[l].T, GMLP_HEAD_DIM, axis=1)
        j = l // 2
        expert_layer = l % 2 == 1
        route_params = None
        if expert_layer:
            router_p = jnp.pad(moe_router[j], ((0, 0), (0, E_LANES - N_EXPERTS)))
            route_params = (row(norm2_g[l]), router_p)
        mixed = _mixer(x, row(norm1_g[l]), w_in[l].astype(BF16), pool_bd.astype(BF16),
                       row(pool_scale[l]), row(gm_norm_g[l]), gm_wcat, gm_bias,
                       conv_dw_w[l], row(conv_dw_b[l]), row(conv_ln_g[l]), row(conv_ln_b[l]),
                       conv_pw_w[l].astype(BF16), row(conv_pw_b[l]), w_out[l].astype(BF16),
                       route_params=route_params)
        if expert_layer:
            x, h, info, cnt = mixed
            xf = _moe(x.reshape(B * S, D), h, info, cnt, moe_wg[j], moe_wu[j], moe_wd[j],
                      row(final_g))
        else:
            xf = _ffn(mixed.reshape(B * S, D), row(norm2_g[l]), ffn_wg[j], ffn_wu[j], ffn_wd[j])
        x = xf.reshape(B, S, D)
    return x
```
